```python
import jax, jax.numpy as jnp
from jax import lax
import numpy as np

D_MODEL = 1024
BATCH = 8
SEQ = 2048
DEPTH = 4

CHUNK = 64
D_MIX = D_MODEL
HEAD_DIM = 64
D_CONV = D_MIX // 2
D_SPATIAL = D_MIX - D_CONV
N_CONV_GROUPS = D_CONV // HEAD_DIM
N_SP_HEADS = D_SPATIAL // HEAD_DIM
N_MIX_HEADS = N_CONV_GROUPS + N_SP_HEADS
CONV_WIDTH = 3
SP_BLOCK = 128
D_IN = 3 * D_CONV + 2 * D_SPATIAL
N_EXPERTS = 16
N_EXPERT_GROUPS = 4
EXPERTS_PER_GROUP = N_EXPERTS // N_EXPERT_GROUPS
TOP_K = 2
D_EXPERT = D_MODEL // 2
N_ADA = 6
EPS = 1e-6

kernel_name = "hybrid_conv_gmlp_grouped_moe_adaln"


def rms_norm(x, g):
    xf = x.astype(jnp.float32)
    y = xf * lax.rsqrt(jnp.mean(xf * xf, axis=-1, keepdims=True) + EPS)
    return (y * g.astype(jnp.float32)).astype(x.dtype)


def modulate(h, shift, scale):
    return h * (1 + scale[:, None, :]) + shift[:, None, :]


def short_gated_conv(b_gate, c_gate, h, conv_w, conv_b):
    z = c_gate * h
    seq = z.shape[1]
    zp = jnp.pad(z, ((0, 0), (CONV_WIDTH - 1, 0), (0, 0)))
    acc = conv_b
    for k in range(CONV_WIDTH):
        acc = acc + conv_w[k] * zp[:, k:k + seq]
    return b_gate * acc


def spatial_gating(u, v, g_v, w_s, b_s):
    bsz, seq, _ = v.shape
    v = rms_norm(v, g_v)
    vb = v.reshape(bsz, seq // SP_BLOCK, SP_BLOCK, N_SP_HEADS, HEAD_DIM)
    pos = jnp.arange(SP_BLOCK)
    mask = (pos[None, :] // CHUNK) <= (pos[:, None] // CHUNK)
    w = jnp.where(mask[None], w_s, 0)
    s = jnp.einsum('hij,bnjhd->bnihd', w, vb) + b_s.T[None, None, :, :, None]
    return u * s.reshape(bsz, seq, D_SPATIAL)


def grouped_router(h, router_w, router_b):
    logits = h.astype(jnp.float32) @ router_w.astype(jnp.float32)
    scores = jax.nn.sigmoid(logits)
    sel = scores + router_b.astype(jnp.float32)
    n_tok = sel.shape[0]
    grp = sel.reshape(n_tok, N_EXPERT_GROUPS, EXPERTS_PER_GROUP)
    grp_score = lax.top_k(grp, 2)[0].sum(-1)
    top_group = jnp.argmax(grp_score, axis=-1)
    expert_group = jnp.arange(N_EXPERTS) // EXPERTS_PER_GROUP
    in_group = expert_group[None, :] == top_group[:, None]
    masked = jnp.where(in_group, sel, -jnp.inf)
    _, idx = lax.top_k(masked, TOP_K)
    w = jnp.take_along_axis(scores, idx, axis=-1)
    w = w / jnp.sum(w, axis=-1, keepdims=True)
    return jnp.sum(jax.nn.one_hot(idx, N_EXPERTS, dtype=jnp.float32) * w[..., None], axis=1)


def moe(h, router_w, router_b, w_gate, w_up, w_down):
    bsz, seq, d = h.shape
    hf = h.reshape(bsz * seq, d)
    combine = grouped_router(hf, router_w, router_b).astype(h.dtype)
    out = jnp.zeros_like(hf)
    for e in range(N_EXPERTS):
        act = jax.nn.silu(hf @ w_gate[e]) * (hf @ w_up[e])
        out = out + combine[:, e:e + 1] * (act @ w_down[e])
    return out.reshape(bsz, seq, d)


def setup_inputs(seed: int = 0) -> dict:
    key = jax.random.key(seed)
    ks = jax.random.split(key, 20)
    f32 = jnp.float32
    nrm = lambda k, shape, s: jax.random.normal(k, shape, f32) * s
    return {
        "x": nrm(ks[0], (BATCH, SEQ, D_MODEL), 1.0),
        "c": nrm(ks[1], (BATCH, D_MODEL), 1.0),
        "w_ada": nrm(ks[2], (DEPTH, D_MODEL, N_ADA * D_MODEL), 0.2 * D_MODEL ** -0.5),
        "b_ada": nrm(ks[3], (DEPTH, N_ADA * D_MODEL), 0.02),
        "norm1_g": 1.0 + nrm(ks[4], (DEPTH, D_MODEL), 0.02),
        "norm2_g": 1.0 + nrm(ks[5], (DEPTH, D_MODEL), 0.02),
        "w_in": nrm(ks[6], (DEPTH, D_MODEL, D_IN), D_MODEL ** -0.5),
        "conv_w": nrm(ks[7], (DEPTH, CONV_WIDTH, D_CONV), CONV_WIDTH ** -0.5),
        "conv_b": nrm(ks[8], (DEPTH, D_CONV), 0.02),
        "sp_norm_g": 1.0 + nrm(ks[9], (DEPTH, D_SPATIAL), 0.02),
        "sp_w": nrm(ks[10], (DEPTH, N_SP_HEADS, SP_BLOCK, SP_BLOCK), SP_BLOCK ** -0.5),
        "sp_b": 1.0 + nrm(ks[11], (DEPTH, N_SP_HEADS, SP_BLOCK), 0.02),
        "mix_out_g": 1.0 + nrm(ks[12], (DEPTH, D_MIX), 0.02),
        "w_out": nrm(ks[13], (DEPTH, D_MIX, D_MODEL), D_MIX ** -0.5),
        "router_w": nrm(ks[14], (D_MODEL, N_EXPERTS), D_MODEL ** -0.5),
        "router_b": nrm(ks[15], (N_EXPERTS,), 0.01),
        "exp_w_gate": nrm(ks[16], (DEPTH, N_EXPERTS, D_MODEL, D_EXPERT), D_MODEL ** -0.5),
        "exp_w_up": nrm(ks[17], (DEPTH, N_EXPERTS, D_MODEL, D_EXPERT), D_MODEL ** -0.5),
        "exp_w_down": nrm(ks[18], (DEPTH, N_EXPERTS, D_EXPERT, D_MODEL), D_EXPERT ** -0.5),
        "final_g": 1.0 + nrm(ks[19], (D_MODEL,), 0.02),
    }


def reference(x, c, w_ada, b_ada, norm1_g, norm2_g, w_in, conv_w, conv_b, sp_norm_g,
              sp_w, sp_b, mix_out_g, w_out, router_w, router_b, exp_w_gate, exp_w_up,
              exp_w_down, final_g):
    bsz, seq, _ = x.shape
    c_act = jax.nn.silu(c)
    split_pts = [D_CONV, 2 * D_CONV, 3 * D_CONV, 3 * D_CONV + D_SPATIAL]
    for l in range(DEPTH):
        ada = c_act @ w_ada[l] + b_ada[l]
        sh1, sc1, g1, sh2, sc2, g2 = jnp.split(ada, N_ADA, axis=-1)

        h = modulate(rms_norm(x, norm1_g[l]), sh1, sc1)
        proj = h @ w_in[l]
        b_gate, c_gate, hc, u, v = jnp.split(proj, split_pts, axis=-1)
        y_conv = short_gated_conv(b_gate, c_gate, hc, conv_w[l], conv_b[l])
        y_sp = spatial_gating(jax.nn.gelu(u), jax.nn.gelu(v), sp_norm_g[l], sp_w[l], sp_b[l])
        y = jnp.concatenate([y_conv, y_sp], axis=-1).reshape(bsz, seq, N_MIX_HEADS, HEAD_DIM)
        y = rms_norm(y, mix_out_g[l].reshape(N_MIX_HEADS, HEAD_DIM)).reshape(bsz, seq, D_MIX)
        x = x + g1[:, None, :] * (y @ w_out[l])

        h2 = modulate(rms_norm(x, norm2_g[l]), sh2, sc2)
        x = x + g2[:, None, :] * moe(h2, router_w, router_b, exp_w_gate[l], exp_w_up[l], exp_w_down[l])
    return rms_norm(x, final_g)
```

```python
import functools

import jax
import jax.numpy as jnp
from jax import lax
from jax.experimental import pallas as pl
from jax.experimental.pallas import tpu as pltpu

F32 = jnp.float32
BF16 = jnp.bfloat16
U32 = jnp.uint32
I32 = jnp.int32

D_MODEL = 1024
BATCH = 8
SEQ = 2048
DEPTH = 4
N_TOK = BATCH * SEQ
CHUNK = 64
HEAD_DIM = 64
D_CONV = 512
D_SPATIAL = 512
N_SP_HEADS = 8
N_MIX_HEADS = 16
SP_BLOCK = 128
N_EXPERTS = 16
N_GROUPS = 4
PER_GROUP = 4
D_EXPERT = 512
N_ADA = 6
EPS = 1e-6

LANES = 128
SUBLANES = 8
HALF = D_MODEL // 2
ROW_TILES = HALF // LANES

TM = 512
TILES_PER_SEQ = SEQ // TM
ROUTE_CHUNK = 4096
N_CHUNKS = N_TOK // ROUTE_CHUNK
TILES_PER_CHUNK = ROUTE_CHUNK // TM
ROW_TILE = 128
CHUNK_ROWS = 2 * ROUTE_CHUNK + N_EXPERTS * ROW_TILE
TOTAL_ROWS = N_CHUNKS * CHUNK_ROWS
MAX_ROW_TILES = TOTAL_ROWS // ROW_TILE
TD = 1024
TC = 256
VMEM_LIMIT = 58 * 1024 * 1024


def _gelu_tanh(x):
    c = 0.7978845608028654
    return 0.5 * x * (1.0 + jnp.tanh(c * (x + 0.044715 * (x * x * x))))


def _split_bf16(x):
    hi = x.astype(BF16)
    lo = (x - hi.astype(F32)).astype(BF16)
    return hi, lo


def _pack_halves(x_bf16_as_f32):
    bits = lax.bitcast_convert_type(x_bf16_as_f32, U32)
    return (bits[:, :HALF] >> 16) | bits[:, HALF:]


def _unpack_halves(words):
    lo = lax.bitcast_convert_type(words << 16, F32)
    hi = lax.bitcast_convert_type(words & jnp.uint32(0xFFFF0000), F32)
    return lo, hi


def _ada_kernel(c_ref, w_ref, b_ref, o_ref):
    c = c_ref[...]
    ca = c * (1.0 / (1.0 + jnp.exp(-c)))
    c_hi, c_lo = _split_bf16(ca)
    w_hi, w_lo = _split_bf16(w_ref[0])
    acc = jnp.dot(c_hi, w_hi, preferred_element_type=F32)
    acc += jnp.dot(c_lo, w_hi, preferred_element_type=F32)
    acc += jnp.dot(c_hi, w_lo, preferred_element_type=F32)
    o_ref[0] = acc + b_ref[0]


def _ada_table(c, w_ada, b_ada):
    n_blk = N_ADA
    return pl.pallas_call(
        _ada_kernel,
        grid=(DEPTH, n_blk),
        in_specs=[
            pl.BlockSpec((BATCH, D_MODEL), lambda l, n: (0, 0)),
            pl.BlockSpec((1, D_MODEL, D_MODEL), lambda l, n: (l, 0, n)),
            pl.BlockSpec((1, 1, D_MODEL), lambda l, n: (l, 0, n)),
        ],
        out_specs=pl.BlockSpec((1, BATCH, D_MODEL), lambda l, n: (l, 0, n)),
        out_shape=jax.ShapeDtypeStruct((DEPTH, BATCH, N_ADA * D_MODEL), F32),
        name="ada_table",
    )(c, w_ada, b_ada.reshape(DEPTH, 1, N_ADA * D_MODEL))


def _mix_kernel(x_ref, ada_ref, n1g_ref, n2g_ref, win_ref, cw_ref, cb_ref, spg_ref, spw_ref,
                spbias_ref, mog_ref, wout_ref, rwh_ref, rwl_ref, rb_ref, hred_ref, hexp_ref,
                tri_ref, x1_ref, h2p_ref, rt_ref, cnt_ref, z_scr, cnt_scr):
    i = pl.program_id(0)
    x = x_ref[...]
    ada = ada_ref[0]
    sh1, sc1, g1 = ada[0:1], ada[1:2], ada[2:3]
    sh2, sc2 = ada[3:4], ada[4:5]

    rinv = lax.rsqrt(jnp.mean(x * x, axis=-1, keepdims=True) + EPS)
    h = (x * rinv * n1g_ref[...]) * (1.0 + sc1) + sh1
    hb = h.astype(BF16)

    def proj(k):
        return jnp.dot(hb, win_ref[:, k * 512:(k + 1) * 512], preferred_element_type=F32)

    @pl.when(i % TILES_PER_SEQ == 0)
    def _():
        z_scr[pl.ds(0, SUBLANES), :] = jnp.zeros((SUBLANES, D_CONV), F32)

    z = proj(1) * proj(2)
    z_scr[pl.ds(SUBLANES, TM), :] = z
    z1 = z_scr[pl.ds(SUBLANES - 1, TM), :]
    z2 = z_scr[pl.ds(SUBLANES - 2, TM), :]
    cw = cw_ref[...]
    acc = cb_ref[...] + cw[0:1] * z2
    acc = acc + cw[1:2] * z1
    acc = acc + cw[2:3] * z
    y_conv = proj(0) * acc
    z_scr[pl.ds(0, SUBLANES), :] = z[TM - SUBLANES:, :]

    ug = _gelu_tanh(proj(3))
    vg = _gelu_tanh(proj(4))
    vr = lax.rsqrt(jnp.mean(vg * vg, axis=-1, keepdims=True) + EPS)
    vn = (vg * vr * spg_ref[...]).astype(BF16)
    lane = lax.broadcasted_iota(I32, (1, LANES), 1)
    first_head = lane < HEAD_DIM
    zero = jnp.zeros((), BF16)
    s_rows = []
    for blk in range(TM // SP_BLOCK):
        cols = []
        for p in range(N_SP_HEADS // 2):
            vv = vn[blk * SP_BLOCK:(blk + 1) * SP_BLOCK, p * LANES:(p + 1) * LANES]
            rhs = jnp.concatenate([jnp.where(first_head, vv, zero), jnp.where(first_head, zero, vv)], axis=0)
            cols.append(jnp.dot(spw_ref[p], rhs, preferred_element_type=F32))
        s_rows.append(jnp.concatenate(cols, axis=-1) + spbias_ref[...])
    y_sp = ug * jnp.concatenate(s_rows, axis=0)

    y = jnp.concatenate([y_conv, y_sp], axis=-1)
    ms = jnp.dot((y * y).astype(BF16), hred_ref[...], preferred_element_type=F32)
    r_hi, r_lo = _split_bf16(lax.rsqrt(ms + EPS))
    scale = jnp.dot(jnp.concatenate([r_hi, r_lo], axis=-1), hexp_ref[...], preferred_element_type=F32)
    yn = (y * scale * mog_ref[...]).astype(BF16)
    x1 = x + g1 * jnp.dot(yn, wout_ref[...], preferred_element_type=F32)
    x1_ref[...] = x1

    r2 = lax.rsqrt(jnp.mean(x1 * x1, axis=-1, keepdims=True) + EPS)
    h2 = (x1 * r2 * n2g_ref[...]) * (1.0 + sc2) + sh2
    h2_hi, h2_lo = _split_bf16(h2)
    packed = _pack_halves(h2_hi.astype(F32))
    for j in range(ROW_TILES):
        h2p_ref[:, j, :, :] = packed[:, j * LANES:(j + 1) * LANES].reshape(TM // SUBLANES, SUBLANES, LANES)

    logits = jnp.dot(h2_hi, rwh_ref[...], preferred_element_type=F32)
    logits += jnp.dot(h2_lo, rwh_ref[...], preferred_element_type=F32)
    logits += jnp.dot(h2_hi, rwl_ref[...], preferred_element_type=F32)
    lg = logits.T[0:N_EXPERTS, :]
    scores = 1.0 / (1.0 + jnp.exp(-lg))
    sel = scores + rb_ref[...]

    gs = []
    for g in range(N_GROUPS):
        a, b, c, d = (sel[g * PER_GROUP + k:g * PER_GROUP + k + 1, :] for k in range(PER_GROUP))
        hi1, lo1 = jnp.maximum(a, b), jnp.minimum(a, b)
        hi2, lo2 = jnp.maximum(c, d), jnp.minimum(c, d)
        gs.append(jnp.maximum(hi1, hi2) + jnp.maximum(jnp.minimum(hi1, hi2), jnp.maximum(lo1, lo2)))
    best = gs[0]
    top_group = jnp.zeros((1, TM), I32)
    for g in range(1, N_GROUPS):
        better = gs[g] > best
        best = jnp.where(better, gs[g], best)
        top_group = jnp.where(better, g, top_group)

    eid_i = lax.broadcasted_iota(I32, (N_EXPERTS, TM), 0)
    eid = eid_i.astype(F32)
    neg_inf = jnp.float32(-jnp.inf)
    masked = jnp.where((eid_i // PER_GROUP) == top_group, sel, neg_inf)
    m1 = jnp.max(masked, axis=0, keepdims=True)
    i1 = jnp.min(jnp.where(masked == m1, eid, float(N_EXPERTS)), axis=0, keepdims=True)
    masked2 = jnp.where(eid == i1, neg_inf, masked)
    m2 = jnp.max(masked2, axis=0, keepdims=True)
    i2 = jnp.min(jnp.where(masked2 == m2, eid, float(N_EXPERTS)), axis=0, keepdims=True)
    pick1 = eid == i1
    pick2 = eid == i2
    w1 = jnp.sum(jnp.where(pick1, scores, 0.0), axis=0, keepdims=True)
    w2 = jnp.sum(jnp.where(pick2, scores, 0.0), axis=0, keepdims=True)
    wsum = w1 + w2
    c1 = w1 / wsum
    c2 = w2 / wsum

    @pl.when(i % TILES_PER_CHUNK == 0)
    def _():
        cnt_scr[...] = jnp.zeros((N_EXPERTS, LANES), F32)

    assigned = jnp.where(pick1 | pick2, 1.0, 0.0)
    before = jnp.dot(assigned.astype(BF16), tri_ref[...], preferred_element_type=F32)
    carry = cnt_scr[...]
    rank = before + carry[:, 0:1]
    rank1 = jnp.sum(jnp.where(pick1, rank, 0.0), axis=0, keepdims=True)
    rank2 = jnp.sum(jnp.where(pick2, rank, 0.0), axis=0, keepdims=True)
    new_cnt = carry + jnp.sum(assigned, axis=1, keepdims=True)
    cnt_scr[...] = new_cnt
    cnt_ref[0] = new_cnt
    unused = jnp.zeros((1, TM), F32)
    for r, row in enumerate((i1, i2, c1, c2, rank1, rank2, unused, unused)):
        rt_ref[r:r + 1, :] = row


def _mix_call(x2d, ada_l, consts):
    n_tiles = N_TOK // TM
    full = lambda shape: pl.BlockSpec(shape, lambda i: (0,) * len(shape))
    in_specs = [
        pl.BlockSpec((TM, D_MODEL), lambda i: (i, 0)),
        pl.BlockSpec((1, N_ADA, D_MODEL), lambda i: (i // TILES_PER_SEQ, 0, 0)),
        full((1, D_MODEL)), full((1, D_MODEL)),
        full((D_MODEL, 5 * 512)),
        full((3, D_CONV)), full((1, D_CONV)), full((1, D_SPATIAL)),
        full((N_SP_HEADS // 2, SP_BLOCK, 2 * SP_BLOCK)),
        full((SP_BLOCK, D_SPATIAL)),
        full((1, D_MODEL)),
        full((D_MODEL, D_MODEL)),
        full((D_MODEL, LANES)), full((D_MODEL, LANES)), full((N_EXPERTS, 1)),
        full((D_MODEL, LANES)), full((2 * LANES, D_MODEL)),
        full((TM, TM)),
    ]
    out_specs = [
        pl.BlockSpec((TM, D_MODEL), lambda i: (i, 0)),
        pl.BlockSpec((TM // SUBLANES, ROW_TILES, SUBLANES, LANES), lambda i: (i, 0, 0, 0)),
        pl.BlockSpec((SUBLANES, TM), lambda i: (0, i)),
        pl.BlockSpec((1, N_EXPERTS, LANES), lambda i: (i, 0, 0)),
    ]
    out_shape = [
        jax.ShapeDtypeStruct((N_TOK, D_MODEL), F32),
        jax.ShapeDtypeStruct((N_TOK // SUBLANES, ROW_TILES, SUBLANES, LANES), U32),
        jax.ShapeDtypeStruct((SUBLANES, N_TOK), F32),
        jax.ShapeDtypeStruct((n_tiles, N_EXPERTS, LANES), F32),
    ]
    return pl.pallas_call(
        _mix_kernel,
        grid=(n_tiles,),
        in_specs=in_specs,
        out_specs=out_specs,
        out_shape=out_shape,
        scratch_shapes=[pltpu.VMEM((TM + SUBLANES, D_CONV), F32), pltpu.VMEM((N_EXPERTS, LANES), F32)],
        compiler_params=pltpu.CompilerParams(
            dimension_semantics=("arbitrary",), vmem_limit_bytes=VMEM_LIMIT),
        name="mix",
    )(x2d, ada_l, *consts)


def _row_addr(t):
    return (t // SUBLANES) * (ROW_TILES * SUBLANES) + t % SUBLANES


def _dispatch_kernel(a0_ref, a1_ref, h2p_ref, xs_ref):
    c = pl.program_id(0)
    s = pl.program_id(1)

    @pl.when(s == 0)
    def _():
        xs_ref[...] = jnp.zeros(xs_ref.shape, U32)

    base = c * ROUTE_CHUNK + s * TD

    def body(t8, carry):
        for k in range(SUBLANES):
            row = h2p_ref[pl.ds(t8 * (ROW_TILES * SUBLANES) + k, ROW_TILES, stride=SUBLANES), :]
            t = base + t8 * SUBLANES + k
            xs_ref[pl.ds(a0_ref[t], ROW_TILES, stride=SUBLANES), :] = row
            xs_ref[pl.ds(a1_ref[t], ROW_TILES, stride=SUBLANES), :] = row
        return carry

    lax.fori_loop(0, TD // SUBLANES, body, 0)


def _dispatch_call(a0, a1, h2p_2d):
    steps = ROUTE_CHUNK // TD
    return pl.pallas_call(
        _dispatch_kernel,
        grid_spec=pltpu.PrefetchScalarGridSpec(
            num_scalar_prefetch=2,
            grid=(N_CHUNKS, steps),
            in_specs=[pl.BlockSpec((TD * ROW_TILES, LANES), lambda c, s, a0, a1: (c * steps + s, 0))],
            out_specs=pl.BlockSpec((CHUNK_ROWS * ROW_TILES, LANES), lambda c, s, a0, a1: (c, 0)),
        ),
        out_shape=jax.ShapeDtypeStruct((TOTAL_ROWS * ROW_TILES, LANES), U32),
        compiler_params=pltpu.CompilerParams(
            dimension_semantics=("arbitrary", "arbitrary"), vmem_limit_bytes=VMEM_LIMIT),
        name="dispatch",
    )(a0, a1, h2p_2d)


def _expert_kernel(blk_ref, exp_ref, nv_ref, xs_ref, wg_ref, wu_ref, wd_ref, ys_ref):
    i = pl.program_id(0)

    @pl.when(i < nv_ref[0])
    def _():
        los, his = [], []
        for j in range(ROW_TILES):
            lo, hi = _unpack_halves(xs_ref[:, j, :, :].reshape(ROW_TILE, LANES))
            los.append(lo)
            his.append(hi)
        xb = jnp.concatenate(los + his, axis=-1).astype(BF16)
        g = jnp.dot(xb, wg_ref[0], preferred_element_type=F32)
        u = jnp.dot(xb, wu_ref[0], preferred_element_type=F32)
        act = (g * (1.0 / (1.0 + jnp.exp(-g))) * u).astype(BF16)
        y = jnp.dot(act, wd_ref[0], preferred_element_type=F32)
        packed = _pack_halves(y.astype(BF16).astype(F32))
        for j in range(ROW_TILES):
            ys_ref[:, j, :, :] = packed[:, j * LANES:(j + 1) * LANES].reshape(
                ROW_TILE // SUBLANES, SUBLANES, LANES)


def _expert_call(tile_blk, tile_exp, n_valid, xs4, wg, wu, wd):
    rows_blk = (ROW_TILE // SUBLANES, ROW_TILES, SUBLANES, LANES)
    return pl.pallas_call(
        _expert_kernel,
        grid_spec=pltpu.PrefetchScalarGridSpec(
            num_scalar_prefetch=3,
            grid=(MAX_ROW_TILES,),
            in_specs=[
                pl.BlockSpec(rows_blk, lambda i, tb, te, nv: (tb[i], 0, 0, 0)),
                pl.BlockSpec((1, D_MODEL, D_EXPERT), lambda i, tb, te, nv: (te[i], 0, 0)),
                pl.BlockSpec((1, D_MODEL, D_EXPERT), lambda i, tb, te, nv: (te[i], 0, 0)),
                pl.BlockSpec((1, D_EXPERT, D_MODEL), lambda i, tb, te, nv: (te[i], 0, 0)),
            ],
            out_specs=pl.BlockSpec(rows_blk, lambda i, tb, te, nv: (tb[i], 0, 0, 0)),
        ),
        out_shape=jax.ShapeDtypeStruct((TOTAL_ROWS // SUBLANES, ROW_TILES, SUBLANES, LANES), U32),
        compiler_params=pltpu.CompilerParams(
            dimension_semantics=("arbitrary",), vmem_limit_bytes=VMEM_LIMIT),
        name="experts",
    )(tile_blk, tile_exp, n_valid, xs4, wg, wu, wd)


def _combine_kernel(a0_ref, a1_ref, ys_ref, x1_ref, w_ref, ada_ref, fg_ref, out_ref, r0_scr, r1_scr,
                    *, final):
    c = pl.program_id(0)
    s = pl.program_id(1)
    base = c * ROUTE_CHUNK + s * TC

    def body(t8, carry):
        for k in range(SUBLANES):
            t = base + t8 * SUBLANES + k
            dst = pl.ds(t8 * (ROW_TILES * SUBLANES) + k, ROW_TILES, stride=SUBLANES)
            r0_scr[dst, :] = ys_ref[pl.ds(a0_ref[t], ROW_TILES, stride=SUBLANES), :]
            r1_scr[dst, :] = ys_ref[pl.ds(a1_ref[t], ROW_TILES, stride=SUBLANES), :]
        return carry

    lax.fori_loop(0, TC // SUBLANES, body, 0)

    w = w_ref[...]
    w0 = w[:, 0:1]
    w1 = w[:, 1:2]
    g2 = ada_ref[0][5:6]
    n_grp = TC // SUBLANES
    los, his = [], []
    for j in range(ROW_TILES):
        parts0 = [r0_scr[pl.ds((tb * ROW_TILES + j) * SUBLANES, SUBLANES), :] for tb in range(n_grp)]
        parts1 = [r1_scr[pl.ds((tb * ROW_TILES + j) * SUBLANES, SUBLANES), :] for tb in range(n_grp)]
        lo0, hi0 = _unpack_halves(jnp.concatenate(parts0, axis=0))
        lo1, hi1 = _unpack_halves(jnp.concatenate(parts1, axis=0))
        los.append(w0 * lo0 + w1 * lo1)
        his.append(w0 * hi0 + w1 * hi1)
    moe = jnp.concatenate(los + his, axis=-1)
    x2 = x1_ref[...] + g2 * moe
    if final:
        r = lax.rsqrt(jnp.mean(x2 * x2, axis=-1, keepdims=True) + EPS)
        x2 = x2 * r * fg_ref[...]
    out_ref[...] = x2


def _combine_call(a0, a1, ys_2d, x1, wcol, ada_l, final_g, final):
    steps = ROUTE_CHUNK // TC
    tok = lambda c, s, a0, a1: (c * steps + s, 0)
    return pl.pallas_call(
        functools.partial(_combine_kernel, final=final),
        grid_spec=pltpu.PrefetchScalarGridSpec(
            num_scalar_prefetch=2,
            grid=(N_CHUNKS, steps),
            in_specs=[
                pl.BlockSpec((CHUNK_ROWS * ROW_TILES, LANES), lambda c, s, a0, a1: (c, 0)),
                pl.BlockSpec((TC, D_MODEL), tok),
                pl.BlockSpec((TC, 2), tok),
                pl.BlockSpec((1, N_ADA, D_MODEL),
                             lambda c, s, a0, a1: ((c * ROUTE_CHUNK + s * TC) // SEQ, 0, 0)),
                pl.BlockSpec((1, D_MODEL), lambda c, s, a0, a1: (0, 0)),
            ],
            out_specs=pl.BlockSpec((TC, D_MODEL), tok),
            scratch_shapes=[pltpu.VMEM((TC * ROW_TILES, LANES), U32),
                            pltpu.VMEM((TC * ROW_TILES, LANES), U32)],
        ),
        out_shape=jax.ShapeDtypeStruct((N_TOK, D_MODEL), F32),
        compiler_params=pltpu.CompilerParams(
            dimension_semantics=("arbitrary", "arbitrary"), vmem_limit_bytes=VMEM_LIMIT),
        name="combine_final" if final else "combine",
    )(a0, a1, ys_2d, x1, wcol, ada_l, final_g)


def _routing_tables(rt, cnt):
    e0 = rt[0].astype(I32)
    e1 = rt[1].astype(I32)
    rank0 = rt[4].astype(I32)
    rank1 = rt[5].astype(I32)
    counts = cnt[TILES_PER_CHUNK - 1::TILES_PER_CHUNK, :, 0].astype(I32)
    n_tiles = (counts + ROW_TILE - 1) // ROW_TILE
    tile_off = jnp.cumsum(n_tiles, axis=1) - n_tiles
    row_off = tile_off * ROW_TILE
    chunk_id = jnp.arange(N_TOK, dtype=I32) // ROUTE_CHUNK
    slot0 = row_off[chunk_id, e0] + rank0
    slot1 = row_off[chunk_id, e1] + rank1
    a0 = _row_addr(slot0)
    a1 = _row_addr(slot1)

    nt_flat = n_tiles.T.reshape(-1)
    ends = jnp.cumsum(nt_flat)
    starts = ends - nt_flat
    total = ends[-1]
    step = jnp.minimum(jnp.arange(MAX_ROW_TILES, dtype=I32), total - 1)
    seg = jnp.searchsorted(ends, step, side="right").astype(I32)
    k = step - starts[seg]
    e = seg // N_CHUNKS
    c = seg % N_CHUNKS
    tile_blk = c * (CHUNK_ROWS // ROW_TILE) + tile_off[c, e] + k
    return a0, a1, tile_blk.astype(I32), e.astype(I32), total.reshape(1).astype(I32)


def _layer_consts(l, norm1_g, norm2_g, w_in, conv_w, conv_b, sp_norm_g, sp_w, sp_b, mix_out_g,
                  w_out, shared):
    pos = jnp.arange(SP_BLOCK)
    mask = (pos[None, :] // CHUNK) <= (pos[:, None] // CHUNK)
    w = jnp.where(mask[None], sp_w[l], 0.0)
    w_pairs = jnp.concatenate([w[0::2], w[1::2]], axis=-1).astype(BF16)
    sp_bias = jnp.repeat(sp_b[l].T, HEAD_DIM, axis=1)
    return (
        norm1_g[l].reshape(1, D_MODEL), norm2_g[l].reshape(1, D_MODEL),
        w_in[l].astype(BF16),
        conv_w[l], conv_b[l].reshape(1, D_CONV), sp_norm_g[l].reshape(1, D_SPATIAL),
        w_pairs, sp_bias,
        mix_out_g[l].reshape(1, D_MODEL),
        w_out[l].astype(BF16),
    ) + shared


def kernel(x, c, w_ada, b_ada, norm1_g, norm2_g, w_in, conv_w, conv_b, sp_norm_g, sp_w, sp_b, mix_out_g, w_out, router_w, router_b, exp_w_gate, exp_w_up, exp_w_down, final_g):
    ada = _ada_table(c, w_ada, b_ada).reshape(DEPTH, BATCH, N_ADA, D_MODEL)

    rw = jnp.pad(router_w, ((0, 0), (0, LANES - N_EXPERTS)))
    rw_hi = rw.astype(BF16)
    rw_lo = (rw - rw_hi.astype(F32)).astype(BF16)
    head_of = jnp.arange(D_MODEL) // HEAD_DIM
    hred = (head_of[:, None] == jnp.arange(LANES)[None, :]).astype(F32) / HEAD_DIM
    hexp_half = (jnp.arange(LANES)[:, None] == head_of[None, :]).astype(F32)
    hexp = jnp.concatenate([hexp_half, hexp_half], axis=0)
    tri = (jnp.arange(TM)[:, None] < jnp.arange(TM)[None, :])
    shared = (rw_hi, rw_lo, router_b.reshape(N_EXPERTS, 1), hred.astype(BF16), hexp.astype(BF16),
              tri.astype(BF16))

    wg = exp_w_gate.astype(BF16)
    wu = exp_w_up.astype(BF16)
    wd = exp_w_down.astype(BF16)
    fg = final_g.reshape(1, D_MODEL)

    xc = x.reshape(N_TOK, D_MODEL)
    for l in range(DEPTH):
        consts = _layer_consts(l, norm1_g, norm2_g, w_in, conv_w, conv_b, sp_norm_g, sp_w, sp_b,
                               mix_out_g, w_out, shared)
        x1, h2p, rt, cnt = _mix_call(xc, ada[l], consts)
        a0, a1, tile_blk, tile_exp, n_valid = _routing_tables(rt, cnt)
        xs = _dispatch_call(a0, a1, h2p.reshape(N_TOK * ROW_TILES, LANES))
        ys = _expert_call(tile_blk, tile_exp, n_valid,
                          xs.reshape(TOTAL_ROWS // SUBLANES, ROW_TILES, SUBLANES, LANES),
                          wg[l], wu[l], wd[l])
        wcol = rt[2:4].T
        xc = _combine_call(a0, a1, ys.reshape(TOTAL_ROWS * ROW_TILES, LANES), x1, wcol, ada[l], fg,
                           final=(l == DEPTH - 1))
    return xc.reshape(BATCH, SEQ, D_MODEL)
```

```python
import functools

import jax
import jax.numpy as jnp
from jax import lax
from jax.experimental import pallas as pl
from jax.experimental.pallas import tpu as pltpu

F32 = jnp.float32
BF16 = jnp.bfloat16
U32 = jnp.uint32
I32 = jnp.int32

D_MODEL = 1024
BATCH = 8
SEQ = 2048
DEPTH = 4
N_TOK = BATCH * SEQ
CHUNK = 64
HEAD_DIM = 64
D_CONV = 512
D_SPATIAL = 512
N_SP_HEADS = 8
N_MIX_HEADS = 16
SP_BLOCK = 128
N_EXPERTS = 16
N_GROUPS = 4
PER_GROUP = 4
D_EXPERT = 512
N_ADA = 6
EPS = 1e-6

LANES = 128
SUBLANES = 8
HALF = D_MODEL // 2
ROW_TILES = HALF // LANES

TM = 512
TILES_PER_SEQ = SEQ // TM
ROUTE_CHUNK = 4096
N_CHUNKS = N_TOK // ROUTE_CHUNK
TILES_PER_CHUNK = ROUTE_CHUNK // TM
ROW_TILE = 128
CHUNK_ROWS = 2 * ROUTE_CHUNK + N_EXPERTS * ROW_TILE
TOTAL_ROWS = N_CHUNKS * CHUNK_ROWS
MAX_ROW_TILES = TOTAL_ROWS // ROW_TILE
TD = 1024
TC = 256
VMEM_LIMIT = 58 * 1024 * 1024


def _gelu_tanh(x):
    c = 0.7978845608028654
    return 0.5 * x * (1.0 + jnp.tanh(c * (x + 0.044715 * (x * x * x))))


def _split_bf16(x):
    hi = x.astype(BF16)
    lo = (x - hi.astype(F32)).astype(BF16)
    return hi, lo


def _pack_halves(x):
    return pltpu.pack_elementwise([x[:, :HALF], x[:, HALF:]], packed_dtype=BF16)


def _unpack_halves(words):
    lo = pltpu.unpack_elementwise(words, index=0, packed_dtype=BF16, unpacked_dtype=F32)
    hi = pltpu.unpack_elementwise(words, index=1, packed_dtype=BF16, unpacked_dtype=F32)
    return lo, hi


def _ada_kernel(c_ref, w_ref, b_ref, o_ref):
    c = c_ref[...]
    ca = c * (1.0 / (1.0 + jnp.exp(-c)))
    c_hi, c_lo = _split_bf16(ca)
    w_hi, w_lo = _split_bf16(w_ref[0])
    acc = jnp.dot(c_hi, w_hi, preferred_element_type=F32)
    acc += jnp.dot(c_lo, w_hi, preferred_element_type=F32)
    acc += jnp.dot(c_hi, w_lo, preferred_element_type=F32)
    o_ref[0] = acc + b_ref[0]


def _ada_table(c, w_ada, b_ada):
    n_blk = N_ADA
    return pl.pallas_call(
        _ada_kernel,
        grid=(DEPTH, n_blk),
        in_specs=[
            pl.BlockSpec((BATCH, D_MODEL), lambda l, n: (0, 0)),
            pl.BlockSpec((1, D_MODEL, D_MODEL), lambda l, n: (l, 0, n)),
            pl.BlockSpec((1, 1, D_MODEL), lambda l, n: (l, 0, n)),
        ],
        out_specs=pl.BlockSpec((1, BATCH, D_MODEL), lambda l, n: (l, 0, n)),
        out_shape=jax.ShapeDtypeStruct((DEPTH, BATCH, N_ADA * D_MODEL), F32),
        name="ada_table",
    )(c, w_ada, b_ada.reshape(DEPTH, 1, N_ADA * D_MODEL))


def _mix_kernel(x_ref, ada_ref, n1g_ref, n2g_ref, win_ref, cw_ref, cb_ref, spg_ref, spw_ref,
                spbias_ref, mog_ref, wout_ref, rwh_ref, rwl_ref, rb_ref, hred_ref, hexp_ref,
                tri_ref, x1_ref, h2p_ref, rt_ref, cnt_ref, z_scr, cnt_scr):
    i = pl.program_id(0)
    x = x_ref[...]
    ada = ada_ref[0]
    sh1, sc1, g1 = ada[0:1], ada[1:2], ada[2:3]
    sh2, sc2 = ada[3:4], ada[4:5]

    rinv = lax.rsqrt(jnp.mean(x * x, axis=-1, keepdims=True) + EPS)
    h = (x * rinv * n1g_ref[...]) * (1.0 + sc1) + sh1
    hb = h.astype(BF16)

    def proj(k):
        return jnp.dot(hb, win_ref[:, k * 512:(k + 1) * 512], preferred_element_type=F32)

    @pl.when(i % TILES_PER_SEQ == 0)
    def _():
        z_scr[pl.ds(0, SUBLANES), :] = jnp.zeros((SUBLANES, D_CONV), F32)

    z = proj(1) * proj(2)
    z_scr[pl.ds(SUBLANES, TM), :] = z
    z1 = z_scr[pl.ds(SUBLANES - 1, TM), :]
    z2 = z_scr[pl.ds(SUBLANES - 2, TM), :]
    cw = cw_ref[...]
    acc = cb_ref[...] + cw[0:1] * z2
    acc = acc + cw[1:2] * z1
    acc = acc + cw[2:3] * z
    y_conv = proj(0) * acc
    z_scr[pl.ds(0, SUBLANES), :] = z[TM - SUBLANES:, :]

    ug = _gelu_tanh(proj(3))
    vg = _gelu_tanh(proj(4))
    vr = lax.rsqrt(jnp.mean(vg * vg, axis=-1, keepdims=True) + EPS)
    vn = (vg * vr * spg_ref[...]).astype(BF16)
    lane = lax.broadcasted_iota(I32, (1, LANES), 1)
    first_head = lane < HEAD_DIM
    zero = jnp.zeros((), BF16)
    s_rows = []
    for blk in range(TM // SP_BLOCK):
        cols = []
        for p in range(N_SP_HEADS // 2):
            vv = vn[blk * SP_BLOCK:(blk + 1) * SP_BLOCK, p * LANES:(p + 1) * LANES]
            rhs = jnp.concatenate([jnp.where(first_head, vv, zero), jnp.where(first_head, zero, vv)], axis=0)
            cols.append(jnp.dot(spw_ref[p], rhs, preferred_element_type=F32))
        s_rows.append(jnp.concatenate(cols, axis=-1) + spbias_ref[...])
    y_sp = ug * jnp.concatenate(s_rows, axis=0)

    y = jnp.concatenate([y_conv, y_sp], axis=-1)
    ms = jnp.dot((y * y).astype(BF16), hred_ref[...], preferred_element_type=F32)
    r_hi, r_lo = _split_bf16(lax.rsqrt(ms + EPS))
    scale = jnp.dot(jnp.concatenate([r_hi, r_lo], axis=-1), hexp_ref[...], preferred_element_type=F32)
    yn = (y * scale * mog_ref[...]).astype(BF16)
    x1 = x + g1 * jnp.dot(yn, wout_ref[...], preferred_element_type=F32)
    x1_ref[...] = x1

    r2 = lax.rsqrt(jnp.mean(x1 * x1, axis=-1, keepdims=True) + EPS)
    h2 = (x1 * r2 * n2g_ref[...]) * (1.0 + sc2) + sh2
    h2_hi, h2_lo = _split_bf16(h2)
    packed = _pack_halves(h2_hi.astype(F32))
    for j in range(ROW_TILES):
        h2p_ref[:, j, :, :] = packed[:, j * LANES:(j + 1) * LANES].reshape(TM // SUBLANES, SUBLANES, LANES)

    logits = jnp.dot(h2_hi, rwh_ref[...], preferred_element_type=F32)
    logits += jnp.dot(h2_lo, rwh_ref[...], preferred_element_type=F32)
    logits += jnp.dot(h2_hi, rwl_ref[...], preferred_element_type=F32)
    lg = logits.T[0:N_EXPERTS, :]
    scores = 1.0 / (1.0 + jnp.exp(-lg))
    sel = scores + rb_ref[...]

    gs = []
    for g in range(N_GROUPS):
        a, b, c, d = (sel[g * PER_GROUP + k:g * PER_GROUP + k + 1, :] for k in range(PER_GROUP))
        hi1, lo1 = jnp.maximum(a, b), jnp.minimum(a, b)
        hi2, lo2 = jnp.maximum(c, d), jnp.minimum(c, d)
        gs.append(jnp.maximum(hi1, hi2) + jnp.maximum(jnp.minimum(hi1, hi2), jnp.maximum(lo1, lo2)))
    best = gs[0]
    top_group = jnp.zeros((1, TM), I32)
    for g in range(1, N_GROUPS):
        better = gs[g] > best
        best = jnp.where(better, gs[g], best)
        top_group = jnp.where(better, g, top_group)

    eid_i = lax.broadcasted_iota(I32, (N_EXPERTS, TM), 0)
    eid = eid_i.astype(F32)
    neg_inf = jnp.float32(-jnp.inf)
    masked = jnp.where((eid_i // PER_GROUP) == top_group, sel, neg_inf)
    m1 = jnp.max(masked, axis=0, keepdims=True)
    i1 = jnp.min(jnp.where(masked == m1, eid, float(N_EXPERTS)), axis=0, keepdims=True)
    masked2 = jnp.where(eid == i1, neg_inf, masked)
    m2 = jnp.max(masked2, axis=0, keepdims=True)
    i2 = jnp.min(jnp.where(masked2 == m2, eid, float(N_EXPERTS)), axis=0, keepdims=True)
    pick1 = eid == i1
    pick2 = eid == i2
    w1 = jnp.sum(jnp.where(pick1, scores, 0.0), axis=0, keepdims=True)
    w2 = jnp.sum(jnp.where(pick2, scores, 0.0), axis=0, keepdims=True)
    wsum = w1 + w2
    c1 = w1 / wsum
    c2 = w2 / wsum

    @pl.when(i % TILES_PER_CHUNK == 0)
    def _():
        cnt_scr[...] = jnp.zeros((N_EXPERTS, LANES), F32)

    assigned = jnp.where(pick1 | pick2, 1.0, 0.0)
    before = jnp.dot(assigned.astype(BF16), tri_ref[...], preferred_element_type=F32)
    carry = cnt_scr[...]
    rank = before + carry[:, 0:1]
    rank1 = jnp.sum(jnp.where(pick1, rank, 0.0), axis=0, keepdims=True)
    rank2 = jnp.sum(jnp.where(pick2, rank, 0.0), axis=0, keepdims=True)
    new_cnt = carry + jnp.sum(assigned, axis=1, keepdims=True)
    cnt_scr[...] = new_cnt
    cnt_ref[0] = new_cnt
    unused = jnp.zeros((1, TM), F32)
    for r, row in enumerate((i1, i2, c1, c2, rank1, rank2, unused, unused)):
        rt_ref[r:r + 1, :] = row


def _mix_call(x2d, ada_l, consts):
    n_tiles = N_TOK // TM
    full = lambda shape: pl.BlockSpec(shape, lambda i: (0,) * len(shape))
    in_specs = [
        pl.BlockSpec((TM, D_MODEL), lambda i: (i, 0)),
        pl.BlockSpec((1, N_ADA, D_MODEL), lambda i: (i // TILES_PER_SEQ, 0, 0)),
        full((1, D_MODEL)), full((1, D_MODEL)),
        full((D_MODEL, 5 * 512)),
        full((3, D_CONV)), full((1, D_CONV)), full((1, D_SPATIAL)),
        full((N_SP_HEADS // 2, SP_BLOCK, 2 * SP_BLOCK)),
        full((SP_BLOCK, D_SPATIAL)),
        full((1, D_MODEL)),
        full((D_MODEL, D_MODEL)),
        full((D_MODEL, LANES)), full((D_MODEL, LANES)), full((N_EXPERTS, 1)),
        full((D_MODEL, LANES)), full((2 * LANES, D_MODEL)),
        full((TM, TM)),
    ]
    out_specs = [
        pl.BlockSpec((TM, D_MODEL), lambda i: (i, 0)),
        pl.BlockSpec((TM // SUBLANES, ROW_TILES, SUBLANES, LANES), lambda i: (i, 0, 0, 0)),
        pl.BlockSpec((SUBLANES, TM), lambda i: (0, i)),
        pl.BlockSpec((1, N_EXPERTS, LANES), lambda i: (i, 0, 0)),
    ]
    out_shape = [
        jax.ShapeDtypeStruct((N_TOK, D_MODEL), F32),
        jax.ShapeDtypeStruct((N_TOK // SUBLANES, ROW_TILES, SUBLANES, LANES), U32),
        jax.ShapeDtypeStruct((SUBLANES, N_TOK), F32),
        jax.ShapeDtypeStruct((n_tiles, N_EXPERTS, LANES), F32),
    ]
    return pl.pallas_call(
        _mix_kernel,
        grid=(n_tiles,),
        in_specs=in_specs,
        out_specs=out_specs,
        out_shape=out_shape,
        scratch_shapes=[pltpu.VMEM((TM + SUBLANES, D_CONV), F32), pltpu.VMEM((N_EXPERTS, LANES), F32)],
        compiler_params=pltpu.CompilerParams(
            dimension_semantics=("arbitrary",), vmem_limit_bytes=VMEM_LIMIT),
        name="mix",
    )(x2d, ada_l, *consts)


def _row_addr(t):
    return (t // SUBLANES) * (ROW_TILES * SUBLANES) + t % SUBLANES


def _dispatch_kernel(a0_ref, a1_ref, h2p_ref, xs_ref):
    c = pl.program_id(0)
    s = pl.program_id(1)

    @pl.when(s == 0)
    def _():
        xs_ref[...] = jnp.zeros(xs_ref.shape, U32)

    base = c * ROUTE_CHUNK + s * TD

    def body(t8, carry):
        for k in range(SUBLANES):
            row = h2p_ref[pl.ds(t8 * (ROW_TILES * SUBLANES) + k, ROW_TILES, stride=SUBLANES), :]
            t = base + t8 * SUBLANES + k
            xs_ref[pl.ds(a0_ref[t], ROW_TILES, stride=SUBLANES), :] = row
            xs_ref[pl.ds(a1_ref[t], ROW_TILES, stride=SUBLANES), :] = row
        return carry

    lax.fori_loop(0, TD // SUBLANES, body, 0)


def _dispatch_call(a0, a1, h2p_2d):
    steps = ROUTE_CHUNK // TD
    return pl.pallas_call(
        _dispatch_kernel,
        grid_spec=pltpu.PrefetchScalarGridSpec(
            num_scalar_prefetch=2,
            grid=(N_CHUNKS, steps),
            in_specs=[pl.BlockSpec((TD * ROW_TILES, LANES), lambda c, s, a0, a1: (c * steps + s, 0))],
            out_specs=pl.BlockSpec((CHUNK_ROWS * ROW_TILES, LANES), lambda c, s, a0, a1: (c, 0)),
        ),
        out_shape=jax.ShapeDtypeStruct((TOTAL_ROWS * ROW_TILES, LANES), U32),
        compiler_params=pltpu.CompilerParams(
            dimension_semantics=("arbitrary", "arbitrary"), vmem_limit_bytes=VMEM_LIMIT),
        name="dispatch",
    )(a0, a1, h2p_2d)


def _expert_kernel(blk_ref, exp_ref, nv_ref, xs_ref, wg_ref, wu_ref, wd_ref, ys_ref):
    i = pl.program_id(0)

    @pl.when(i < nv_ref[0])
    def _():
        los, his = [], []
        for j in range(ROW_TILES):
            lo, hi = _unpack_halves(xs_ref[:, j, :, :].reshape(ROW_TILE, LANES))
            los.append(lo)
            his.append(hi)
        xb = jnp.concatenate(los + his, axis=-1).astype(BF16)
        g = jnp.dot(xb, wg_ref[0], preferred_element_type=F32)
        u = jnp.dot(xb, wu_ref[0], preferred_element_type=F32)
        act = (g * (1.0 / (1.0 + jnp.exp(-g))) * u).astype(BF16)
        y = jnp.dot(act, wd_ref[0], preferred_element_type=F32)
        packed = _pack_halves(y.astype(BF16).astype(F32))
        for j in range(ROW_TILES):
            ys_ref[:, j, :, :] = packed[:, j * LANES:(j + 1) * LANES].reshape(
                ROW_TILE // SUBLANES, SUBLANES, LANES)


def _expert_call(tile_blk, tile_exp, n_valid, xs4, wg, wu, wd):
    rows_blk = (ROW_TILE // SUBLANES, ROW_TILES, SUBLANES, LANES)
    return pl.pallas_call(
        _expert_kernel,
        grid_spec=pltpu.PrefetchScalarGridSpec(
            num_scalar_prefetch=3,
            grid=(MAX_ROW_TILES,),
            in_specs=[
                pl.BlockSpec(rows_blk, lambda i, tb, te, nv: (tb[i], 0, 0, 0)),
                pl.BlockSpec((1, D_MODEL, D_EXPERT), lambda i, tb, te, nv: (te[i], 0, 0)),
                pl.BlockSpec((1, D_MODEL, D_EXPERT), lambda i, tb, te, nv: (te[i], 0, 0)),
                pl.BlockSpec((1, D_EXPERT, D_MODEL), lambda i, tb, te, nv: (te[i], 0, 0)),
            ],
            out_specs=pl.BlockSpec(rows_blk, lambda i, tb, te, nv: (tb[i], 0, 0, 0)),
        ),
        out_shape=jax.ShapeDtypeStruct((TOTAL_ROWS // SUBLANES, ROW_TILES, SUBLANES, LANES), U32),
        compiler_params=pltpu.CompilerParams(
            dimension_semantics=("arbitrary",), vmem_limit_bytes=VMEM_LIMIT),
        name="experts",
    )(tile_blk, tile_exp, n_valid, xs4, wg, wu, wd)


def _combine_kernel(a0_ref, a1_ref, ys_ref, x1_ref, w_ref, ada_ref, fg_ref, out_ref, r0_scr, r1_scr,
                    *, final):
    c = pl.program_id(0)
    s = pl.program_id(1)
    base = c * ROUTE_CHUNK + s * TC

    def body(t8, carry):
        for k in range(SUBLANES):
            t = base + t8 * SUBLANES + k
            dst = pl.ds(t8 * (ROW_TILES * SUBLANES) + k, ROW_TILES, stride=SUBLANES)
            r0_scr[dst, :] = ys_ref[pl.ds(a0_ref[t], ROW_TILES, stride=SUBLANES), :]
            r1_scr[dst, :] = ys_ref[pl.ds(a1_ref[t], ROW_TILES, stride=SUBLANES), :]
        return carry

    lax.fori_loop(0, TC // SUBLANES, body, 0)

    w = w_ref[...]
    w0 = w[:, 0:1]
    w1 = w[:, 1:2]
    g2 = ada_ref[0][5:6]
    n_grp = TC // SUBLANES
    los, his = [], []
    for j in range(ROW_TILES):
        parts0 = [r0_scr[pl.ds((tb * ROW_TILES + j) * SUBLANES, SUBLANES), :] for tb in range(n_grp)]
        parts1 = [r1_scr[pl.ds((tb * ROW_TILES + j) * SUBLANES, SUBLANES), :] for tb in range(n_grp)]
        lo0, hi0 = _unpack_halves(jnp.concatenate(parts0, axis=0))
        lo1, hi1 = _unpack_halves(jnp.concatenate(parts1, axis=0))
        los.append(w0 * lo0 + w1 * lo1)
        his.append(w0 * hi0 + w1 * hi1)
    moe = jnp.concatenate(los + his, axis=-1)
    x2 = x1_ref[...] + g2 * moe
    if final:
        r = lax.rsqrt(jnp.mean(x2 * x2, axis=-1, keepdims=True) + EPS)
        x2 = x2 * r * fg_ref[...]
    out_ref[...] = x2


def _combine_call(a0, a1, ys_2d, x1, wcol, ada_l, final_g, final):
    steps = ROUTE_CHUNK // TC
    tok = lambda c, s, a0, a1: (c * steps + s, 0)
    return pl.pallas_call(
        functools.partial(_combine_kernel, final=final),
        grid_spec=pltpu.PrefetchScalarGridSpec(
            num_scalar_prefetch=2,
            grid=(N_CHUNKS, steps),
            in_specs=[
                pl.BlockSpec((CHUNK_ROWS * ROW_TILES, LANES), lambda c, s, a0, a1: (c, 0)),
                pl.BlockSpec((TC, D_MODEL), tok),
                pl.BlockSpec((TC, 2), tok),
                pl.BlockSpec((1, N_ADA, D_MODEL),
                             lambda c, s, a0, a1: ((c * ROUTE_CHUNK + s * TC) // SEQ, 0, 0)),
                pl.BlockSpec((1, D_MODEL), lambda c, s, a0, a1: (0, 0)),
            ],
            out_specs=pl.BlockSpec((TC, D_MODEL), tok),
            scratch_shapes=[pltpu.VMEM((TC * ROW_TILES, LANES), U32),
                            pltpu.VMEM((TC * ROW_TILES, LANES), U32)],
        ),
        out_shape=jax.ShapeDtypeStruct((N_TOK, D_MODEL), F32),
        compiler_params=pltpu.CompilerParams(
            dimension_semantics=("arbitrary", "arbitrary"), vmem_limit_bytes=VMEM_LIMIT),
        name="combine_final" if final else "combine",
    )(a0, a1, ys_2d, x1, wcol, ada_l, final_g)


def _routing_tables(rt, cnt):
    e0 = rt[0].astype(I32)
    e1 = rt[1].astype(I32)
    rank0 = rt[4].astype(I32)
    rank1 = rt[5].astype(I32)
    counts = cnt[TILES_PER_CHUNK - 1::TILES_PER_CHUNK, :, 0].astype(I32)
    n_tiles = (counts + ROW_TILE - 1) // ROW_TILE
    tile_off = jnp.cumsum(n_tiles, axis=1) - n_tiles
    row_off = tile_off * ROW_TILE
    experts = jnp.arange(N_EXPERTS, dtype=I32)[None, :, None]
    off_tok = row_off[:, :, None]
    pick = lambda e: jnp.sum(
        jnp.where(e.reshape(N_CHUNKS, 1, ROUTE_CHUNK) == experts, off_tok, 0), axis=1).reshape(N_TOK)
    slot0 = pick(e0) + rank0
    slot1 = pick(e1) + rank1
    a0 = _row_addr(slot0)
    a1 = _row_addr(slot1)

    nt_flat = n_tiles.T.reshape(-1)
    ends = jnp.cumsum(nt_flat)
    starts = ends - nt_flat
    total = ends[-1]
    step = jnp.minimum(jnp.arange(MAX_ROW_TILES, dtype=I32), total - 1)
    seg = jnp.sum((ends[None, :] <= step[:, None]).astype(I32), axis=1)
    in_seg = seg[:, None] == jnp.arange(N_EXPERTS * N_CHUNKS, dtype=I32)[None, :]
    k = step - jnp.sum(jnp.where(in_seg, starts[None, :], 0), axis=1)
    e = seg // N_CHUNKS
    c = seg % N_CHUNKS
    tile_off_flat = tile_off.T.reshape(-1)
    tile_blk = c * (CHUNK_ROWS // ROW_TILE) + jnp.sum(jnp.where(in_seg, tile_off_flat[None, :], 0), axis=1) + k
    return a0, a1, tile_blk.astype(I32), e.astype(I32), total.reshape(1).astype(I32)


def _layer_consts(l, norm1_g, norm2_g, w_in, conv_w, conv_b, sp_norm_g, sp_w, sp_b, mix_out_g,
                  w_out, shared):
    pos = jnp.arange(SP_BLOCK)
    mask = (pos[None, :] // CHUNK) <= (pos[:, None] // CHUNK)
    w = jnp.where(mask[None], sp_w[l], 0.0)
    w_pairs = jnp.concatenate([w[0::2], w[1::2]], axis=-1).astype(BF16)
    sp_bias = jnp.repeat(sp_b[l].T, HEAD_DIM, axis=1)
    return (
        norm1_g[l].reshape(1, D_MODEL), norm2_g[l].reshape(1, D_MODEL),
        w_in[l].astype(BF16),
        conv_w[l], conv_b[l].reshape(1, D_CONV), sp_norm_g[l].reshape(1, D_SPATIAL),
        w_pairs, sp_bias,
        mix_out_g[l].reshape(1, D_MODEL),
        w_out[l].astype(BF16),
    ) + shared


def kernel(x, c, w_ada, b_ada, norm1_g, norm2_g, w_in, conv_w, conv_b, sp_norm_g, sp_w, sp_b, mix_out_g, w_out, router_w, router_b, exp_w_gate, exp_w_up, exp_w_down, final_g):
    ada = _ada_table(c, w_ada, b_ada).reshape(DEPTH, BATCH, N_ADA, D_MODEL)

    rw = jnp.pad(router_w, ((0, 0), (0, LANES - N_EXPERTS)))
    rw_hi = rw.astype(BF16)
    rw_lo = (rw - rw_hi.astype(F32)).astype(BF16)
    head_of = jnp.arange(D_MODEL) // HEAD_DIM
    hred = (head_of[:, None] == jnp.arange(LANES)[None, :]).astype(F32) / HEAD_DIM
    hexp_half = (jnp.arange(LANES)[:, None] == head_of[None, :]).astype(F32)
    hexp = jnp.concatenate([hexp_half, hexp_half], axis=0)
    tri = (jnp.arange(TM)[:, None] < jnp.arange(TM)[None, :])
    shared = (rw_hi, rw_lo, router_b.reshape(N_EXPERTS, 1), hred.astype(BF16), hexp.astype(BF16),
              tri.astype(BF16))

    wg = exp_w_gate.astype(BF16)
    wu = exp_w_up.astype(BF16)
    wd = exp_w_down.astype(BF16)
    fg = final_g.reshape(1, D_MODEL)

    xc = x.reshape(N_TOK, D_MODEL)
    for l in range(DEPTH):
        consts = _layer_consts(l, norm1_g, norm2_g, w_in, conv_w, conv_b, sp_norm_g, sp_w, sp_b,
                               mix_out_g, w_out, shared)
        x1, h2p, rt, cnt = _mix_call(xc, ada[l], consts)
        a0, a1, tile_blk, tile_exp, n_valid = _routing_tables(rt, cnt)
        xs = _dispatch_call(a0, a1, h2p.reshape(N_TOK * ROW_TILES, LANES))
        ys = _expert_call(tile_blk, tile_exp, n_valid,
                          xs.reshape(TOTAL_ROWS // SUBLANES, ROW_TILES, SUBLANES, LANES),
                          wg[l], wu[l], wd[l])
        wcol = rt[2:4].T
        xc = _combine_call(a0, a1, ys.reshape(TOTAL_ROWS * ROW_TILES, LANES), x1, wcol, ada[l], fg,
                           final=(l == DEPTH - 1))
    return xc.reshape(BATCH, SEQ, D_MODEL)
```

```python
import functools

import jax
import jax.numpy as jnp
from jax import lax
from jax.experimental import pallas as pl
from jax.experimental.pallas import tpu as pltpu

F32 = jnp.float32
BF16 = jnp.bfloat16
U32 = jnp.uint32
I32 = jnp.int32

D_MODEL = 1024
BATCH = 8
SEQ = 2048
DEPTH = 4
N_TOK = BATCH * SEQ
CHUNK = 64
HEAD_DIM = 64
D_CONV = 512
D_SPATIAL = 512
N_SP_HEADS = 8
N_MIX_HEADS = 16
SP_BLOCK = 128
N_EXPERTS = 16
N_GROUPS = 4
PER_GROUP = 4
D_EXPERT = 512
N_ADA = 6
EPS = 1e-6

LANES = 128
SUBLANES = 8
HALF = D_MODEL // 2
ROW_TILES = HALF // LANES

TM = 512
SB = SP_BLOCK
TILES_PER_SEQ = SEQ // TM
ROUTE_CHUNK = 4096
N_CHUNKS = N_TOK // ROUTE_CHUNK
TILES_PER_CHUNK = ROUTE_CHUNK // TM
ROW_TILE = 128
ROW_GROUPS = ROW_TILE // SUBLANES
CHUNK_ROWS = 2 * ROUTE_CHUNK + N_EXPERTS * ROW_TILE
TOTAL_ROWS = N_CHUNKS * CHUNK_ROWS
MAX_ROW_TILES = TOTAL_ROWS // ROW_TILE
TD = 1024
TC = 512
VMEM_LIMIT = 58 * 1024 * 1024


def _gelu_tanh(x):
    c = 0.7978845608028654
    return 0.5 * x * (1.0 + jnp.tanh(c * (x + 0.044715 * (x * x * x))))


def _split_bf16(x):
    hi = x.astype(BF16)
    lo = (x - hi.astype(F32)).astype(BF16)
    return hi, lo


def _pack_halves(x):
    return pltpu.pack_elementwise([x[:, :HALF], x[:, HALF:]], packed_dtype=BF16)


def _unpack_halves(words):
    lo = pltpu.unpack_elementwise(words, index=0, packed_dtype=BF16, unpacked_dtype=F32)
    hi = pltpu.unpack_elementwise(words, index=1, packed_dtype=BF16, unpacked_dtype=F32)
    return lo, hi


def _ada_kernel(c_ref, w_ref, b_ref, o_ref):
    c = c_ref[...]
    ca = c * (1.0 / (1.0 + jnp.exp(-c)))
    c_hi, c_lo = _split_bf16(ca)
    w_hi, w_lo = _split_bf16(w_ref[0])
    acc = jnp.dot(c_hi, w_hi, preferred_element_type=F32)
    acc += jnp.dot(c_lo, w_hi, preferred_element_type=F32)
    acc += jnp.dot(c_hi, w_lo, preferred_element_type=F32)
    o_ref[0] = acc + b_ref[0]


def _ada_table(c, w_ada, b_ada):
    n_blk = N_ADA
    return pl.pallas_call(
        _ada_kernel,
        grid=(DEPTH, n_blk),
        in_specs=[
            pl.BlockSpec((BATCH, D_MODEL), lambda l, n: (0, 0)),
            pl.BlockSpec((1, D_MODEL, D_MODEL), lambda l, n: (l, 0, n)),
            pl.BlockSpec((1, 1, D_MODEL), lambda l, n: (l, 0, n)),
        ],
        out_specs=pl.BlockSpec((1, BATCH, D_MODEL), lambda l, n: (l, 0, n)),
        out_shape=jax.ShapeDtypeStruct((DEPTH, BATCH, N_ADA * D_MODEL), F32),
        name="ada_table",
    )(c, w_ada, b_ada.reshape(DEPTH, 1, N_ADA * D_MODEL))


def _mix_kernel(x_ref, ada_ref, n1g_ref, n2g_ref, win_ref, cw_ref, cb_ref, spg_ref, spw_ref,
                spbias_ref, mog_ref, wout_ref, rwh_ref, rwl_ref, rb_ref, hred_ref, hexp_ref,
                tri_ref, x1_ref, h2p_ref, rt_ref, cnt_ref, z_scr, cnt_scr):
    i = pl.program_id(0)

    @pl.when(i % TILES_PER_SEQ == 0)
    def _():
        z_scr[pl.ds(0, SUBLANES), :] = jnp.zeros((SUBLANES, D_CONV), F32)

    @pl.when(i % TILES_PER_CHUNK == 0)
    def _():
        cnt_scr[...] = jnp.zeros((N_EXPERTS, LANES), F32)

    for sb in range(TM // SB):
        _mix_rows(sb * SB, x_ref, ada_ref, n1g_ref, n2g_ref, win_ref, cw_ref, cb_ref, spg_ref,
                  spw_ref, spbias_ref, mog_ref, wout_ref, rwh_ref, rwl_ref, rb_ref, hred_ref,
                  hexp_ref, tri_ref, x1_ref, h2p_ref, rt_ref, z_scr, cnt_scr)
    z_scr[pl.ds(0, SUBLANES), :] = z_scr[pl.ds(TM, SUBLANES), :]
    cnt_ref[0] = cnt_scr[...]


def _mix_rows(r0, x_ref, ada_ref, n1g_ref, n2g_ref, win_ref, cw_ref, cb_ref, spg_ref, spw_ref,
              spbias_ref, mog_ref, wout_ref, rwh_ref, rwl_ref, rb_ref, hred_ref, hexp_ref,
              tri_ref, x1_ref, h2p_ref, rt_ref, z_scr, cnt_scr):
    x = x_ref[pl.ds(r0, SB), :]
    ada = ada_ref[0]
    sh1, sc1, g1 = ada[0:1], ada[1:2], ada[2:3]
    sh2, sc2 = ada[3:4], ada[4:5]

    rinv = lax.rsqrt(jnp.mean(x * x, axis=-1, keepdims=True) + EPS)
    h = (x * rinv * n1g_ref[...]) * (1.0 + sc1) + sh1
    hb = h.astype(BF16)

    def proj(k):
        return jnp.dot(hb, win_ref[:, k * 512:(k + 1) * 512], preferred_element_type=F32)

    z = proj(1) * proj(2)
    z_scr[pl.ds(SUBLANES + r0, SB), :] = z
    z1 = z_scr[pl.ds(SUBLANES + r0 - 1, SB), :]
    z2 = z_scr[pl.ds(SUBLANES + r0 - 2, SB), :]
    cw = cw_ref[...]
    acc = cb_ref[...] + cw[0:1] * z2
    acc = acc + cw[1:2] * z1
    acc = acc + cw[2:3] * z
    y_conv = proj(0) * acc

    ug = _gelu_tanh(proj(3))
    vg = _gelu_tanh(proj(4))
    vr = lax.rsqrt(jnp.mean(vg * vg, axis=-1, keepdims=True) + EPS)
    vn = (vg * vr * spg_ref[...]).astype(BF16)
    lane = lax.broadcasted_iota(I32, (1, LANES), 1)
    first_head = lane < HEAD_DIM
    zero = jnp.zeros((), BF16)
    cols = []
    for p in range(N_SP_HEADS // 2):
        vv = vn[:, p * LANES:(p + 1) * LANES]
        rhs = jnp.concatenate([jnp.where(first_head, vv, zero), jnp.where(first_head, zero, vv)], axis=0)
        cols.append(jnp.dot(spw_ref[p], rhs, preferred_element_type=F32))
    y_sp = ug * (jnp.concatenate(cols, axis=-1) + spbias_ref[...])

    y = jnp.concatenate([y_conv, y_sp], axis=-1)
    ms = jnp.dot((y * y).astype(BF16), hred_ref[...], preferred_element_type=F32)
    r_hi, r_lo = _split_bf16(lax.rsqrt(ms + EPS))
    scale = jnp.dot(jnp.concatenate([r_hi, r_lo], axis=-1), hexp_ref[...], preferred_element_type=F32)
    yn = (y * scale * mog_ref[...]).astype(BF16)
    x1 = x + g1 * jnp.dot(yn, wout_ref[...], preferred_element_type=F32)
    x1_ref[pl.ds(r0, SB), :] = x1

    r2 = lax.rsqrt(jnp.mean(x1 * x1, axis=-1, keepdims=True) + EPS)
    h2 = (x1 * r2 * n2g_ref[...]) * (1.0 + sc2) + sh2
    h2_hi, h2_lo = _split_bf16(h2)
    packed = _pack_halves(h2_hi.astype(F32))
    for j in range(ROW_TILES):
        h2p_ref[pl.ds(r0 // SUBLANES, SB // SUBLANES), j, :, :] = (
            packed[:, j * LANES:(j + 1) * LANES].reshape(SB // SUBLANES, SUBLANES, LANES))

    logits = jnp.dot(h2_hi, rwh_ref[...], preferred_element_type=F32)
    logits += jnp.dot(h2_lo, rwh_ref[...], preferred_element_type=F32)
    logits += jnp.dot(h2_hi, rwl_ref[...], preferred_element_type=F32)
    lg = logits.T[0:N_EXPERTS, :]
    scores = 1.0 / (1.0 + jnp.exp(-lg))
    sel = scores + rb_ref[...]

    gs = []
    for g in range(N_GROUPS):
        a, b, c, d = (sel[g * PER_GROUP + k:g * PER_GROUP + k + 1, :] for k in range(PER_GROUP))
        hi1, lo1 = jnp.maximum(a, b), jnp.minimum(a, b)
        hi2, lo2 = jnp.maximum(c, d), jnp.minimum(c, d)
        gs.append(jnp.maximum(hi1, hi2) + jnp.maximum(jnp.minimum(hi1, hi2), jnp.maximum(lo1, lo2)))
    best = gs[0]
    top_group = jnp.zeros((1, SB), I32)
    for g in range(1, N_GROUPS):
        better = gs[g] > best
        best = jnp.where(better, gs[g], best)
        top_group = jnp.where(better, g, top_group)

    eid_i = lax.broadcasted_iota(I32, (N_EXPERTS, SB), 0)
    eid = eid_i.astype(F32)
    neg_inf = jnp.float32(-jnp.inf)
    masked = jnp.where((eid_i // PER_GROUP) == top_group, sel, neg_inf)
    m1 = jnp.max(masked, axis=0, keepdims=True)
    i1 = jnp.min(jnp.where(masked == m1, eid, float(N_EXPERTS)), axis=0, keepdims=True)
    masked2 = jnp.where(eid == i1, neg_inf, masked)
    m2 = jnp.max(masked2, axis=0, keepdims=True)
    i2 = jnp.min(jnp.where(masked2 == m2, eid, float(N_EXPERTS)), axis=0, keepdims=True)
    pick1 = eid == i1
    pick2 = eid == i2
    w1 = jnp.sum(jnp.where(pick1, scores, 0.0), axis=0, keepdims=True)
    w2 = jnp.sum(jnp.where(pick2, scores, 0.0), axis=0, keepdims=True)
    wsum = w1 + w2
    c1 = w1 / wsum
    c2 = w2 / wsum

    assigned = jnp.where(pick1 | pick2, 1.0, 0.0)
    before = jnp.dot(assigned.astype(BF16), tri_ref[...], preferred_element_type=F32)
    carry = cnt_scr[...]
    rank = before + carry[:, 0:1]
    rank1 = jnp.sum(jnp.where(pick1, rank, 0.0), axis=0, keepdims=True)
    rank2 = jnp.sum(jnp.where(pick2, rank, 0.0), axis=0, keepdims=True)
    cnt_scr[...] = carry + jnp.sum(assigned, axis=1, keepdims=True)
    unused = jnp.zeros((1, SB), F32)
    for r, row in enumerate((i1, i2, c1, c2, rank1, rank2, unused, unused)):
        rt_ref[r:r + 1, pl.ds(r0, SB)] = row


def _mix_call(x2d, ada_l, consts):
    n_tiles = N_TOK // TM
    full = lambda shape: pl.BlockSpec(shape, lambda i: (0,) * len(shape))
    in_specs = [
        pl.BlockSpec((TM, D_MODEL), lambda i: (i, 0)),
        pl.BlockSpec((1, N_ADA, D_MODEL), lambda i: (i // TILES_PER_SEQ, 0, 0)),
        full((1, D_MODEL)), full((1, D_MODEL)),
        full((D_MODEL, 5 * 512)),
        full((3, D_CONV)), full((1, D_CONV)), full((1, D_SPATIAL)),
        full((N_SP_HEADS // 2, SP_BLOCK, 2 * SP_BLOCK)),
        full((SP_BLOCK, D_SPATIAL)),
        full((1, D_MODEL)),
        full((D_MODEL, D_MODEL)),
        full((D_MODEL, LANES)), full((D_MODEL, LANES)), full((N_EXPERTS, 1)),
        full((D_MODEL, LANES)), full((2 * LANES, D_MODEL)),
        full((SB, SB)),
    ]
    out_specs = [
        pl.BlockSpec((TM, D_MODEL), lambda i: (i, 0)),
        pl.BlockSpec((TM // SUBLANES, ROW_TILES, SUBLANES, LANES), lambda i: (i, 0, 0, 0)),
        pl.BlockSpec((SUBLANES, TM), lambda i: (0, i)),
        pl.BlockSpec((1, N_EXPERTS, LANES), lambda i: (i, 0, 0)),
    ]
    out_shape = [
        jax.ShapeDtypeStruct((N_TOK, D_MODEL), F32),
        jax.ShapeDtypeStruct((N_TOK // SUBLANES, ROW_TILES, SUBLANES, LANES), U32),
        jax.ShapeDtypeStruct((SUBLANES, N_TOK), F32),
        jax.ShapeDtypeStruct((n_tiles, N_EXPERTS, LANES), F32),
    ]
    return pl.pallas_call(
        _mix_kernel,
        grid=(n_tiles,),
        in_specs=in_specs,
        out_specs=out_specs,
        out_shape=out_shape,
        scratch_shapes=[pltpu.VMEM((TM + SUBLANES, D_CONV), F32), pltpu.VMEM((N_EXPERTS, LANES), F32)],
        compiler_params=pltpu.CompilerParams(
            dimension_semantics=("arbitrary",), vmem_limit_bytes=VMEM_LIMIT),
        name="mix",
    )(x2d, ada_l, *consts)


def _row_addr(t):
    return (t // SUBLANES) * (ROW_TILES * SUBLANES) + t % SUBLANES


def _dispatch_kernel(a0_ref, a1_ref, h2p_ref, xs_ref):
    c = pl.program_id(0)
    s = pl.program_id(1)

    @pl.when(s == 0)
    def _():
        xs_ref[...] = jnp.zeros(xs_ref.shape, U32)

    base = c * ROUTE_CHUNK + s * TD

    def body(t8, carry):
        for k in range(SUBLANES):
            row = h2p_ref[pl.ds(t8 * (ROW_TILES * SUBLANES) + k, ROW_TILES, stride=SUBLANES), :]
            t = base + t8 * SUBLANES + k
            xs_ref[pl.ds(a0_ref[t], ROW_TILES, stride=SUBLANES), :] = row
            xs_ref[pl.ds(a1_ref[t], ROW_TILES, stride=SUBLANES), :] = row
        return carry

    lax.fori_loop(0, TD // SUBLANES, body, 0)


def _dispatch_call(a0, a1, h2p_2d):
    steps = ROUTE_CHUNK // TD
    return pl.pallas_call(
        _dispatch_kernel,
        grid_spec=pltpu.PrefetchScalarGridSpec(
            num_scalar_prefetch=2,
            grid=(N_CHUNKS, steps),
            in_specs=[pl.BlockSpec((TD * ROW_TILES, LANES), lambda c, s, a0, a1: (c * steps + s, 0))],
            out_specs=pl.BlockSpec((CHUNK_ROWS * ROW_TILES, LANES), lambda c, s, a0, a1: (c, 0)),
        ),
        out_shape=jax.ShapeDtypeStruct((TOTAL_ROWS * ROW_TILES, LANES), U32),
        compiler_params=pltpu.CompilerParams(
            dimension_semantics=("arbitrary", "arbitrary"), vmem_limit_bytes=VMEM_LIMIT),
        name="dispatch",
    )(a0, a1, h2p_2d)


def _expert_kernel(tb_ref, est_ref, ecnt_ref, xs_hbm, wg_ref, wu_ref, wd_ref, ys_hbm,
                   wg_s, wu_s, wd_s, xbuf, ybuf, sem_in, sem_out):
    e = pl.program_id(0)
    first = est_ref[e]
    n_tiles = ecnt_ref[e]
    n_pairs = (n_tiles + 1) // 2

    wg_s[...] = wg_ref[0, 0].astype(BF16)
    wu_s[...] = wu_ref[0, 0].astype(BF16)
    wd_s[...] = wd_ref[0, 0].astype(BF16)
    xbuf[...] = jnp.zeros(xbuf.shape, U32)

    def in_copy(k, slot, half):
        rows = pl.ds(tb_ref[first + k] * ROW_GROUPS, ROW_GROUPS)
        return pltpu.make_async_copy(xs_hbm.at[rows], xbuf.at[slot, half], sem_in.at[slot, half])

    def out_copy(k, slot, half):
        rows = pl.ds(tb_ref[first + k] * ROW_GROUPS, ROW_GROUPS)
        return pltpu.make_async_copy(ybuf.at[slot, half], ys_hbm.at[rows], sem_out.at[slot, half])

    def for_pair(p, slot, action, make):
        getattr(make(2 * p, slot, 0), action)()

        @pl.when(2 * p + 1 < n_tiles)
        def _():
            getattr(make(2 * p + 1, slot, 1), action)()

    @pl.when(n_pairs > 0)
    def _():
        for_pair(0, 0, "start", in_copy)

    def compute(slot, half):
        los, his = [], []
        for j in range(ROW_TILES):
            lo, hi = _unpack_halves(xbuf[slot, half, :, j, :, :].reshape(ROW_TILE, LANES))
            los.append(lo)
            his.append(hi)
        xb = jnp.concatenate(los + his, axis=-1).astype(BF16)
        g = jnp.dot(xb, wg_s[...], preferred_element_type=F32)
        u = jnp.dot(xb, wu_s[...], preferred_element_type=F32)
        act = (g * (1.0 / (1.0 + jnp.exp(-g))) * u).astype(BF16)
        y = jnp.dot(act, wd_s[...], preferred_element_type=F32)
        packed = _pack_halves(y.astype(BF16).astype(F32))
        for j in range(ROW_TILES):
            ybuf[slot, half, :, j, :, :] = packed[:, j * LANES:(j + 1) * LANES].reshape(
                ROW_GROUPS, SUBLANES, LANES)

    def body(p, carry):
        slot = p % 2
        for_pair(p, slot, "wait", in_copy)

        @pl.when(p + 1 < n_pairs)
        def _():
            for_pair(p + 1, 1 - slot, "start", in_copy)

        @pl.when(p >= 2)
        def _():
            for_pair(p - 2, slot, "wait", out_copy)

        compute(slot, 0)
        compute(slot, 1)
        for_pair(p, slot, "start", out_copy)
        return carry

    lax.fori_loop(0, n_pairs, body, 0)

    @pl.when(n_pairs >= 2)
    def _():
        for_pair(n_pairs - 2, n_pairs % 2, "wait", out_copy)

    @pl.when(n_pairs >= 1)
    def _():
        for_pair(n_pairs - 1, (n_pairs - 1) % 2, "wait", out_copy)


def _expert_call(layer, tile_blk, e_start, e_count, xs4, wg, wu, wd):
    pair_buf = (2, 2, ROW_GROUPS, ROW_TILES, SUBLANES, LANES)
    return pl.pallas_call(
        _expert_kernel,
        grid_spec=pltpu.PrefetchScalarGridSpec(
            num_scalar_prefetch=3,
            grid=(N_EXPERTS,),
            in_specs=[
                pl.BlockSpec(memory_space=pl.ANY),
                pl.BlockSpec((1, 1, D_MODEL, D_EXPERT), lambda e, tb, es, ec: (layer, e, 0, 0)),
                pl.BlockSpec((1, 1, D_MODEL, D_EXPERT), lambda e, tb, es, ec: (layer, e, 0, 0)),
                pl.BlockSpec((1, 1, D_EXPERT, D_MODEL), lambda e, tb, es, ec: (layer, e, 0, 0)),
            ],
            out_specs=pl.BlockSpec(memory_space=pl.ANY),
            scratch_shapes=[
                pltpu.VMEM((D_MODEL, D_EXPERT), BF16),
                pltpu.VMEM((D_MODEL, D_EXPERT), BF16),
                pltpu.VMEM((D_EXPERT, D_MODEL), BF16),
                pltpu.VMEM(pair_buf, U32),
                pltpu.VMEM(pair_buf, U32),
                pltpu.SemaphoreType.DMA((2, 2)),
                pltpu.SemaphoreType.DMA((2, 2)),
            ],
        ),
        out_shape=jax.ShapeDtypeStruct((TOTAL_ROWS // SUBLANES, ROW_TILES, SUBLANES, LANES), U32),
        input_output_aliases={3: 0},
        compiler_params=pltpu.CompilerParams(
            dimension_semantics=("arbitrary",), vmem_limit_bytes=VMEM_LIMIT),
        name="experts",
    )(tile_blk, e_start, e_count, xs4, wg, wu, wd)


def _combine_kernel(a0_ref, a1_ref, ys_ref, x1_ref, w_ref, ada_ref, fg_ref, out_ref, r0_scr, r1_scr,
                    *, final):
    c = pl.program_id(0)
    s = pl.program_id(1)
    base = c * ROUTE_CHUNK + s * TC

    def body(t8, carry):
        for k in range(SUBLANES):
            t = base + t8 * SUBLANES + k
            dst = pl.ds(t8 * (ROW_TILES * SUBLANES) + k, ROW_TILES, stride=SUBLANES)
            r0_scr[dst, :] = ys_ref[pl.ds(a0_ref[t], ROW_TILES, stride=SUBLANES), :]
            r1_scr[dst, :] = ys_ref[pl.ds(a1_ref[t], ROW_TILES, stride=SUBLANES), :]
        return carry

    lax.fori_loop(0, TC // SUBLANES, body, 0)

    w = w_ref[...]
    w0 = w[:, 0:1]
    w1 = w[:, 1:2]
    g2 = ada_ref[0][5:6]
    n_grp = TC // SUBLANES
    los, his = [], []
    for j in range(ROW_TILES):
        parts0 = [r0_scr[pl.ds((tb * ROW_TILES + j) * SUBLANES, SUBLANES), :] for tb in range(n_grp)]
        parts1 = [r1_scr[pl.ds((tb * ROW_TILES + j) * SUBLANES, SUBLANES), :] for tb in range(n_grp)]
        lo0, hi0 = _unpack_halves(jnp.concatenate(parts0, axis=0))
        lo1, hi1 = _unpack_halves(jnp.concatenate(parts1, axis=0))
        los.append(w0 * lo0 + w1 * lo1)
        his.append(w0 * hi0 + w1 * hi1)
    moe = jnp.concatenate(los + his, axis=-1)
    x2 = x1_ref[...] + g2 * moe
    if final:
        r = lax.rsqrt(jnp.mean(x2 * x2, axis=-1, keepdims=True) + EPS)
        x2 = x2 * r * fg_ref[...]
    out_ref[...] = x2


def _combine_call(a0, a1, ys_2d, x1, wcol, ada_l, final_g, final):
    steps = ROUTE_CHUNK // TC
    tok = lambda c, s, a0, a1: (c * steps + s, 0)
    return pl.pallas_call(
        functools.partial(_combine_kernel, final=final),
        grid_spec=pltpu.PrefetchScalarGridSpec(
            num_scalar_prefetch=2,
            grid=(N_CHUNKS, steps),
            in_specs=[
                pl.BlockSpec((CHUNK_ROWS * ROW_TILES, LANES), lambda c, s, a0, a1: (c, 0)),
                pl.BlockSpec((TC, D_MODEL), tok),
                pl.BlockSpec((TC, 2), tok),
                pl.BlockSpec((1, N_ADA, D_MODEL),
                             lambda c, s, a0, a1: ((c * ROUTE_CHUNK + s * TC) // SEQ, 0, 0)),
                pl.BlockSpec((1, D_MODEL), lambda c, s, a0, a1: (0, 0)),
            ],
            out_specs=pl.BlockSpec((TC, D_MODEL), tok),
            scratch_shapes=[pltpu.VMEM((TC * ROW_TILES, LANES), U32),
                            pltpu.VMEM((TC * ROW_TILES, LANES), U32)],
        ),
        out_shape=jax.ShapeDtypeStruct((N_TOK, D_MODEL), F32),
        compiler_params=pltpu.CompilerParams(
            dimension_semantics=("arbitrary", "arbitrary"), vmem_limit_bytes=VMEM_LIMIT),
        name="combine_final" if final else "combine",
    )(a0, a1, ys_2d, x1, wcol, ada_l, final_g)


def _routing_tables(rt, cnt):
    e0 = rt[0].astype(I32)
    e1 = rt[1].astype(I32)
    rank0 = rt[4].astype(I32)
    rank1 = rt[5].astype(I32)
    counts = cnt[TILES_PER_CHUNK - 1::TILES_PER_CHUNK, :, 0].astype(I32)
    n_tiles = (counts + ROW_TILE - 1) // ROW_TILE
    tile_off = jnp.cumsum(n_tiles, axis=1) - n_tiles
    row_off = tile_off * ROW_TILE
    experts = jnp.arange(N_EXPERTS, dtype=I32)[None, :, None]
    off_tok = row_off[:, :, None]
    pick = lambda e: jnp.sum(
        jnp.where(e.reshape(N_CHUNKS, 1, ROUTE_CHUNK) == experts, off_tok, 0), axis=1).reshape(N_TOK)
    slot0 = pick(e0) + rank0
    slot1 = pick(e1) + rank1
    a0 = _row_addr(slot0)
    a1 = _row_addr(slot1)

    nt_flat = n_tiles.T.reshape(-1)
    ends = jnp.cumsum(nt_flat)
    starts = ends - nt_flat
    total = ends[-1]
    step = jnp.minimum(jnp.arange(MAX_ROW_TILES, dtype=I32), total - 1)
    seg = jnp.sum((ends[None, :] <= step[:, None]).astype(I32), axis=1)
    in_seg = seg[:, None] == jnp.arange(N_EXPERTS * N_CHUNKS, dtype=I32)[None, :]
    k = step - jnp.sum(jnp.where(in_seg, starts[None, :], 0), axis=1)
    c = seg % N_CHUNKS
    tile_off_flat = tile_off.T.reshape(-1)
    tile_blk = c * (CHUNK_ROWS // ROW_TILE) + jnp.sum(jnp.where(in_seg, tile_off_flat[None, :], 0), axis=1) + k
    e_count = jnp.sum(n_tiles, axis=0)
    e_start = jnp.cumsum(e_count) - e_count
    return a0, a1, tile_blk.astype(I32), e_start.astype(I32), e_count.astype(I32)


def _layer_consts(l, norm1_g, norm2_g, w_in, conv_w, conv_b, sp_norm_g, sp_w, sp_b, mix_out_g,
                  w_out, shared):
    pos = jnp.arange(SP_BLOCK)
    mask = (pos[None, :] // CHUNK) <= (pos[:, None] // CHUNK)
    w = jnp.where(mask[None], sp_w[l], 0.0)
    w_pairs = jnp.concatenate([w[0::2], w[1::2]], axis=-1).astype(BF16)
    sp_bias = jnp.repeat(sp_b[l].T, HEAD_DIM, axis=1)
    return (
        norm1_g[l].reshape(1, D_MODEL), norm2_g[l].reshape(1, D_MODEL),
        w_in[l].astype(BF16),
        conv_w[l], conv_b[l].reshape(1, D_CONV), sp_norm_g[l].reshape(1, D_SPATIAL),
        w_pairs, sp_bias,
        mix_out_g[l].reshape(1, D_MODEL),
        w_out[l].astype(BF16),
    ) + shared


def kernel(x, c, w_ada, b_ada, norm1_g, norm2_g, w_in, conv_w, conv_b, sp_norm_g, sp_w, sp_b, mix_out_g, w_out, router_w, router_b, exp_w_gate, exp_w_up, exp_w_down, final_g):
    ada = _ada_table(c, w_ada, b_ada).reshape(DEPTH, BATCH, N_ADA, D_MODEL)

    rw = jnp.pad(router_w, ((0, 0), (0, LANES - N_EXPERTS)))
    rw_hi = rw.astype(BF16)
    rw_lo = (rw - rw_hi.astype(F32)).astype(BF16)
    head_of = jnp.arange(D_MODEL) // HEAD_DIM
    hred = (head_of[:, None] == jnp.arange(LANES)[None, :]).astype(F32) / HEAD_DIM
    hexp_half = (jnp.arange(LANES)[:, None] == head_of[None, :]).astype(F32)
    hexp = jnp.concatenate([hexp_half, hexp_half], axis=0)
    tri = (jnp.arange(SB)[:, None] < jnp.arange(SB)[None, :])
    shared = (rw_hi, rw_lo, router_b.reshape(N_EXPERTS, 1), hred.astype(BF16), hexp.astype(BF16),
              tri.astype(BF16))
    fg = final_g.reshape(1, D_MODEL)

    xc = x.reshape(N_TOK, D_MODEL)
    for l in range(DEPTH):
        consts = _layer_consts(l, norm1_g, norm2_g, w_in, conv_w, conv_b, sp_norm_g, sp_w, sp_b,
                               mix_out_g, w_out, shared)
        x1, h2p, rt, cnt = _mix_call(xc, ada[l], consts)
        a0, a1, tile_blk, e_start, e_count = _routing_tables(rt, cnt)
        xs = _dispatch_call(a0, a1, h2p.reshape(N_TOK * ROW_TILES, LANES))
        ys = _expert_call(l, tile_blk, e_start, e_count,
                          xs.reshape(TOTAL_ROWS // SUBLANES, ROW_TILES, SUBLANES, LANES),
                          exp_w_gate, exp_w_up, exp_w_down)
        wcol = rt[2:4].T
        xc = _combine_call(a0, a1, ys.reshape(TOTAL_ROWS * ROW_TILES, LANES), x1, wcol, ada[l], fg,
                           final=(l == DEPTH - 1))
    return xc.reshape(BATCH, SEQ, D_MODEL)
```

```python
import functools

import jax
import jax.numpy as jnp
from jax import lax
from jax.experimental import pallas as pl
from jax.experimental.pallas import tpu as pltpu

F32 = jnp.float32
BF16 = jnp.bfloat16
U32 = jnp.uint32
I32 = jnp.int32

D_MODEL = 1024
BATCH = 8
SEQ = 2048
DEPTH = 4
N_TOK = BATCH * SEQ
CHUNK = 64
HEAD_DIM = 64
D_CONV = 512
D_SPATIAL = 512
N_SP_HEADS = 8
N_MIX_HEADS = 16
SP_BLOCK = 128
N_EXPERTS = 16
N_GROUPS = 4
PER_GROUP = 4
D_EXPERT = 512
N_ADA = 6
EPS = 1e-6

LANES = 128
SUBLANES = 8
HALF = D_MODEL // 2
ROW_TILES = HALF // LANES

TM = 512
SB = TM
TILES_PER_SEQ = SEQ // TM
ROUTE_CHUNK = 4096
N_CHUNKS = N_TOK // ROUTE_CHUNK
TILES_PER_CHUNK = ROUTE_CHUNK // TM
ROW_TILE = 128
ROW_GROUPS = ROW_TILE // SUBLANES
GROUP = 4
CHUNK_ROWS = 2 * ROUTE_CHUNK + N_EXPERTS * ROW_TILE
TOTAL_ROWS = N_CHUNKS * CHUNK_ROWS
MAX_ROW_TILES = TOTAL_ROWS // ROW_TILE
TD = 1024
TC = 512
VMEM_LIMIT = 58 * 1024 * 1024


def _gelu_tanh(x):
    c = 0.7978845608028654
    return 0.5 * x * (1.0 + jnp.tanh(c * (x + 0.044715 * (x * x * x))))


def _split_bf16(x):
    hi = x.astype(BF16)
    lo = (x - hi.astype(F32)).astype(BF16)
    return hi, lo


def _pack_halves(x):
    return pltpu.pack_elementwise([x[:, :HALF], x[:, HALF:]], packed_dtype=BF16)


def _unpack_halves(words):
    lo = pltpu.unpack_elementwise(words, index=0, packed_dtype=BF16, unpacked_dtype=F32)
    hi = pltpu.unpack_elementwise(words, index=1, packed_dtype=BF16, unpacked_dtype=F32)
    return lo, hi


def _ada_kernel(c_ref, w_ref, b_ref, o_ref):
    c = c_ref[...]
    ca = c * (1.0 / (1.0 + jnp.exp(-c)))
    c_hi, c_lo = _split_bf16(ca)
    w_hi, w_lo = _split_bf16(w_ref[0])
    acc = jnp.dot(c_hi, w_hi, preferred_element_type=F32)
    acc += jnp.dot(c_lo, w_hi, preferred_element_type=F32)
    acc += jnp.dot(c_hi, w_lo, preferred_element_type=F32)
    o_ref[0] = acc + b_ref[0]


def _ada_table(c, w_ada, b_ada):
    n_blk = N_ADA
    return pl.pallas_call(
        _ada_kernel,
        grid=(DEPTH, n_blk),
        in_specs=[
            pl.BlockSpec((BATCH, D_MODEL), lambda l, n: (0, 0)),
            pl.BlockSpec((1, D_MODEL, D_MODEL), lambda l, n: (l, 0, n)),
            pl.BlockSpec((1, 1, D_MODEL), lambda l, n: (l, 0, n)),
        ],
        out_specs=pl.BlockSpec((1, BATCH, D_MODEL), lambda l, n: (l, 0, n)),
        out_shape=jax.ShapeDtypeStruct((DEPTH, BATCH, N_ADA * D_MODEL), F32),
        name="ada_table",
    )(c, w_ada, b_ada.reshape(DEPTH, 1, N_ADA * D_MODEL))


def _mix_kernel(x_ref, ada_ref, n1g_ref, n2g_ref, win_ref, cw_ref, cb_ref, spg_ref, spw_ref,
                spbias_ref, mog_ref, wout_ref, rwh_ref, rwl_ref, rb_ref, hred_ref, hexp_ref,
                tri_ref, x1_ref, h2p_ref, rt_ref, cnt_ref, z_scr, cnt_scr):
    i = pl.program_id(0)

    @pl.when(i % TILES_PER_SEQ == 0)
    def _():
        z_scr[pl.ds(0, SUBLANES), :] = jnp.zeros((SUBLANES, D_CONV), F32)

    @pl.when(i % TILES_PER_CHUNK == 0)
    def _():
        cnt_scr[...] = jnp.zeros((N_EXPERTS, LANES), F32)

    for sb in range(TM // SB):
        _mix_rows(sb * SB, x_ref, ada_ref, n1g_ref, n2g_ref, win_ref, cw_ref, cb_ref, spg_ref,
                  spw_ref, spbias_ref, mog_ref, wout_ref, rwh_ref, rwl_ref, rb_ref, hred_ref,
                  hexp_ref, tri_ref, x1_ref, h2p_ref, rt_ref, z_scr, cnt_scr)
    z_scr[pl.ds(0, SUBLANES), :] = z_scr[pl.ds(TM, SUBLANES), :]
    cnt_ref[0] = cnt_scr[...]


def _mix_rows(r0, x_ref, ada_ref, n1g_ref, n2g_ref, win_ref, cw_ref, cb_ref, spg_ref, spw_ref,
              spbias_ref, mog_ref, wout_ref, rwh_ref, rwl_ref, rb_ref, hred_ref, hexp_ref,
              tri_ref, x1_ref, h2p_ref, rt_ref, z_scr, cnt_scr):
    x = x_ref[pl.ds(r0, SB), :]
    ada = ada_ref[0]
    sh1, sc1, g1 = ada[0:1], ada[1:2], ada[2:3]
    sh2, sc2 = ada[3:4], ada[4:5]

    rinv = lax.rsqrt(jnp.mean(x * x, axis=-1, keepdims=True) + EPS)
    h = (x * rinv * n1g_ref[...]) * (1.0 + sc1) + sh1
    hb = h.astype(BF16)

    def proj(k):
        return jnp.dot(hb, win_ref[:, k * 512:(k + 1) * 512], preferred_element_type=F32)

    z = proj(1) * proj(2)
    z_scr[pl.ds(SUBLANES + r0, SB), :] = z
    z1 = z_scr[pl.ds(SUBLANES + r0 - 1, SB), :]
    z2 = z_scr[pl.ds(SUBLANES + r0 - 2, SB), :]
    cw = cw_ref[...]
    acc = cb_ref[...] + cw[0:1] * z2
    acc = acc + cw[1:2] * z1
    acc = acc + cw[2:3] * z
    y_conv = proj(0) * acc

    ug = _gelu_tanh(proj(3))
    vg = _gelu_tanh(proj(4))
    vr = lax.rsqrt(jnp.mean(vg * vg, axis=-1, keepdims=True) + EPS)
    vn = (vg * vr * spg_ref[...]).astype(BF16)
    lane = lax.broadcasted_iota(I32, (1, LANES), 1)
    first_head = lane < HEAD_DIM
    zero = jnp.zeros((), BF16)
    s_rows = []
    for blk in range(SB // SP_BLOCK):
        cols = []
        for p in range(N_SP_HEADS // 2):
            vv = vn[blk * SP_BLOCK:(blk + 1) * SP_BLOCK, p * LANES:(p + 1) * LANES]
            rhs = jnp.concatenate([jnp.where(first_head, vv, zero), jnp.where(first_head, zero, vv)], axis=0)
            cols.append(jnp.dot(spw_ref[p], rhs, preferred_element_type=F32))
        s_rows.append(jnp.concatenate(cols, axis=-1) + spbias_ref[...])
    y_sp = ug * jnp.concatenate(s_rows, axis=0)

    y = jnp.concatenate([y_conv, y_sp], axis=-1)
    ms = jnp.dot((y * y).astype(BF16), hred_ref[...], preferred_element_type=F32)
    r_hi, r_lo = _split_bf16(lax.rsqrt(ms + EPS))
    scale = jnp.dot(jnp.concatenate([r_hi, r_lo], axis=-1), hexp_ref[...], preferred_element_type=F32)
    yn = (y * scale * mog_ref[...]).astype(BF16)
    x1 = x + g1 * jnp.dot(yn, wout_ref[...], preferred_element_type=F32)
    x1_ref[pl.ds(r0, SB), :] = x1

    r2 = lax.rsqrt(jnp.mean(x1 * x1, axis=-1, keepdims=True) + EPS)
    h2 = (x1 * r2 * n2g_ref[...]) * (1.0 + sc2) + sh2
    h2_hi, h2_lo = _split_bf16(h2)
    packed = _pack_halves(h2_hi.astype(F32))
    for j in range(ROW_TILES):
        h2p_ref[pl.ds(r0 // SUBLANES, SB // SUBLANES), j, :, :] = (
            packed[:, j * LANES:(j + 1) * LANES].reshape(SB // SUBLANES, SUBLANES, LANES))

    logits = jnp.dot(h2_hi, rwh_ref[...], preferred_element_type=F32)
    logits += jnp.dot(h2_lo, rwh_ref[...], preferred_element_type=F32)
    logits += jnp.dot(h2_hi, rwl_ref[...], preferred_element_type=F32)
    lg = logits.T[0:N_EXPERTS, :]
    scores = 1.0 / (1.0 + jnp.exp(-lg))
    sel = scores + rb_ref[...]

    gs = []
    for g in range(N_GROUPS):
        a, b, c, d = (sel[g * PER_GROUP + k:g * PER_GROUP + k + 1, :] for k in range(PER_GROUP))
        hi1, lo1 = jnp.maximum(a, b), jnp.minimum(a, b)
        hi2, lo2 = jnp.maximum(c, d), jnp.minimum(c, d)
        gs.append(jnp.maximum(hi1, hi2) + jnp.maximum(jnp.minimum(hi1, hi2), jnp.maximum(lo1, lo2)))
    best = gs[0]
    top_group = jnp.zeros((1, SB), I32)
    for g in range(1, N_GROUPS):
        better = gs[g] > best
        best = jnp.where(better, gs[g], best)
        top_group = jnp.where(better, g, top_group)

    eid_i = lax.broadcasted_iota(I32, (N_EXPERTS, SB), 0)
    eid = eid_i.astype(F32)
    neg_inf = jnp.float32(-jnp.inf)
    masked = jnp.where((eid_i // PER_GROUP) == top_group, sel, neg_inf)
    m1 = jnp.max(masked, axis=0, keepdims=True)
    i1 = jnp.min(jnp.where(masked == m1, eid, float(N_EXPERTS)), axis=0, keepdims=True)
    masked2 = jnp.where(eid == i1, neg_inf, masked)
    m2 = jnp.max(masked2, axis=0, keepdims=True)
    i2 = jnp.min(jnp.where(masked2 == m2, eid, float(N_EXPERTS)), axis=0, keepdims=True)
    pick1 = eid == i1
    pick2 = eid == i2
    w1 = jnp.sum(jnp.where(pick1, scores, 0.0), axis=0, keepdims=True)
    w2 = jnp.sum(jnp.where(pick2, scores, 0.0), axis=0, keepdims=True)
    wsum = w1 + w2
    c1 = w1 / wsum
    c2 = w2 / wsum

    assigned = jnp.where(pick1 | pick2, 1.0, 0.0)
    before = jnp.dot(assigned.astype(BF16), tri_ref[...], preferred_element_type=F32)
    carry = cnt_scr[...]
    rank = before + carry[:, 0:1]
    rank1 = jnp.sum(jnp.where(pick1, rank, 0.0), axis=0, keepdims=True)
    rank2 = jnp.sum(jnp.where(pick2, rank, 0.0), axis=0, keepdims=True)
    cnt_scr[...] = carry + jnp.sum(assigned, axis=1, keepdims=True)
    unused = jnp.zeros((1, SB), F32)
    for r, row in enumerate((i1, i2, c1, c2, rank1, rank2, unused, unused)):
        rt_ref[r:r + 1, pl.ds(r0, SB)] = row


def _mix_call(x2d, ada_l, consts):
    n_tiles = N_TOK // TM
    full = lambda shape: pl.BlockSpec(shape, lambda i: (0,) * len(shape))
    in_specs = [
        pl.BlockSpec((TM, D_MODEL), lambda i: (i, 0)),
        pl.BlockSpec((1, N_ADA, D_MODEL), lambda i: (i // TILES_PER_SEQ, 0, 0)),
        full((1, D_MODEL)), full((1, D_MODEL)),
        full((D_MODEL, 5 * 512)),
        full((3, D_CONV)), full((1, D_CONV)), full((1, D_SPATIAL)),
        full((N_SP_HEADS // 2, SP_BLOCK, 2 * SP_BLOCK)),
        full((SP_BLOCK, D_SPATIAL)),
        full((1, D_MODEL)),
        full((D_MODEL, D_MODEL)),
        full((D_MODEL, LANES)), full((D_MODEL, LANES)), full((N_EXPERTS, 1)),
        full((D_MODEL, LANES)), full((2 * LANES, D_MODEL)),
        full((SB, SB)),
    ]
    out_specs = [
        pl.BlockSpec((TM, D_MODEL), lambda i: (i, 0)),
        pl.BlockSpec((TM // SUBLANES, ROW_TILES, SUBLANES, LANES), lambda i: (i, 0, 0, 0)),
        pl.BlockSpec((SUBLANES, TM), lambda i: (0, i)),
        pl.BlockSpec((1, N_EXPERTS, LANES), lambda i: (i, 0, 0)),
    ]
    out_shape = [
        jax.ShapeDtypeStruct((N_TOK, D_MODEL), F32),
        jax.ShapeDtypeStruct((N_TOK // SUBLANES, ROW_TILES, SUBLANES, LANES), U32),
        jax.ShapeDtypeStruct((SUBLANES, N_TOK), F32),
        jax.ShapeDtypeStruct((n_tiles, N_EXPERTS, LANES), F32),
    ]
    return pl.pallas_call(
        _mix_kernel,
        grid=(n_tiles,),
        in_specs=in_specs,
        out_specs=out_specs,
        out_shape=out_shape,
        scratch_shapes=[pltpu.VMEM((TM + SUBLANES, D_CONV), F32), pltpu.VMEM((N_EXPERTS, LANES), F32)],
        compiler_params=pltpu.CompilerParams(
            dimension_semantics=("arbitrary",), vmem_limit_bytes=VMEM_LIMIT),
        name="mix",
    )(x2d, ada_l, *consts)


def _row_addr(t):
    return (t // SUBLANES) * (ROW_TILES * SUBLANES) + t % SUBLANES


def _dispatch_kernel(a0_ref, a1_ref, h2p_ref, xs_ref):
    c = pl.program_id(0)
    s = pl.program_id(1)

    @pl.when(s == 0)
    def _():
        xs_ref[...] = jnp.zeros(xs_ref.shape, U32)

    base = c * ROUTE_CHUNK + s * TD

    def body(t8, carry):
        for k in range(SUBLANES):
            row = h2p_ref[pl.ds(t8 * (ROW_TILES * SUBLANES) + k, ROW_TILES, stride=SUBLANES), :]
            t = base + t8 * SUBLANES + k
            xs_ref[pl.ds(a0_ref[t], ROW_TILES, stride=SUBLANES), :] = row
            xs_ref[pl.ds(a1_ref[t], ROW_TILES, stride=SUBLANES), :] = row
        return carry

    lax.fori_loop(0, TD // SUBLANES, body, 0)


def _dispatch_call(a0, a1, h2p_2d):
    steps = ROUTE_CHUNK // TD
    return pl.pallas_call(
        _dispatch_kernel,
        grid_spec=pltpu.PrefetchScalarGridSpec(
            num_scalar_prefetch=2,
            grid=(N_CHUNKS, steps),
            in_specs=[pl.BlockSpec((TD * ROW_TILES, LANES), lambda c, s, a0, a1: (c * steps + s, 0))],
            out_specs=pl.BlockSpec((CHUNK_ROWS * ROW_TILES, LANES), lambda c, s, a0, a1: (c, 0)),
        ),
        out_shape=jax.ShapeDtypeStruct((TOTAL_ROWS * ROW_TILES, LANES), U32),
        compiler_params=pltpu.CompilerParams(
            dimension_semantics=("arbitrary", "arbitrary"), vmem_limit_bytes=VMEM_LIMIT),
        name="dispatch",
    )(a0, a1, h2p_2d)


def _expert_kernel(tb_ref, est_ref, ecnt_ref, xs_hbm, wg_ref, wu_ref, wd_ref, ys_hbm,
                   wg_s, wu_s, wd_s, xbuf, ybuf, sem_in, sem_out):
    e = pl.program_id(0)
    first = est_ref[e]
    n_tiles = ecnt_ref[e]
    n_groups = (n_tiles + GROUP - 1) // GROUP

    def in_copy(k, slot, t):
        rows = pl.ds(tb_ref[first + k] * ROW_GROUPS, ROW_GROUPS)
        return pltpu.make_async_copy(xs_hbm.at[rows], xbuf.at[slot, t], sem_in.at[slot, t])

    def out_copy(k, slot, t):
        rows = pl.ds(tb_ref[first + k] * ROW_GROUPS, ROW_GROUPS)
        return pltpu.make_async_copy(ybuf.at[slot, t], ys_hbm.at[rows], sem_out.at[slot, t])

    def for_group(p, slot, action, make):
        getattr(make(GROUP * p, slot, 0), action)()
        for t in range(1, GROUP):
            @pl.when(GROUP * p + t < n_tiles)
            def _():
                getattr(make(GROUP * p + t, slot, t), action)()

    xbuf[...] = jnp.zeros(xbuf.shape, U32)

    @pl.when(n_groups > 0)
    def _():
        for_group(0, 0, "start", in_copy)

    wg_s[...] = wg_ref[0, 0].astype(BF16)
    wu_s[...] = wu_ref[0, 0].astype(BF16)
    wd_s[...] = wd_ref[0, 0].astype(BF16)

    def compute(slot, tiles):
        los, his = [], []
        for j in range(ROW_TILES):
            words = jnp.concatenate(
                [xbuf[slot, t, :, j, :, :].reshape(ROW_TILE, LANES) for t in tiles], axis=0)
            lo, hi = _unpack_halves(words)
            los.append(lo)
            his.append(hi)
        xb = jnp.concatenate(los + his, axis=-1).astype(BF16)
        g = jnp.dot(xb, wg_s[...], preferred_element_type=F32)
        u = jnp.dot(xb, wu_s[...], preferred_element_type=F32)
        act = (g * (1.0 / (1.0 + jnp.exp(-g))) * u).astype(BF16)
        y = jnp.dot(act, wd_s[...], preferred_element_type=F32)
        packed = _pack_halves(y.astype(BF16).astype(F32))
        for n, t in enumerate(tiles):
            for j in range(ROW_TILES):
                ybuf[slot, t, :, j, :, :] = packed[n * ROW_TILE:(n + 1) * ROW_TILE,
                                                   j * LANES:(j + 1) * LANES].reshape(
                    ROW_GROUPS, SUBLANES, LANES)

    def body(p, carry):
        slot = p % 2
        for_group(p, slot, "wait", in_copy)

        @pl.when(p + 1 < n_groups)
        def _():
            for_group(p + 1, 1 - slot, "start", in_copy)

        @pl.when(p >= 2)
        def _():
            for_group(p - 2, slot, "wait", out_copy)

        compute(slot, (0, 1))
        compute(slot, (2, 3))
        for_group(p, slot, "start", out_copy)
        return carry

    lax.fori_loop(0, n_groups, body, 0)

    @pl.when(n_groups >= 2)
    def _():
        for_group(n_groups - 2, n_groups % 2, "wait", out_copy)

    @pl.when(n_groups >= 1)
    def _():
        for_group(n_groups - 1, (n_groups - 1) % 2, "wait", out_copy)


def _expert_call(layer, tile_blk, e_start, e_count, xs4, wg, wu, wd):
    group_buf = (2, GROUP, ROW_GROUPS, ROW_TILES, SUBLANES, LANES)
    return pl.pallas_call(
        _expert_kernel,
        grid_spec=pltpu.PrefetchScalarGridSpec(
            num_scalar_prefetch=3,
            grid=(N_EXPERTS,),
            in_specs=[
                pl.BlockSpec(memory_space=pl.ANY),
                pl.BlockSpec((1, 1, D_MODEL, D_EXPERT), lambda e, tb, es, ec: (layer, e, 0, 0)),
                pl.BlockSpec((1, 1, D_MODEL, D_EXPERT), lambda e, tb, es, ec: (layer, e, 0, 0)),
                pl.BlockSpec((1, 1, D_EXPERT, D_MODEL), lambda e, tb, es, ec: (layer, e, 0, 0)),
            ],
            out_specs=pl.BlockSpec(memory_space=pl.ANY),
            scratch_shapes=[
                pltpu.VMEM((D_MODEL, D_EXPERT), BF16),
                pltpu.VMEM((D_MODEL, D_EXPERT), BF16),
                pltpu.VMEM((D_EXPERT, D_MODEL), BF16),
                pltpu.VMEM(group_buf, U32),
                pltpu.VMEM(group_buf, U32),
                pltpu.SemaphoreType.DMA((2, GROUP)),
                pltpu.SemaphoreType.DMA((2, GROUP)),
            ],
        ),
        out_shape=jax.ShapeDtypeStruct((TOTAL_ROWS // SUBLANES, ROW_TILES, SUBLANES, LANES), U32),
        input_output_aliases={3: 0},
        compiler_params=pltpu.CompilerParams(
            dimension_semantics=("arbitrary",), vmem_limit_bytes=VMEM_LIMIT),
        name="experts",
    )(tile_blk, e_start, e_count, xs4, wg, wu, wd)


def _combine_kernel(a0_ref, a1_ref, ys_ref, x1_ref, w_ref, ada_ref, fg_ref, out_ref, r0_scr, r1_scr,
                    *, final):
    c = pl.program_id(0)
    s = pl.program_id(1)
    base = c * ROUTE_CHUNK + s * TC

    def body(t8, carry):
        for k in range(SUBLANES):
            t = base + t8 * SUBLANES + k
            dst = pl.ds(t8 * (ROW_TILES * SUBLANES) + k, ROW_TILES, stride=SUBLANES)
            r0_scr[dst, :] = ys_ref[pl.ds(a0_ref[t], ROW_TILES, stride=SUBLANES), :]
            r1_scr[dst, :] = ys_ref[pl.ds(a1_ref[t], ROW_TILES, stride=SUBLANES), :]
        return carry

    lax.fori_loop(0, TC // SUBLANES, body, 0)

    w = w_ref[...]
    w0 = w[:, 0:1]
    w1 = w[:, 1:2]
    g2 = ada_ref[0][5:6]
    n_grp = TC // SUBLANES
    los, his = [], []
    for j in range(ROW_TILES):
        parts0 = [r0_scr[pl.ds((tb * ROW_TILES + j) * SUBLANES, SUBLANES), :] for tb in range(n_grp)]
        parts1 = [r1_scr[pl.ds((tb * ROW_TILES + j) * SUBLANES, SUBLANES), :] for tb in range(n_grp)]
        lo0, hi0 = _unpack_halves(jnp.concatenate(parts0, axis=0))
        lo1, hi1 = _unpack_halves(jnp.concatenate(parts1, axis=0))
        los.append(w0 * lo0 + w1 * lo1)
        his.append(w0 * hi0 + w1 * hi1)
    moe = jnp.concatenate(los + his, axis=-1)
    x2 = x1_ref[...] + g2 * moe
    if final:
        r = lax.rsqrt(jnp.mean(x2 * x2, axis=-1, keepdims=True) + EPS)
        x2 = x2 * r * fg_ref[...]
    out_ref[...] = x2


def _combine_call(a0, a1, ys_2d, x1, wcol, ada_l, final_g, final):
    steps = ROUTE_CHUNK // TC
    tok = lambda c, s, a0, a1: (c * steps + s, 0)
    return pl.pallas_call(
        functools.partial(_combine_kernel, final=final),
        grid_spec=pltpu.PrefetchScalarGridSpec(
            num_scalar_prefetch=2,
            grid=(N_CHUNKS, steps),
            in_specs=[
                pl.BlockSpec((CHUNK_ROWS * ROW_TILES, LANES), lambda c, s, a0, a1: (c, 0)),
                pl.BlockSpec((TC, D_MODEL), tok),
                pl.BlockSpec((TC, 2), tok),
                pl.BlockSpec((1, N_ADA, D_MODEL),
                             lambda c, s, a0, a1: ((c * ROUTE_CHUNK + s * TC) // SEQ, 0, 0)),
                pl.BlockSpec((1, D_MODEL), lambda c, s, a0, a1: (0, 0)),
            ],
            out_specs=pl.BlockSpec((TC, D_MODEL), tok),
            scratch_shapes=[pltpu.VMEM((TC * ROW_TILES, LANES), U32),
                            pltpu.VMEM((TC * ROW_TILES, LANES), U32)],
        ),
        out_shape=jax.ShapeDtypeStruct((N_TOK, D_MODEL), F32),
        compiler_params=pltpu.CompilerParams(
            dimension_semantics=("arbitrary", "arbitrary"), vmem_limit_bytes=VMEM_LIMIT),
        name="combine_final" if final else "combine",
    )(a0, a1, ys_2d, x1, wcol, ada_l, final_g)


def _routing_tables(rt, cnt):
    e0 = rt[0].astype(I32)
    e1 = rt[1].astype(I32)
    rank0 = rt[4].astype(I32)
    rank1 = rt[5].astype(I32)
    counts = cnt[TILES_PER_CHUNK - 1::TILES_PER_CHUNK, :, 0].astype(I32)
    n_tiles = (counts + ROW_TILE - 1) // ROW_TILE
    tile_off = jnp.cumsum(n_tiles, axis=1) - n_tiles
    row_off = tile_off * ROW_TILE
    experts = jnp.arange(N_EXPERTS, dtype=I32)[None, :, None]
    off_tok = row_off[:, :, None]
    pick = lambda e: jnp.sum(
        jnp.where(e.reshape(N_CHUNKS, 1, ROUTE_CHUNK) == experts, off_tok, 0), axis=1).reshape(N_TOK)
    slot0 = pick(e0) + rank0
    slot1 = pick(e1) + rank1
    a0 = _row_addr(slot0)
    a1 = _row_addr(slot1)

    nt_flat = n_tiles.T.reshape(-1)
    ends = jnp.cumsum(nt_flat)
    starts = ends - nt_flat
    total = ends[-1]
    step = jnp.minimum(jnp.arange(MAX_ROW_TILES, dtype=I32), total - 1)
    seg = jnp.sum((ends[None, :] <= step[:, None]).astype(I32), axis=1)
    in_seg = seg[:, None] == jnp.arange(N_EXPERTS * N_CHUNKS, dtype=I32)[None, :]
    k = step - jnp.sum(jnp.where(in_seg, starts[None, :], 0), axis=1)
    c = seg % N_CHUNKS
    tile_off_flat = tile_off.T.reshape(-1)
    tile_blk = c * (CHUNK_ROWS // ROW_TILE) + jnp.sum(jnp.where(in_seg, tile_off_flat[None, :], 0), axis=1) + k
    e_count = jnp.sum(n_tiles, axis=0)
    e_start = jnp.cumsum(e_count) - e_count
    return a0, a1, tile_blk.astype(I32), e_start.astype(I32), e_count.astype(I32)


def _layer_consts(l, norm1_g, norm2_g, w_in, conv_w, conv_b, sp_norm_g, sp_w, sp_b, mix_out_g,
                  w_out, shared):
    pos = jnp.arange(SP_BLOCK)
    mask = (pos[None, :] // CHUNK) <= (pos[:, None] // CHUNK)
    w = jnp.where(mask[None], sp_w[l], 0.0)
    w_pairs = jnp.concatenate([w[0::2], w[1::2]], axis=-1).astype(BF16)
    sp_bias = jnp.repeat(sp_b[l].T, HEAD_DIM, axis=1)
    return (
        norm1_g[l].reshape(1, D_MODEL), norm2_g[l].reshape(1, D_MODEL),
        w_in[l].astype(BF16),
        conv_w[l], conv_b[l].reshape(1, D_CONV), sp_norm_g[l].reshape(1, D_SPATIAL),
        w_pairs, sp_bias,
        mix_out_g[l].reshape(1, D_MODEL),
        w_out[l].astype(BF16),
    ) + shared


def kernel(x, c, w_ada, b_ada, norm1_g, norm2_g, w_in, conv_w, conv_b, sp_norm_g, sp_w, sp_b, mix_out_g, w_out, router_w, router_b, exp_w_gate, exp_w_up, exp_w_down, final_g):
    ada = _ada_table(c, w_ada, b_ada).reshape(DEPTH, BATCH, N_ADA, D_MODEL)

    rw = jnp.pad(router_w, ((0, 0), (0, LANES - N_EXPERTS)))
    rw_hi = rw.astype(BF16)
    rw_lo = (rw - rw_hi.astype(F32)).astype(BF16)
    head_of = jnp.arange(D_MODEL) // HEAD_DIM
    hred = (head_of[:, None] == jnp.arange(LANES)[None, :]).astype(F32) / HEAD_DIM
    hexp_half = (jnp.arange(LANES)[:, None] == head_of[None, :]).astype(F32)
    hexp = jnp.concatenate([hexp_half, hexp_half], axis=0)
    tri = (jnp.arange(SB)[:, None] < jnp.arange(SB)[None, :])
    shared = (rw_hi, rw_lo, router_b.reshape(N_EXPERTS, 1), hred.astype(BF16), hexp.astype(BF16),
              tri.astype(BF16))
    fg = final_g.reshape(1, D_MODEL)

    xc = x.reshape(N_TOK, D_MODEL)
    for l in range(DEPTH):
        consts = _layer_consts(l, norm1_g, norm2_g, w_in, conv_w, conv_b, sp_norm_g, sp_w, sp_b,
                               mix_out_g, w_out, shared)
        x1, h2p, rt, cnt = _mix_call(xc, ada[l], consts)
        a0, a1, tile_blk, e_start, e_count = _routing_tables(rt, cnt)
        xs = _dispatch_call(a0, a1, h2p.reshape(N_TOK * ROW_TILES, LANES))
        ys = _expert_call(l, tile_blk, e_start, e_count,
                          xs.reshape(TOTAL_ROWS // SUBLANES, ROW_TILES, SUBLANES, LANES),
                          exp_w_gate, exp_w_up, exp_w_down)
        wcol = rt[2:4].T
        xc = _combine_call(a0, a1, ys.reshape(TOTAL_ROWS * ROW_TILES, LANES), x1, wcol, ada[l], fg,
                           final=(l == DEPTH - 1))
    return xc.reshape(BATCH, SEQ, D_MODEL)
```

```python
import functools

import jax
import jax.numpy as jnp
from jax import lax
from jax.experimental import pallas as pl
from jax.experimental.pallas import tpu as pltpu

F32 = jnp.float32
BF16 = jnp.bfloat16
U32 = jnp.uint32
I32 = jnp.int32

D_MODEL = 1024
BATCH = 8
SEQ = 2048
DEPTH = 4
N_TOK = BATCH * SEQ
CHUNK = 64
HEAD_DIM = 64
D_CONV = 512
D_SPATIAL = 512
N_SP_HEADS = 8
N_MIX_HEADS = 16
SP_BLOCK = 128
N_EXPERTS = 16
N_GROUPS = 4
PER_GROUP = 4
D_EXPERT = 512
N_ADA = 6
EPS = 1e-6

LANES = 128
SUBLANES = 8
HALF = D_MODEL // 2
ROW_TILES = HALF // LANES

TM = 512
SB = TM // 2
TILES_PER_SEQ = SEQ // TM
ROUTE_CHUNK = 4096
N_CHUNKS = N_TOK // ROUTE_CHUNK
TILES_PER_CHUNK = ROUTE_CHUNK // TM
ROW_TILE = 128
ROW_GROUPS = ROW_TILE // SUBLANES
GROUP = 4
CHUNK_ROWS = 2 * ROUTE_CHUNK + N_EXPERTS * ROW_TILE
TOTAL_ROWS = N_CHUNKS * CHUNK_ROWS
MAX_ROW_TILES = TOTAL_ROWS // ROW_TILE
TD = 1024
TC = 512
VMEM_LIMIT = 58 * 1024 * 1024


def _gelu_tanh(x):
    c = 0.7978845608028654
    return 0.5 * x * (1.0 + jnp.tanh(c * (x + 0.044715 * (x * x * x))))


def _split_bf16(x):
    hi = x.astype(BF16)
    lo = (x - hi.astype(F32)).astype(BF16)
    return hi, lo


def _pack_halves(x):
    return pltpu.pack_elementwise([x[:, :HALF], x[:, HALF:]], packed_dtype=BF16)


def _unpack_halves(words):
    lo = pltpu.unpack_elementwise(words, index=0, packed_dtype=BF16, unpacked_dtype=F32)
    hi = pltpu.unpack_elementwise(words, index=1, packed_dtype=BF16, unpacked_dtype=F32)
    return lo, hi


def _ada_kernel(c_ref, w_ref, b_ref, o_ref):
    c = c_ref[...]
    ca = c * (1.0 / (1.0 + jnp.exp(-c)))
    c_hi, c_lo = _split_bf16(ca)
    w_hi, w_lo = _split_bf16(w_ref[0])
    acc = jnp.dot(c_hi, w_hi, preferred_element_type=F32)
    acc += jnp.dot(c_lo, w_hi, preferred_element_type=F32)
    acc += jnp.dot(c_hi, w_lo, preferred_element_type=F32)
    o_ref[0] = acc + b_ref[0]


def _ada_table(c, w_ada, b_ada):
    n_blk = N_ADA
    return pl.pallas_call(
        _ada_kernel,
        grid=(DEPTH, n_blk),
        in_specs=[
            pl.BlockSpec((BATCH, D_MODEL), lambda l, n: (0, 0)),
            pl.BlockSpec((1, D_MODEL, D_MODEL), lambda l, n: (l, 0, n)),
            pl.BlockSpec((1, 1, D_MODEL), lambda l, n: (l, 0, n)),
        ],
        out_specs=pl.BlockSpec((1, BATCH, D_MODEL), lambda l, n: (l, 0, n)),
        out_shape=jax.ShapeDtypeStruct((DEPTH, BATCH, N_ADA * D_MODEL), F32),
        name="ada_table",
    )(c, w_ada, b_ada.reshape(DEPTH, 1, N_ADA * D_MODEL))


def _mix_kernel(x_ref, ada_ref, n1g_ref, n2g_ref, win_ref, cw_ref, cb_ref, spg_ref, spw_ref,
                spbias_ref, mog_ref, wout_ref, rwh_ref, rwl_ref, rb_ref, hred_ref, hexp_ref,
                tri_ref, x1_ref, h2p_ref, rt_ref, cnt_ref, z_scr, cnt_scr):
    i = pl.program_id(0)

    @pl.when(i % TILES_PER_SEQ == 0)
    def _():
        z_scr[pl.ds(0, SUBLANES), :] = jnp.zeros((SUBLANES, D_CONV), F32)

    @pl.when(i % TILES_PER_CHUNK == 0)
    def _():
        cnt_scr[...] = jnp.zeros((N_EXPERTS, LANES), F32)

    chains = [
        _mix_rows(sb * SB, x_ref, ada_ref, n1g_ref, n2g_ref, win_ref, cw_ref, cb_ref, spg_ref,
                  spw_ref, spbias_ref, mog_ref, wout_ref, rwh_ref, rwl_ref, rb_ref, hred_ref,
                  hexp_ref, tri_ref, x1_ref, h2p_ref, rt_ref, z_scr, cnt_scr)
        for sb in range(TM // SB)]
    for c in MIX_STAGE_ORDER:
        next(chains[c])
    z_scr[pl.ds(0, SUBLANES), :] = z_scr[pl.ds(TM, SUBLANES), :]
    cnt_ref[0] = cnt_scr[...]


MIX_STAGE_ORDER = (0, 0, 1, 0, 1, 0, 1, 0, 0, 1, 1, 0, 0, 1, 1, 1)


def _mix_rows(r0, x_ref, ada_ref, n1g_ref, n2g_ref, win_ref, cw_ref, cb_ref, spg_ref, spw_ref,
              spbias_ref, mog_ref, wout_ref, rwh_ref, rwl_ref, rb_ref, hred_ref, hexp_ref,
              tri_ref, x1_ref, h2p_ref, rt_ref, z_scr, cnt_scr):
    x = x_ref[pl.ds(r0, SB), :]
    ada = ada_ref[0]
    sh1, sc1, g1 = ada[0:1], ada[1:2], ada[2:3]
    sh2, sc2 = ada[3:4], ada[4:5]

    rinv = lax.rsqrt(jnp.mean(x * x, axis=-1, keepdims=True) + EPS)
    h = (x * rinv * n1g_ref[...]) * (1.0 + sc1) + sh1
    hb = h.astype(BF16)
    yield

    def proj(k):
        return jnp.dot(hb, win_ref[:, k * 512:(k + 1) * 512], preferred_element_type=F32)

    z = proj(1) * proj(2)
    z_scr[pl.ds(SUBLANES + r0, SB), :] = z
    z1 = z_scr[pl.ds(SUBLANES + r0 - 1, SB), :]
    z2 = z_scr[pl.ds(SUBLANES + r0 - 2, SB), :]
    yield
    p_b, p_u, p_v = proj(0), proj(3), proj(4)
    yield
    cw = cw_ref[...]
    acc = cb_ref[...] + cw[0:1] * z2
    acc = acc + cw[1:2] * z1
    acc = acc + cw[2:3] * z
    y_conv = p_b * acc

    ug = _gelu_tanh(p_u)
    vg = _gelu_tanh(p_v)
    vr = lax.rsqrt(jnp.mean(vg * vg, axis=-1, keepdims=True) + EPS)
    vn = (vg * vr * spg_ref[...]).astype(BF16)
    lane = lax.broadcasted_iota(I32, (1, LANES), 1)
    first_head = lane < HEAD_DIM
    zero = jnp.zeros((), BF16)
    s_rows = []
    for blk in range(SB // SP_BLOCK):
        cols = []
        for p in range(N_SP_HEADS // 2):
            vv = vn[blk * SP_BLOCK:(blk + 1) * SP_BLOCK, p * LANES:(p + 1) * LANES]
            rhs = jnp.concatenate([jnp.where(first_head, vv, zero), jnp.where(first_head, zero, vv)], axis=0)
            cols.append(jnp.dot(spw_ref[p], rhs, preferred_element_type=F32))
        s_rows.append(jnp.concatenate(cols, axis=-1) + spbias_ref[...])
    y_sp = ug * jnp.concatenate(s_rows, axis=0)
    yield

    y = jnp.concatenate([y_conv, y_sp], axis=-1)
    ms = jnp.dot((y * y).astype(BF16), hred_ref[...], preferred_element_type=F32)
    r_hi, r_lo = _split_bf16(lax.rsqrt(ms + EPS))
    scale = jnp.dot(jnp.concatenate([r_hi, r_lo], axis=-1), hexp_ref[...], preferred_element_type=F32)
    yn = (y * scale * mog_ref[...]).astype(BF16)
    yield
    x1 = x + g1 * jnp.dot(yn, wout_ref[...], preferred_element_type=F32)
    x1_ref[pl.ds(r0, SB), :] = x1
    yield

    r2 = lax.rsqrt(jnp.mean(x1 * x1, axis=-1, keepdims=True) + EPS)
    h2 = (x1 * r2 * n2g_ref[...]) * (1.0 + sc2) + sh2
    h2_hi, h2_lo = _split_bf16(h2)
    packed = _pack_halves(h2_hi.astype(F32))
    for j in range(ROW_TILES):
        h2p_ref[pl.ds(r0 // SUBLANES, SB // SUBLANES), j, :, :] = (
            packed[:, j * LANES:(j + 1) * LANES].reshape(SB // SUBLANES, SUBLANES, LANES))
    yield

    logits = jnp.dot(h2_hi, rwh_ref[...], preferred_element_type=F32)
    logits += jnp.dot(h2_lo, rwh_ref[...], preferred_element_type=F32)
    logits += jnp.dot(h2_hi, rwl_ref[...], preferred_element_type=F32)
    lg = logits.T[0:N_EXPERTS, :]
    scores = 1.0 / (1.0 + jnp.exp(-lg))
    sel = scores + rb_ref[...]

    gs = []
    for g in range(N_GROUPS):
        a, b, c, d = (sel[g * PER_GROUP + k:g * PER_GROUP + k + 1, :] for k in range(PER_GROUP))
        hi1, lo1 = jnp.maximum(a, b), jnp.minimum(a, b)
        hi2, lo2 = jnp.maximum(c, d), jnp.minimum(c, d)
        gs.append(jnp.maximum(hi1, hi2) + jnp.maximum(jnp.minimum(hi1, hi2), jnp.maximum(lo1, lo2)))
    best = gs[0]
    top_group = jnp.zeros((1, SB), I32)
    for g in range(1, N_GROUPS):
        better = gs[g] > best
        best = jnp.where(better, gs[g], best)
        top_group = jnp.where(better, g, top_group)

    eid_i = lax.broadcasted_iota(I32, (N_EXPERTS, SB), 0)
    eid = eid_i.astype(F32)
    neg_inf = jnp.float32(-jnp.inf)
    masked = jnp.where((eid_i // PER_GROUP) == top_group, sel, neg_inf)
    m1 = jnp.max(masked, axis=0, keepdims=True)
    i1 = jnp.min(jnp.where(masked == m1, eid, float(N_EXPERTS)), axis=0, keepdims=True)
    masked2 = jnp.where(eid == i1, neg_inf, masked)
    m2 = jnp.max(masked2, axis=0, keepdims=True)
    i2 = jnp.min(jnp.where(masked2 == m2, eid, float(N_EXPERTS)), axis=0, keepdims=True)
    pick1 = eid == i1
    pick2 = eid == i2
    w1 = jnp.sum(jnp.where(pick1, scores, 0.0), axis=0, keepdims=True)
    w2 = jnp.sum(jnp.where(pick2, scores, 0.0), axis=0, keepdims=True)
    wsum = w1 + w2
    c1 = w1 / wsum
    c2 = w2 / wsum

    assigned = jnp.where(pick1 | pick2, 1.0, 0.0)
    before = jnp.dot(assigned.astype(BF16), tri_ref[...], preferred_element_type=F32)
    carry = cnt_scr[...]
    rank = before + carry[:, 0:1]
    rank1 = jnp.sum(jnp.where(pick1, rank, 0.0), axis=0, keepdims=True)
    rank2 = jnp.sum(jnp.where(pick2, rank, 0.0), axis=0, keepdims=True)
    cnt_scr[...] = carry + jnp.sum(assigned, axis=1, keepdims=True)
    unused = jnp.zeros((1, SB), F32)
    for r, row in enumerate((i1, i2, c1, c2, rank1, rank2, unused, unused)):
        rt_ref[r:r + 1, pl.ds(r0, SB)] = row
    yield


def _mix_call(x2d, ada_l, consts):
    n_tiles = N_TOK // TM
    full = lambda shape: pl.BlockSpec(shape, lambda i: (0,) * len(shape))
    in_specs = [
        pl.BlockSpec((TM, D_MODEL), lambda i: (i, 0)),
        pl.BlockSpec((1, N_ADA, D_MODEL), lambda i: (i // TILES_PER_SEQ, 0, 0)),
        full((1, D_MODEL)), full((1, D_MODEL)),
        full((D_MODEL, 5 * 512)),
        full((3, D_CONV)), full((1, D_CONV)), full((1, D_SPATIAL)),
        full((N_SP_HEADS // 2, SP_BLOCK, 2 * SP_BLOCK)),
        full((SP_BLOCK, D_SPATIAL)),
        full((1, D_MODEL)),
        full((D_MODEL, D_MODEL)),
        full((D_MODEL, LANES)), full((D_MODEL, LANES)), full((N_EXPERTS, 1)),
        full((D_MODEL, LANES)), full((2 * LANES, D_MODEL)),
        full((SB, SB)),
    ]
    out_specs = [
        pl.BlockSpec((TM, D_MODEL), lambda i: (i, 0)),
        pl.BlockSpec((TM // SUBLANES, ROW_TILES, SUBLANES, LANES), lambda i: (i, 0, 0, 0)),
        pl.BlockSpec((SUBLANES, TM), lambda i: (0, i)),
        pl.BlockSpec((1, N_EXPERTS, LANES), lambda i: (i, 0, 0)),
    ]
    out_shape = [
        jax.ShapeDtypeStruct((N_TOK, D_MODEL), F32),
        jax.ShapeDtypeStruct((N_TOK // SUBLANES, ROW_TILES, SUBLANES, LANES), U32),
        jax.ShapeDtypeStruct((SUBLANES, N_TOK), F32),
        jax.ShapeDtypeStruct((n_tiles, N_EXPERTS, LANES), F32),
    ]
    return pl.pallas_call(
        _mix_kernel,
        grid=(n_tiles,),
        in_specs=in_specs,
        out_specs=out_specs,
        out_shape=out_shape,
        scratch_shapes=[pltpu.VMEM((TM + SUBLANES, D_CONV), F32), pltpu.VMEM((N_EXPERTS, LANES), F32)],
        compiler_params=pltpu.CompilerParams(
            dimension_semantics=("arbitrary",), vmem_limit_bytes=VMEM_LIMIT),
        name="mix",
    )(x2d, ada_l, *consts)


def _row_addr(t):
    return (t // SUBLANES) * (ROW_TILES * SUBLANES) + t % SUBLANES


def _dispatch_kernel(a0_ref, a1_ref, h2p_ref, xs_ref):
    c = pl.program_id(0)
    s = pl.program_id(1)

    @pl.when(s == 0)
    def _():
        xs_ref[...] = jnp.zeros(xs_ref.shape, U32)

    base = c * ROUTE_CHUNK + s * TD

    def body(t8, carry):
        for k in range(SUBLANES):
            row = h2p_ref[pl.ds(t8 * (ROW_TILES * SUBLANES) + k, ROW_TILES, stride=SUBLANES), :]
            t = base + t8 * SUBLANES + k
            xs_ref[pl.ds(a0_ref[t], ROW_TILES, stride=SUBLANES), :] = row
            xs_ref[pl.ds(a1_ref[t], ROW_TILES, stride=SUBLANES), :] = row
        return carry

    lax.fori_loop(0, TD // SUBLANES, body, 0)


def _dispatch_call(a0, a1, h2p_2d):
    steps = ROUTE_CHUNK // TD
    return pl.pallas_call(
        _dispatch_kernel,
        grid_spec=pltpu.PrefetchScalarGridSpec(
            num_scalar_prefetch=2,
            grid=(N_CHUNKS, steps),
            in_specs=[pl.BlockSpec((TD * ROW_TILES, LANES), lambda c, s, a0, a1: (c * steps + s, 0))],
            out_specs=pl.BlockSpec((CHUNK_ROWS * ROW_TILES, LANES), lambda c, s, a0, a1: (c, 0)),
        ),
        out_shape=jax.ShapeDtypeStruct((TOTAL_ROWS * ROW_TILES, LANES), U32),
        compiler_params=pltpu.CompilerParams(
            dimension_semantics=("arbitrary", "arbitrary"), vmem_limit_bytes=VMEM_LIMIT),
        name="dispatch",
    )(a0, a1, h2p_2d)


def _expert_kernel(tb_ref, est_ref, ecnt_ref, xs_hbm, wg_ref, wu_ref, wd_ref, ys_hbm,
                   wg_s, wu_s, wd_s, xbuf, ybuf, sem_in, sem_out):
    e = pl.program_id(0)
    first = est_ref[e]
    n_tiles = ecnt_ref[e]
    n_groups = (n_tiles + GROUP - 1) // GROUP

    def in_copy(k, slot, t):
        rows = pl.ds(tb_ref[first + k] * ROW_GROUPS, ROW_GROUPS)
        return pltpu.make_async_copy(xs_hbm.at[rows], xbuf.at[slot, t], sem_in.at[slot, t])

    def out_copy(k, slot, t):
        rows = pl.ds(tb_ref[first + k] * ROW_GROUPS, ROW_GROUPS)
        return pltpu.make_async_copy(ybuf.at[slot, t], ys_hbm.at[rows], sem_out.at[slot, t])

    def act(copy, action):
        if action == "start":
            copy.start(priority=1)
        else:
            copy.wait()

    def for_group(p, slot, action, make):
        act(make(GROUP * p, slot, 0), action)
        for t in range(1, GROUP):
            @pl.when(GROUP * p + t < n_tiles)
            def _():
                act(make(GROUP * p + t, slot, t), action)

    xbuf[...] = jnp.zeros(xbuf.shape, U32)

    @pl.when(n_groups > 0)
    def _():
        for_group(0, 0, "start", in_copy)

    wg_s[...] = wg_ref[0, 0].astype(BF16)
    wu_s[...] = wu_ref[0, 0].astype(BF16)
    wd_s[...] = wd_ref[0, 0].astype(BF16)

    def compute(slot, tiles):
        los, his = [], []
        for j in range(ROW_TILES):
            words = jnp.concatenate(
                [xbuf[slot, t, :, j, :, :].reshape(ROW_TILE, LANES) for t in tiles], axis=0)
            lo, hi = _unpack_halves(words)
            los.append(lo)
            his.append(hi)
        xb = jnp.concatenate(los + his, axis=-1).astype(BF16)
        g = jnp.dot(xb, wg_s[...], preferred_element_type=F32)
        u = jnp.dot(xb, wu_s[...], preferred_element_type=F32)
        act = (g * (1.0 / (1.0 + jnp.exp(-g))) * u).astype(BF16)
        y = jnp.dot(act, wd_s[...], preferred_element_type=F32)
        packed = _pack_halves(y.astype(BF16).astype(F32))
        for n, t in enumerate(tiles):
            for j in range(ROW_TILES):
                ybuf[slot, t, :, j, :, :] = packed[n * ROW_TILE:(n + 1) * ROW_TILE,
                                                   j * LANES:(j + 1) * LANES].reshape(
                    ROW_GROUPS, SUBLANES, LANES)

    def body(p, carry):
        slot = p % 2
        for_group(p, slot, "wait", in_copy)

        @pl.when(p + 1 < n_groups)
        def _():
            for_group(p + 1, 1 - slot, "start", in_copy)

        @pl.when(p >= 2)
        def _():
            for_group(p - 2, slot, "wait", out_copy)

        compute(slot, (0, 1))
        compute(slot, (2, 3))
        for_group(p, slot, "start", out_copy)
        return carry

    lax.fori_loop(0, n_groups, body, 0)

    @pl.when(n_groups >= 2)
    def _():
        for_group(n_groups - 2, n_groups % 2, "wait", out_copy)

    @pl.when(n_groups >= 1)
    def _():
        for_group(n_groups - 1, (n_groups - 1) % 2, "wait", out_copy)


def _expert_call(layer, tile_blk, e_start, e_count, xs4, wg, wu, wd):
    group_buf = (2, GROUP, ROW_GROUPS, ROW_TILES, SUBLANES, LANES)
    return pl.pallas_call(
        _expert_kernel,
        grid_spec=pltpu.PrefetchScalarGridSpec(
            num_scalar_prefetch=3,
            grid=(N_EXPERTS,),
            in_specs=[
                pl.BlockSpec(memory_space=pl.ANY),
                pl.BlockSpec((1, 1, D_MODEL, D_EXPERT), lambda e, tb, es, ec: (layer, e, 0, 0)),
                pl.BlockSpec((1, 1, D_MODEL, D_EXPERT), lambda e, tb, es, ec: (layer, e, 0, 0)),
                pl.BlockSpec((1, 1, D_EXPERT, D_MODEL), lambda e, tb, es, ec: (layer, e, 0, 0)),
            ],
            out_specs=pl.BlockSpec(memory_space=pl.ANY),
            scratch_shapes=[
                pltpu.VMEM((D_MODEL, D_EXPERT), BF16),
                pltpu.VMEM((D_MODEL, D_EXPERT), BF16),
                pltpu.VMEM((D_EXPERT, D_MODEL), BF16),
                pltpu.VMEM(group_buf, U32),
                pltpu.VMEM(group_buf, U32),
                pltpu.SemaphoreType.DMA((2, GROUP)),
                pltpu.SemaphoreType.DMA((2, GROUP)),
            ],
        ),
        out_shape=jax.ShapeDtypeStruct((TOTAL_ROWS // SUBLANES, ROW_TILES, SUBLANES, LANES), U32),
        input_output_aliases={3: 0},
        compiler_params=pltpu.CompilerParams(
            dimension_semantics=("arbitrary",), vmem_limit_bytes=VMEM_LIMIT),
        name="experts",
    )(tile_blk, e_start, e_count, xs4, wg, wu, wd)


def _combine_kernel(a0_ref, a1_ref, ys_ref, x1_ref, w_ref, ada_ref, fg_ref, out_ref, r0_scr, r1_scr,
                    *, final):
    c = pl.program_id(0)
    s = pl.program_id(1)
    base = c * ROUTE_CHUNK + s * TC

    def body(t8, carry):
        for k in range(SUBLANES):
            t = base + t8 * SUBLANES + k
            dst = pl.ds(t8 * (ROW_TILES * SUBLANES) + k, ROW_TILES, stride=SUBLANES)
            r0_scr[dst, :] = ys_ref[pl.ds(a0_ref[t], ROW_TILES, stride=SUBLANES), :]
            r1_scr[dst, :] = ys_ref[pl.ds(a1_ref[t], ROW_TILES, stride=SUBLANES), :]
        return carry

    lax.fori_loop(0, TC // SUBLANES, body, 0)

    w = w_ref[...]
    w0 = w[:, 0:1]
    w1 = w[:, 1:2]
    g2 = ada_ref[0][5:6]
    n_grp = TC // SUBLANES
    los, his = [], []
    for j in range(ROW_TILES):
        parts0 = [r0_scr[pl.ds((tb * ROW_TILES + j) * SUBLANES, SUBLANES), :] for tb in range(n_grp)]
        parts1 = [r1_scr[pl.ds((tb * ROW_TILES + j) * SUBLANES, SUBLANES), :] for tb in range(n_grp)]
        lo0, hi0 = _unpack_halves(jnp.concatenate(parts0, axis=0))
        lo1, hi1 = _unpack_halves(jnp.concatenate(parts1, axis=0))
        los.append(w0 * lo0 + w1 * lo1)
        his.append(w0 * hi0 + w1 * hi1)
    moe = jnp.concatenate(los + his, axis=-1)
    x2 = x1_ref[...] + g2 * moe
    if final:
        r = lax.rsqrt(jnp.mean(x2 * x2, axis=-1, keepdims=True) + EPS)
        x2 = x2 * r * fg_ref[...]
    out_ref[...] = x2


def _combine_call(a0, a1, ys_2d, x1, wcol, ada_l, final_g, final):
    steps = ROUTE_CHUNK // TC
    tok = lambda c, s, a0, a1: (c * steps + s, 0)
    return pl.pallas_call(
        functools.partial(_combine_kernel, final=final),
        grid_spec=pltpu.PrefetchScalarGridSpec(
            num_scalar_prefetch=2,
            grid=(N_CHUNKS, steps),
            in_specs=[
                pl.BlockSpec((CHUNK_ROWS * ROW_TILES, LANES), lambda c, s, a0, a1: (c, 0)),
                pl.BlockSpec((TC, D_MODEL), tok),
                pl.BlockSpec((TC, 2), tok),
                pl.BlockSpec((1, N_ADA, D_MODEL),
                             lambda c, s, a0, a1: ((c * ROUTE_CHUNK + s * TC) // SEQ, 0, 0)),
                pl.BlockSpec((1, D_MODEL), lambda c, s, a0, a1: (0, 0)),
            ],
            out_specs=pl.BlockSpec((TC, D_MODEL), tok),
            scratch_shapes=[pltpu.VMEM((TC * ROW_TILES, LANES), U32),
                            pltpu.VMEM((TC * ROW_TILES, LANES), U32)],
        ),
        out_shape=jax.ShapeDtypeStruct((N_TOK, D_MODEL), F32),
        compiler_params=pltpu.CompilerParams(
            dimension_semantics=("arbitrary", "arbitrary"), vmem_limit_bytes=VMEM_LIMIT),
        name="combine_final" if final else "combine",
    )(a0, a1, ys_2d, x1, wcol, ada_l, final_g)


def _routing_tables(rt, cnt):
    e0 = rt[0].astype(I32)
    e1 = rt[1].astype(I32)
    rank0 = rt[4].astype(I32)
    rank1 = rt[5].astype(I32)
    counts = cnt[TILES_PER_CHUNK - 1::TILES_PER_CHUNK, :, 0].astype(I32)
    n_tiles = (counts + ROW_TILE - 1) // ROW_TILE
    tile_off = jnp.cumsum(n_tiles, axis=1) - n_tiles
    row_off = tile_off * ROW_TILE
    experts = jnp.arange(N_EXPERTS, dtype=I32)[None, :, None]
    off_tok = row_off[:, :, None]
    pick = lambda e: jnp.sum(
        jnp.where(e.reshape(N_CHUNKS, 1, ROUTE_CHUNK) == experts, off_tok, 0), axis=1).reshape(N_TOK)
    slot0 = pick(e0) + rank0
    slot1 = pick(e1) + rank1
    a0 = _row_addr(slot0)
    a1 = _row_addr(slot1)

    nt_flat = n_tiles.T.reshape(-1)
    ends = jnp.cumsum(nt_flat)
    starts = ends - nt_flat
    total = ends[-1]
    step = jnp.minimum(jnp.arange(MAX_ROW_TILES, dtype=I32), total - 1)
    seg = jnp.sum((ends[None, :] <= step[:, None]).astype(I32), axis=1)
    in_seg = seg[:, None] == jnp.arange(N_EXPERTS * N_CHUNKS, dtype=I32)[None, :]
    k = step - jnp.sum(jnp.where(in_seg, starts[None, :], 0), axis=1)
    c = seg % N_CHUNKS
    tile_off_flat = tile_off.T.reshape(-1)
    tile_blk = c * (CHUNK_ROWS // ROW_TILE) + jnp.sum(jnp.where(in_seg, tile_off_flat[None, :], 0), axis=1) + k
    e_count = jnp.sum(n_tiles, axis=0)
    e_start = jnp.cumsum(e_count) - e_count
    return a0, a1, tile_blk.astype(I32), e_start.astype(I32), e_count.astype(I32)


def _layer_consts(l, norm1_g, norm2_g, w_in, conv_w, conv_b, sp_norm_g, sp_w, sp_b, mix_out_g,
                  w_out, shared):
    pos = jnp.arange(SP_BLOCK)
    mask = (pos[None, :] // CHUNK) <= (pos[:, None] // CHUNK)
    w = jnp.where(mask[None], sp_w[l], 0.0)
    w_pairs = jnp.concatenate([w[0::2], w[1::2]], axis=-1).astype(BF16)
    sp_bias = jnp.repeat(sp_b[l].T, HEAD_DIM, axis=1)
    return (
        norm1_g[l].reshape(1, D_MODEL), norm2_g[l].reshape(1, D_MODEL),
        w_in[l].astype(BF16),
        conv_w[l], conv_b[l].reshape(1, D_CONV), sp_norm_g[l].reshape(1, D_SPATIAL),
        w_pairs, sp_bias,
        mix_out_g[l].reshape(1, D_MODEL),
        w_out[l].astype(BF16),
    ) + shared


def kernel(x, c, w_ada, b_ada, norm1_g, norm2_g, w_in, conv_w, conv_b, sp_norm_g, sp_w, sp_b, mix_out_g, w_out, router_w, router_b, exp_w_gate, exp_w_up, exp_w_down, final_g):
    ada = _ada_table(c, w_ada, b_ada).reshape(DEPTH, BATCH, N_ADA, D_MODEL)

    rw = jnp.pad(router_w, ((0, 0), (0, LANES - N_EXPERTS)))
    rw_hi = rw.astype(BF16)
    rw_lo = (rw - rw_hi.astype(F32)).astype(BF16)
    head_of = jnp.arange(D_MODEL) // HEAD_DIM
    hred = (head_of[:, None] == jnp.arange(LANES)[None, :]).astype(F32) / HEAD_DIM
    hexp_half = (jnp.arange(LANES)[:, None] == head_of[None, :]).astype(F32)
    hexp = jnp.concatenate([hexp_half, hexp_half], axis=0)
    tri = (jnp.arange(SB)[:, None] < jnp.arange(SB)[None, :])
    shared = (rw_hi, rw_lo, router_b.reshape(N_EXPERTS, 1), hred.astype(BF16), hexp.astype(BF16),
              tri.astype(BF16))
    fg = final_g.reshape(1, D_MODEL)

    xc = x.reshape(N_TOK, D_MODEL)
    for l in range(DEPTH):
        consts = _layer_consts(l, norm1_g, norm2_g, w_in, conv_w, conv_b, sp_norm_g, sp_w, sp_b,
                               mix_out_g, w_out, shared)
        x1, h2p, rt, cnt = _mix_call(xc, ada[l], consts)
        a0, a1, tile_blk, e_start, e_count = _routing_tables(rt, cnt)
        xs = _dispatch_call(a0, a1, h2p.reshape(N_TOK * ROW_TILES, LANES))
        ys = _expert_call(l, tile_blk, e_start, e_count,
                          xs.reshape(TOTAL_ROWS // SUBLANES, ROW_TILES, SUBLANES, LANES),
                          exp_w_gate, exp_w_up, exp_w_down)
        wcol = rt[2:4].T
        xc = _combine_call(a0, a1, ys.reshape(TOTAL_ROWS * ROW_TILES, LANES), x1, wcol, ada[l], fg,
                           final=(l == DEPTH - 1))
    return xc.reshape(BATCH, SEQ, D_MODEL)
```

```python
import functools

import jax
import jax.numpy as jnp
from jax import lax
from jax.experimental import pallas as pl
from jax.experimental.pallas import tpu as pltpu

F32 = jnp.float32
BF16 = jnp.bfloat16
U32 = jnp.uint32
I32 = jnp.int32

D_MODEL = 1024
BATCH = 8
SEQ = 2048
DEPTH = 4
N_TOK = BATCH * SEQ
CHUNK = 64
HEAD_DIM = 64
D_CONV = 512
D_SPATIAL = 512
N_SP_HEADS = 8
N_MIX_HEADS = 16
SP_BLOCK = 128
N_EXPERTS = 16
N_GROUPS = 4
PER_GROUP = 4
D_EXPERT = 512
N_ADA = 6
EPS = 1e-6

LANES = 128
SUBLANES = 8
HALF = D_MODEL // 2
ROW_TILES = HALF // LANES

TM = 512
SB = TM // 2
TILES_PER_SEQ = SEQ // TM
ROUTE_CHUNK = 4096
N_CHUNKS = N_TOK // ROUTE_CHUNK
TILES_PER_CHUNK = ROUTE_CHUNK // TM
ROW_TILE = 128
ROW_GROUPS = ROW_TILE // SUBLANES
GROUP = 4
CHUNK_ROWS = 2 * ROUTE_CHUNK + N_EXPERTS * ROW_TILE
TOTAL_ROWS = N_CHUNKS * CHUNK_ROWS
MAX_ROW_TILES = TOTAL_ROWS // ROW_TILE
MAX_GROUPS = MAX_ROW_TILES // GROUP + N_EXPERTS
TD = 1024
TC = 512
VMEM_LIMIT = 58 * 1024 * 1024


def _gelu_tanh(x):
    c = 0.7978845608028654
    return 0.5 * x * (1.0 + jnp.tanh(c * (x + 0.044715 * (x * x * x))))


def _split_bf16(x):
    hi = x.astype(BF16)
    lo = (x - hi.astype(F32)).astype(BF16)
    return hi, lo


def _pack_halves(x):
    return pltpu.pack_elementwise([x[:, :HALF], x[:, HALF:]], packed_dtype=BF16)


def _unpack_halves(words):
    lo = pltpu.unpack_elementwise(words, index=0, packed_dtype=BF16, unpacked_dtype=F32)
    hi = pltpu.unpack_elementwise(words, index=1, packed_dtype=BF16, unpacked_dtype=F32)
    return lo, hi


def _ada_kernel(c_ref, w_ref, b_ref, o_ref):
    c = c_ref[...]
    ca = c * (1.0 / (1.0 + jnp.exp(-c)))
    c_hi, c_lo = _split_bf16(ca)
    w_hi, w_lo = _split_bf16(w_ref[0])
    acc = jnp.dot(c_hi, w_hi, preferred_element_type=F32)
    acc += jnp.dot(c_lo, w_hi, preferred_element_type=F32)
    acc += jnp.dot(c_hi, w_lo, preferred_element_type=F32)
    o_ref[0] = acc + b_ref[0]


def _ada_table(c, w_ada, b_ada):
    n_blk = N_ADA
    return pl.pallas_call(
        _ada_kernel,
        grid=(DEPTH, n_blk),
        in_specs=[
            pl.BlockSpec((BATCH, D_MODEL), lambda l, n: (0, 0)),
            pl.BlockSpec((1, D_MODEL, D_MODEL), lambda l, n: (l, 0, n)),
            pl.BlockSpec((1, 1, D_MODEL), lambda l, n: (l, 0, n)),
        ],
        out_specs=pl.BlockSpec((1, BATCH, D_MODEL), lambda l, n: (l, 0, n)),
        out_shape=jax.ShapeDtypeStruct((DEPTH, BATCH, N_ADA * D_MODEL), F32),
        name="ada_table",
    )(c, w_ada, b_ada.reshape(DEPTH, 1, N_ADA * D_MODEL))


def _mix_kernel(x_ref, ada_ref, n1g_ref, n2g_ref, win_ref, cw_ref, cb_ref, spg_ref, spw_ref,
                spbias_ref, mog_ref, wout_ref, rwh_ref, rwl_ref, rb_ref, hred_ref, hexp_ref,
                tri_ref, x1_ref, h2p_ref, rt_ref, cnt_ref, z_scr, cnt_scr):
    i = pl.program_id(0)

    @pl.when(i % TILES_PER_SEQ == 0)
    def _():
        z_scr[pl.ds(0, SUBLANES), :] = jnp.zeros((SUBLANES, D_CONV), F32)

    @pl.when(i % TILES_PER_CHUNK == 0)
    def _():
        cnt_scr[...] = jnp.zeros((N_EXPERTS, LANES), F32)

    chains = [
        _mix_rows(sb * SB, x_ref, ada_ref, n1g_ref, n2g_ref, win_ref, cw_ref, cb_ref, spg_ref,
                  spw_ref, spbias_ref, mog_ref, wout_ref, rwh_ref, rwl_ref, rb_ref, hred_ref,
                  hexp_ref, tri_ref, x1_ref, h2p_ref, rt_ref, z_scr, cnt_scr)
        for sb in range(TM // SB)]
    for c in MIX_STAGE_ORDER:
        next(chains[c])
    z_scr[pl.ds(0, SUBLANES), :] = z_scr[pl.ds(TM, SUBLANES), :]
    cnt_ref[0] = cnt_scr[...]


MIX_STAGE_ORDER = (0, 0, 1, 0, 1, 0, 1, 0, 0, 1, 1, 0, 0, 1, 1, 1)


def _mix_rows(r0, x_ref, ada_ref, n1g_ref, n2g_ref, win_ref, cw_ref, cb_ref, spg_ref, spw_ref,
              spbias_ref, mog_ref, wout_ref, rwh_ref, rwl_ref, rb_ref, hred_ref, hexp_ref,
              tri_ref, x1_ref, h2p_ref, rt_ref, z_scr, cnt_scr):
    x = x_ref[pl.ds(r0, SB), :]
    ada = ada_ref[0]
    sh1, sc1, g1 = ada[0:1], ada[1:2], ada[2:3]
    sh2, sc2 = ada[3:4], ada[4:5]

    rinv = lax.rsqrt(jnp.mean(x * x, axis=-1, keepdims=True) + EPS)
    h = (x * rinv * n1g_ref[...]) * (1.0 + sc1) + sh1
    hb = h.astype(BF16)
    yield

    def proj(k):
        return jnp.dot(hb, win_ref[:, k * 512:(k + 1) * 512], preferred_element_type=F32)

    z = proj(1) * proj(2)
    z_scr[pl.ds(SUBLANES + r0, SB), :] = z
    z1 = z_scr[pl.ds(SUBLANES + r0 - 1, SB), :]
    z2 = z_scr[pl.ds(SUBLANES + r0 - 2, SB), :]
    yield
    p_b, p_u, p_v = proj(0), proj(3), proj(4)
    yield
    cw = cw_ref[...]
    acc = cb_ref[...] + cw[0:1] * z2
    acc = acc + cw[1:2] * z1
    acc = acc + cw[2:3] * z
    y_conv = p_b * acc

    ug = _gelu_tanh(p_u)
    vg = _gelu_tanh(p_v)
    vr = lax.rsqrt(jnp.mean(vg * vg, axis=-1, keepdims=True) + EPS)
    vn = (vg * vr * spg_ref[...]).astype(BF16)
    lane = lax.broadcasted_iota(I32, (1, LANES), 1)
    first_head = lane < HEAD_DIM
    zero = jnp.zeros((), BF16)
    s_rows = []
    for blk in range(SB // SP_BLOCK):
        cols = []
        for p in range(N_SP_HEADS // 2):
            vv = vn[blk * SP_BLOCK:(blk + 1) * SP_BLOCK, p * LANES:(p + 1) * LANES]
            rhs = jnp.concatenate([jnp.where(first_head, vv, zero), jnp.where(first_head, zero, vv)], axis=0)
            cols.append(jnp.dot(spw_ref[p], rhs, preferred_element_type=F32))
        s_rows.append(jnp.concatenate(cols, axis=-1) + spbias_ref[...])
    y_sp = ug * jnp.concatenate(s_rows, axis=0)
    yield

    y = jnp.concatenate([y_conv, y_sp], axis=-1)
    ms = jnp.dot((y * y).astype(BF16), hred_ref[...], preferred_element_type=F32)
    r_hi, r_lo = _split_bf16(lax.rsqrt(ms + EPS))
    scale = jnp.dot(jnp.concatenate([r_hi, r_lo], axis=-1), hexp_ref[...], preferred_element_type=F32)
    yn = (y * scale * mog_ref[...]).astype(BF16)
    yield
    x1 = x + g1 * jnp.dot(yn, wout_ref[...], preferred_element_type=F32)
    x1_ref[pl.ds(r0, SB), :] = x1
    yield

    r2 = lax.rsqrt(jnp.mean(x1 * x1, axis=-1, keepdims=True) + EPS)
    h2 = (x1 * r2 * n2g_ref[...]) * (1.0 + sc2) + sh2
    h2_hi, h2_lo = _split_bf16(h2)
    packed = _pack_halves(h2_hi.astype(F32))
    for j in range(ROW_TILES):
        h2p_ref[pl.ds(r0 // SUBLANES, SB // SUBLANES), j, :, :] = (
            packed[:, j * LANES:(j + 1) * LANES].reshape(SB // SUBLANES, SUBLANES, LANES))
    yield

    logits = jnp.dot(h2_hi, rwh_ref[...], preferred_element_type=F32)
    logits += jnp.dot(h2_lo, rwh_ref[...], preferred_element_type=F32)
    logits += jnp.dot(h2_hi, rwl_ref[...], preferred_element_type=F32)
    lg = logits.T[0:N_EXPERTS, :]
    scores = 1.0 / (1.0 + jnp.exp(-lg))
    sel = scores + rb_ref[...]

    gs = []
    for g in range(N_GROUPS):
        a, b, c, d = (sel[g * PER_GROUP + k:g * PER_GROUP + k + 1, :] for k in range(PER_GROUP))
        hi1, lo1 = jnp.maximum(a, b), jnp.minimum(a, b)
        hi2, lo2 = jnp.maximum(c, d), jnp.minimum(c, d)
        gs.append(jnp.maximum(hi1, hi2) + jnp.maximum(jnp.minimum(hi1, hi2), jnp.maximum(lo1, lo2)))
    best = gs[0]
    top_group = jnp.zeros((1, SB), I32)
    for g in range(1, N_GROUPS):
        better = gs[g] > best
        best = jnp.where(better, gs[g], best)
        top_group = jnp.where(better, g, top_group)

    eid_i = lax.broadcasted_iota(I32, (N_EXPERTS, SB), 0)
    eid = eid_i.astype(F32)
    neg_inf = jnp.float32(-jnp.inf)
    masked = jnp.where((eid_i // PER_GROUP) == top_group, sel, neg_inf)
    m1 = jnp.max(masked, axis=0, keepdims=True)
    i1 = jnp.min(jnp.where(masked == m1, eid, float(N_EXPERTS)), axis=0, keepdims=True)
    masked2 = jnp.where(eid == i1, neg_inf, masked)
    m2 = jnp.max(masked2, axis=0, keepdims=True)
    i2 = jnp.min(jnp.where(masked2 == m2, eid, float(N_EXPERTS)), axis=0, keepdims=True)
    pick1 = eid == i1
    pick2 = eid == i2
    w1 = jnp.sum(jnp.where(pick1, scores, 0.0), axis=0, keepdims=True)
    w2 = jnp.sum(jnp.where(pick2, scores, 0.0), axis=0, keepdims=True)
    wsum = w1 + w2
    c1 = w1 / wsum
    c2 = w2 / wsum

    assigned = jnp.where(pick1 | pick2, 1.0, 0.0)
    before = jnp.dot(assigned.astype(BF16), tri_ref[...], preferred_element_type=F32)
    carry = cnt_scr[...]
    rank = before + carry[:, 0:1]
    rank1 = jnp.sum(jnp.where(pick1, rank, 0.0), axis=0, keepdims=True)
    rank2 = jnp.sum(jnp.where(pick2, rank, 0.0), axis=0, keepdims=True)
    cnt_scr[...] = carry + jnp.sum(assigned, axis=1, keepdims=True)
    unused = jnp.zeros((1, SB), F32)
    for r, row in enumerate((i1, i2, c1, c2, rank1, rank2, unused, unused)):
        rt_ref[r:r + 1, pl.ds(r0, SB)] = row
    yield


def _mix_call(x2d, ada_l, consts):
    n_tiles = N_TOK // TM
    full = lambda shape: pl.BlockSpec(shape, lambda i: (0,) * len(shape))
    in_specs = [
        pl.BlockSpec((TM, D_MODEL), lambda i: (i, 0)),
        pl.BlockSpec((1, N_ADA, D_MODEL), lambda i: (i // TILES_PER_SEQ, 0, 0)),
        full((1, D_MODEL)), full((1, D_MODEL)),
        full((D_MODEL, 5 * 512)),
        full((3, D_CONV)), full((1, D_CONV)), full((1, D_SPATIAL)),
        full((N_SP_HEADS // 2, SP_BLOCK, 2 * SP_BLOCK)),
        full((SP_BLOCK, D_SPATIAL)),
        full((1, D_MODEL)),
        full((D_MODEL, D_MODEL)),
        full((D_MODEL, LANES)), full((D_MODEL, LANES)), full((N_EXPERTS, 1)),
        full((D_MODEL, LANES)), full((2 * LANES, D_MODEL)),
        full((SB, SB)),
    ]
    out_specs = [
        pl.BlockSpec((TM, D_MODEL), lambda i: (i, 0)),
        pl.BlockSpec((TM // SUBLANES, ROW_TILES, SUBLANES, LANES), lambda i: (i, 0, 0, 0)),
        pl.BlockSpec((SUBLANES, TM), lambda i: (0, i)),
        pl.BlockSpec((1, N_EXPERTS, LANES), lambda i: (i, 0, 0)),
    ]
    out_shape = [
        jax.ShapeDtypeStruct((N_TOK, D_MODEL), F32),
        jax.ShapeDtypeStruct((N_TOK // SUBLANES, ROW_TILES, SUBLANES, LANES), U32),
        jax.ShapeDtypeStruct((SUBLANES, N_TOK), F32),
        jax.ShapeDtypeStruct((n_tiles, N_EXPERTS, LANES), F32),
    ]
    return pl.pallas_call(
        _mix_kernel,
        grid=(n_tiles,),
        in_specs=in_specs,
        out_specs=out_specs,
        out_shape=out_shape,
        scratch_shapes=[pltpu.VMEM((TM + SUBLANES, D_CONV), F32), pltpu.VMEM((N_EXPERTS, LANES), F32)],
        compiler_params=pltpu.CompilerParams(
            dimension_semantics=("arbitrary",), vmem_limit_bytes=VMEM_LIMIT),
        name="mix",
    )(x2d, ada_l, *consts)


def _row_addr(t):
    return (t // SUBLANES) * (ROW_TILES * SUBLANES) + t % SUBLANES


def _dispatch_kernel(a0_ref, a1_ref, h2p_ref, xs_ref):
    c = pl.program_id(0)
    s = pl.program_id(1)

    @pl.when(s == 0)
    def _():
        xs_ref[...] = jnp.zeros(xs_ref.shape, U32)

    base = c * ROUTE_CHUNK + s * TD

    def body(t8, carry):
        for k in range(SUBLANES):
            row = h2p_ref[pl.ds(t8 * (ROW_TILES * SUBLANES) + k, ROW_TILES, stride=SUBLANES), :]
            t = base + t8 * SUBLANES + k
            xs_ref[pl.ds(a0_ref[t], ROW_TILES, stride=SUBLANES), :] = row
            xs_ref[pl.ds(a1_ref[t], ROW_TILES, stride=SUBLANES), :] = row
        return carry

    lax.fori_loop(0, TD // SUBLANES, body, 0)


def _dispatch_call(a0, a1, h2p_2d):
    steps = ROUTE_CHUNK // TD
    return pl.pallas_call(
        _dispatch_kernel,
        grid_spec=pltpu.PrefetchScalarGridSpec(
            num_scalar_prefetch=2,
            grid=(N_CHUNKS, steps),
            in_specs=[pl.BlockSpec((TD * ROW_TILES, LANES), lambda c, s, a0, a1: (c * steps + s, 0))],
            out_specs=pl.BlockSpec((CHUNK_ROWS * ROW_TILES, LANES), lambda c, s, a0, a1: (c, 0)),
        ),
        out_shape=jax.ShapeDtypeStruct((TOTAL_ROWS * ROW_TILES, LANES), U32),
        compiler_params=pltpu.CompilerParams(
            dimension_semantics=("arbitrary", "arbitrary"), vmem_limit_bytes=VMEM_LIMIT),
        name="dispatch",
    )(a0, a1, h2p_2d)


def _expert_kernel(tb_ref, gst_ref, xs_hbm, wg_ref, wu_ref, wd_ref, ys_hbm,
                   wg_s, wu_s, wd_s, xbuf, ybuf, sem_in, sem_out):
    e = pl.program_id(0)
    g_first = gst_ref[e]
    g_end = gst_ref[e + 1]
    g_all = gst_ref[N_EXPERTS]

    def in_copy(q, t):
        rows = pl.ds(tb_ref[GROUP * q + t] * ROW_GROUPS, ROW_GROUPS)
        return pltpu.make_async_copy(xs_hbm.at[rows], xbuf.at[q % 2, t], sem_in.at[q % 2, t])

    def out_copy(q, t):
        rows = pl.ds(tb_ref[GROUP * q + t] * ROW_GROUPS, ROW_GROUPS)
        return pltpu.make_async_copy(ybuf.at[q % 2, t], ys_hbm.at[rows], sem_out.at[q % 2, t])

    def for_group(q, action, make):
        for t in range(GROUP):
            @pl.when(tb_ref[GROUP * q + t] >= 0)
            def _():
                if action == "start":
                    make(q, t).start(priority=1)
                else:
                    make(q, t).wait()

    @pl.when(e == 0)
    def _():
        xbuf[...] = jnp.zeros(xbuf.shape, U32)

        @pl.when(g_all > 0)
        def _():
            for_group(0, "start", in_copy)

    wg_s[...] = wg_ref[0, 0].astype(BF16)
    wu_s[...] = wu_ref[0, 0].astype(BF16)
    wd_s[...] = wd_ref[0, 0].astype(BF16)

    def compute(slot, tiles):
        los, his = [], []
        for j in range(ROW_TILES):
            words = jnp.concatenate(
                [xbuf[slot, t, :, j, :, :].reshape(ROW_TILE, LANES) for t in tiles], axis=0)
            lo, hi = _unpack_halves(words)
            los.append(lo)
            his.append(hi)
        xb = jnp.concatenate(los + his, axis=-1).astype(BF16)
        g = jnp.dot(xb, wg_s[...], preferred_element_type=F32)
        u = jnp.dot(xb, wu_s[...], preferred_element_type=F32)
        act = (g * (1.0 / (1.0 + jnp.exp(-g))) * u).astype(BF16)
        y = jnp.dot(act, wd_s[...], preferred_element_type=F32)
        packed = _pack_halves(y.astype(BF16).astype(F32))
        for n, t in enumerate(tiles):
            for j in range(ROW_TILES):
                ybuf[slot, t, :, j, :, :] = packed[n * ROW_TILE:(n + 1) * ROW_TILE,
                                                   j * LANES:(j + 1) * LANES].reshape(
                    ROW_GROUPS, SUBLANES, LANES)

    def body(q, carry):
        slot = q % 2
        for_group(q, "wait", in_copy)

        @pl.when(q + 1 < g_all)
        def _():
            for_group(q + 1, "start", in_copy)

        @pl.when(q >= 2)
        def _():
            for_group(q - 2, "wait", out_copy)

        compute(slot, (0, 1))
        compute(slot, (2, 3))
        for_group(q, "start", out_copy)
        return carry

    lax.fori_loop(g_first, g_end, body, 0)

    @pl.when(e == N_EXPERTS - 1)
    def _():
        @pl.when(g_all >= 2)
        def _():
            for_group(g_all - 2, "wait", out_copy)

        @pl.when(g_all >= 1)
        def _():
            for_group(g_all - 1, "wait", out_copy)


def _expert_call(layer, group_tiles, group_start, xs4, wg, wu, wd):
    group_buf = (2, GROUP, ROW_GROUPS, ROW_TILES, SUBLANES, LANES)
    return pl.pallas_call(
        _expert_kernel,
        grid_spec=pltpu.PrefetchScalarGridSpec(
            num_scalar_prefetch=2,
            grid=(N_EXPERTS,),
            in_specs=[
                pl.BlockSpec(memory_space=pl.ANY),
                pl.BlockSpec((1, 1, D_MODEL, D_EXPERT), lambda e, tb, gs: (layer, e, 0, 0)),
                pl.BlockSpec((1, 1, D_MODEL, D_EXPERT), lambda e, tb, gs: (layer, e, 0, 0)),
                pl.BlockSpec((1, 1, D_EXPERT, D_MODEL), lambda e, tb, gs: (layer, e, 0, 0)),
            ],
            out_specs=pl.BlockSpec(memory_space=pl.ANY),
            scratch_shapes=[
                pltpu.VMEM((D_MODEL, D_EXPERT), BF16),
                pltpu.VMEM((D_MODEL, D_EXPERT), BF16),
                pltpu.VMEM((D_EXPERT, D_MODEL), BF16),
                pltpu.VMEM(group_buf, U32),
                pltpu.VMEM(group_buf, U32),
                pltpu.SemaphoreType.DMA((2, GROUP)),
                pltpu.SemaphoreType.DMA((2, GROUP)),
            ],
        ),
        out_shape=jax.ShapeDtypeStruct((TOTAL_ROWS // SUBLANES, ROW_TILES, SUBLANES, LANES), U32),
        input_output_aliases={2: 0},
        compiler_params=pltpu.CompilerParams(
            dimension_semantics=("arbitrary",), vmem_limit_bytes=VMEM_LIMIT),
        name="experts",
    )(group_tiles, group_start, xs4, wg, wu, wd)


def _combine_kernel(a0_ref, a1_ref, ys_ref, x1_ref, w_ref, ada_ref, fg_ref, out_ref, r0_scr, r1_scr,
                    *, final):
    c = pl.program_id(0)
    s = pl.program_id(1)
    base = c * ROUTE_CHUNK + s * TC

    def body(t8, carry):
        for k in range(SUBLANES):
            t = base + t8 * SUBLANES + k
            dst = pl.ds(t8 * (ROW_TILES * SUBLANES) + k, ROW_TILES, stride=SUBLANES)
            r0_scr[dst, :] = ys_ref[pl.ds(a0_ref[t], ROW_TILES, stride=SUBLANES), :]
            r1_scr[dst, :] = ys_ref[pl.ds(a1_ref[t], ROW_TILES, stride=SUBLANES), :]
        return carry

    lax.fori_loop(0, TC // SUBLANES, body, 0)

    w = w_ref[...]
    w0 = w[:, 0:1]
    w1 = w[:, 1:2]
    g2 = ada_ref[0][5:6]
    n_grp = TC // SUBLANES
    los, his = [], []
    for j in range(ROW_TILES):
        parts0 = [r0_scr[pl.ds((tb * ROW_TILES + j) * SUBLANES, SUBLANES), :] for tb in range(n_grp)]
        parts1 = [r1_scr[pl.ds((tb * ROW_TILES + j) * SUBLANES, SUBLANES), :] for tb in range(n_grp)]
        lo0, hi0 = _unpack_halves(jnp.concatenate(parts0, axis=0))
        lo1, hi1 = _unpack_halves(jnp.concatenate(parts1, axis=0))
        los.append(w0 * lo0 + w1 * lo1)
        his.append(w0 * hi0 + w1 * hi1)
    moe = jnp.concatenate(los + his, axis=-1)
    x2 = x1_ref[...] + g2 * moe
    if final:
        r = lax.rsqrt(jnp.mean(x2 * x2, axis=-1, keepdims=True) + EPS)
        x2 = x2 * r * fg_ref[...]
    out_ref[...] = x2


def _combine_call(a0, a1, ys_2d, x1, wcol, ada_l, final_g, final):
    steps = ROUTE_CHUNK // TC
    tok = lambda c, s, a0, a1: (c * steps + s, 0)
    return pl.pallas_call(
        functools.partial(_combine_kernel, final=final),
        grid_spec=pltpu.PrefetchScalarGridSpec(
            num_scalar_prefetch=2,
            grid=(N_CHUNKS, steps),
            in_specs=[
                pl.BlockSpec((CHUNK_ROWS * ROW_TILES, LANES), lambda c, s, a0, a1: (c, 0)),
                pl.BlockSpec((TC, D_MODEL), tok),
                pl.BlockSpec((TC, 2), tok),
                pl.BlockSpec((1, N_ADA, D_MODEL),
                             lambda c, s, a0, a1: ((c * ROUTE_CHUNK + s * TC) // SEQ, 0, 0)),
                pl.BlockSpec((1, D_MODEL), lambda c, s, a0, a1: (0, 0)),
            ],
            out_specs=pl.BlockSpec((TC, D_MODEL), tok),
            scratch_shapes=[pltpu.VMEM((TC * ROW_TILES, LANES), U32),
                            pltpu.VMEM((TC * ROW_TILES, LANES), U32)],
        ),
        out_shape=jax.ShapeDtypeStruct((N_TOK, D_MODEL), F32),
        compiler_params=pltpu.CompilerParams(
            dimension_semantics=("arbitrary", "arbitrary"), vmem_limit_bytes=VMEM_LIMIT),
        name="combine_final" if final else "combine",
    )(a0, a1, ys_2d, x1, wcol, ada_l, final_g)


def _routing_tables(rt, cnt):
    e0 = rt[0].astype(I32)
    e1 = rt[1].astype(I32)
    rank0 = rt[4].astype(I32)
    rank1 = rt[5].astype(I32)
    counts = cnt[TILES_PER_CHUNK - 1::TILES_PER_CHUNK, :, 0].astype(I32)
    n_tiles = (counts + ROW_TILE - 1) // ROW_TILE
    tile_off = jnp.cumsum(n_tiles, axis=1) - n_tiles
    row_off = tile_off * ROW_TILE
    experts = jnp.arange(N_EXPERTS, dtype=I32)[None, :, None]
    off_tok = row_off[:, :, None]
    pick = lambda e: jnp.sum(
        jnp.where(e.reshape(N_CHUNKS, 1, ROUTE_CHUNK) == experts, off_tok, 0), axis=1).reshape(N_TOK)
    slot0 = pick(e0) + rank0
    slot1 = pick(e1) + rank1
    a0 = _row_addr(slot0)
    a1 = _row_addr(slot1)

    nt_flat = n_tiles.T.reshape(-1)
    ends = jnp.cumsum(nt_flat)
    starts = ends - nt_flat
    total = ends[-1]
    step = jnp.minimum(jnp.arange(MAX_ROW_TILES, dtype=I32), total - 1)
    seg = jnp.sum((ends[None, :] <= step[:, None]).astype(I32), axis=1)
    in_seg = seg[:, None] == jnp.arange(N_EXPERTS * N_CHUNKS, dtype=I32)[None, :]
    k = step - jnp.sum(jnp.where(in_seg, starts[None, :], 0), axis=1)
    c = seg % N_CHUNKS
    tile_off_flat = tile_off.T.reshape(-1)
    tile_blk = c * (CHUNK_ROWS // ROW_TILE) + jnp.sum(jnp.where(in_seg, tile_off_flat[None, :], 0), axis=1) + k
    e_count = jnp.sum(n_tiles, axis=0)
    e_start = jnp.cumsum(e_count) - e_count
    g_count = (e_count + GROUP - 1) // GROUP
    g_end = jnp.cumsum(g_count)
    g_start = g_end - g_count
    slot_id = jnp.arange(MAX_GROUPS * GROUP, dtype=I32)
    q = slot_id // GROUP
    owner = jnp.sum((g_end[None, :] <= q[:, None]).astype(I32), axis=1)
    is_owner = owner[:, None] == jnp.arange(N_EXPERTS, dtype=I32)[None, :]
    sel = lambda v: jnp.sum(jnp.where(is_owner, v[None, :], 0), axis=1)
    k = (q - sel(g_start)) * GROUP + slot_id % GROUP
    valid = (owner < N_EXPERTS) & (k < sel(e_count))
    flat = sel(e_start) + k
    pick_tile = flat[:, None] == jnp.arange(MAX_ROW_TILES, dtype=I32)[None, :]
    group_tiles = jnp.where(valid, jnp.sum(jnp.where(pick_tile, tile_blk[None, :], 0), axis=1), -1)
    group_start = jnp.concatenate([g_start, g_end[-1:]])
    return a0, a1, group_tiles.astype(I32), group_start.astype(I32)


def _layer_consts(l, norm1_g, norm2_g, w_in, conv_w, conv_b, sp_norm_g, sp_w, sp_b, mix_out_g,
                  w_out, shared):
    pos = jnp.arange(SP_BLOCK)
    mask = (pos[None, :] // CHUNK) <= (pos[:, None] // CHUNK)
    w = jnp.where(mask[None], sp_w[l], 0.0)
    w_pairs = jnp.concatenate([w[0::2], w[1::2]], axis=-1).astype(BF16)
    sp_bias = jnp.repeat(sp_b[l].T, HEAD_DIM, axis=1)
    return (
        norm1_g[l].reshape(1, D_MODEL), norm2_g[l].reshape(1, D_MODEL),
        w_in[l].astype(BF16),
        conv_w[l], conv_b[l].reshape(1, D_CONV), sp_norm_g[l].reshape(1, D_SPATIAL),
        w_pairs, sp_bias,
        mix_out_g[l].reshape(1, D_MODEL),
        w_out[l].astype(BF16),
    ) + shared


def kernel(x, c, w_ada, b_ada, norm1_g, norm2_g, w_in, conv_w, conv_b, sp_norm_g, sp_w, sp_b, mix_out_g, w_out, router_w, router_b, exp_w_gate, exp_w_up, exp_w_down, final_g):
    ada = _ada_table(c, w_ada, b_ada).reshape(DEPTH, BATCH, N_ADA, D_MODEL)

    rw = jnp.pad(router_w, ((0, 0), (0, LANES - N_EXPERTS)))
    rw_hi = rw.astype(BF16)
    rw_lo = (rw - rw_hi.astype(F32)).astype(BF16)
    head_of = jnp.arange(D_MODEL) // HEAD_DIM
    hred = (head_of[:, None] == jnp.arange(LANES)[None, :]).astype(F32) / HEAD_DIM
    hexp_half = (jnp.arange(LANES)[:, None] == head_of[None, :]).astype(F32)
    hexp = jnp.concatenate([hexp_half, hexp_half], axis=0)
    tri = (jnp.arange(SB)[:, None] < jnp.arange(SB)[None, :])
    shared = (rw_hi, rw_lo, router_b.reshape(N_EXPERTS, 1), hred.astype(BF16), hexp.astype(BF16),
              tri.astype(BF16))
    fg = final_g.reshape(1, D_MODEL)

    xc = x.reshape(N_TOK, D_MODEL)
    for l in range(DEPTH):
        consts = _layer_consts(l, norm1_g, norm2_g, w_in, conv_w, conv_b, sp_norm_g, sp_w, sp_b,
                               mix_out_g, w_out, shared)
        x1, h2p, rt, cnt = _mix_call(xc, ada[l], consts)
        a0, a1, group_tiles, group_start = _routing_tables(rt, cnt)
        xs = _dispatch_call(a0, a1, h2p.reshape(N_TOK * ROW_TILES, LANES))
        ys = _expert_call(l, group_tiles, group_start,
                          xs.reshape(TOTAL_ROWS // SUBLANES, ROW_TILES, SUBLANES, LANES),
                          exp_w_gate, exp_w_up, exp_w_down)
        wcol = rt[2:4].T
        xc = _combine_call(a0, a1, ys.reshape(TOTAL_ROWS * ROW_TILES, LANES), x1, wcol, ada[l], fg,
                           final=(l == DEPTH - 1))
    return xc.reshape(BATCH, SEQ, D_MODEL)
```

```python
import functools

import jax
import jax.numpy as jnp
from jax import lax
from jax.experimental import pallas as pl
from jax.experimental.pallas import tpu as pltpu

F32 = jnp.float32
BF16 = jnp.bfloat16
U32 = jnp.uint32
I32 = jnp.int32

D_MODEL = 1024
BATCH = 8
SEQ = 2048
DEPTH = 4
N_TOK = BATCH * SEQ
CHUNK = 64
HEAD_DIM = 64
D_CONV = 512
D_SPATIAL = 512
N_SP_HEADS = 8
N_MIX_HEADS = 16
SP_BLOCK = 128
N_EXPERTS = 16
N_GROUPS = 4
PER_GROUP = 4
D_EXPERT = 512
N_ADA = 6
EPS = 1e-6

LANES = 128
SUBLANES = 8
HALF = D_MODEL // 2
ROW_TILES = HALF // LANES

TM = 1024
SB = 256
TILES_PER_SEQ = SEQ // TM
ROUTE_CHUNK = 4096
N_CHUNKS = N_TOK // ROUTE_CHUNK
TILES_PER_CHUNK = ROUTE_CHUNK // TM
ROW_TILE = 128
ROW_GROUPS = ROW_TILE // SUBLANES
GROUP = 4
CHUNK_ROWS = 2 * ROUTE_CHUNK + N_EXPERTS * ROW_TILE
TOTAL_ROWS = N_CHUNKS * CHUNK_ROWS
MAX_ROW_TILES = TOTAL_ROWS // ROW_TILE
MAX_GROUPS = MAX_ROW_TILES // GROUP + N_EXPERTS
LOOP_TOKENS = 16
TD = 1024
TC = 512
VMEM_LIMIT = 58 * 1024 * 1024


def _gelu_tanh(x):
    c = 0.7978845608028654
    return 0.5 * x * (1.0 + jnp.tanh(c * (x + 0.044715 * (x * x * x))))


def _split_bf16(x):
    hi = x.astype(BF16)
    lo = (x - hi.astype(F32)).astype(BF16)
    return hi, lo


def _pack_halves(x):
    return pltpu.pack_elementwise([x[:, :HALF], x[:, HALF:]], packed_dtype=BF16)


def _unpack_halves(words):
    lo = pltpu.unpack_elementwise(words, index=0, packed_dtype=BF16, unpacked_dtype=F32)
    hi = pltpu.unpack_elementwise(words, index=1, packed_dtype=BF16, unpacked_dtype=F32)
    return lo, hi


def _ada_kernel(c_ref, w_ref, b_ref, o_ref):
    c = c_ref[...]
    ca = c * (1.0 / (1.0 + jnp.exp(-c)))
    c_hi, c_lo = _split_bf16(ca)
    w_hi, w_lo = _split_bf16(w_ref[0])
    acc = jnp.dot(c_hi, w_hi, preferred_element_type=F32)
    acc += jnp.dot(c_lo, w_hi, preferred_element_type=F32)
    acc += jnp.dot(c_hi, w_lo, preferred_element_type=F32)
    o_ref[0] = acc + b_ref[0]


def _ada_table(c, w_ada, b_ada):
    n_blk = N_ADA
    return pl.pallas_call(
        _ada_kernel,
        grid=(DEPTH, n_blk),
        in_specs=[
            pl.BlockSpec((BATCH, D_MODEL), lambda l, n: (0, 0)),
            pl.BlockSpec((1, D_MODEL, D_MODEL), lambda l, n: (l, 0, n)),
            pl.BlockSpec((1, 1, D_MODEL), lambda l, n: (l, 0, n)),
        ],
        out_specs=pl.BlockSpec((1, BATCH, D_MODEL), lambda l, n: (l, 0, n)),
        out_shape=jax.ShapeDtypeStruct((DEPTH, BATCH, N_ADA * D_MODEL), F32),
        name="ada_table",
    )(c, w_ada, b_ada.reshape(DEPTH, 1, N_ADA * D_MODEL))


def _mix_kernel(x_ref, ada_ref, n1g_ref, n2g_ref, win_ref, cw_ref, cb_ref, spg_ref, spw_ref,
                spbias_ref, mog_ref, wout_ref, rwh_ref, rwl_ref, rb_ref, hred_ref, hexp_ref,
                tri_ref, x1_ref, h2p_ref, rt_ref, cnt_ref, z_scr, cnt_scr):
    i = pl.program_id(0)

    @pl.when(i % TILES_PER_SEQ == 0)
    def _():
        z_scr[pl.ds(0, SUBLANES), :] = jnp.zeros((SUBLANES, D_CONV), F32)

    @pl.when(i % TILES_PER_CHUNK == 0)
    def _():
        cnt_scr[...] = jnp.zeros((N_EXPERTS, LANES), F32)

    chains = [
        _mix_rows(sb * SB, x_ref, ada_ref, n1g_ref, n2g_ref, win_ref, cw_ref, cb_ref, spg_ref,
                  spw_ref, spbias_ref, mog_ref, wout_ref, rwh_ref, rwl_ref, rb_ref, hred_ref,
                  hexp_ref, tri_ref, x1_ref, h2p_ref, rt_ref, z_scr, cnt_scr)
        for sb in range(TM // SB)]
    for c in MIX_STAGE_ORDER:
        next(chains[c])
    z_scr[pl.ds(0, SUBLANES), :] = z_scr[pl.ds(TM, SUBLANES), :]
    cnt_ref[0] = cnt_scr[...]


MIX_STAGES = 8
MIX_SKEW = 2
MIX_STAGE_ORDER = tuple(
    c for t in range(MIX_STAGES + MIX_SKEW * (TM // SB - 1)) for c in range(TM // SB)
    if 0 <= t - MIX_SKEW * c < MIX_STAGES)


def _mix_rows(r0, x_ref, ada_ref, n1g_ref, n2g_ref, win_ref, cw_ref, cb_ref, spg_ref, spw_ref,
              spbias_ref, mog_ref, wout_ref, rwh_ref, rwl_ref, rb_ref, hred_ref, hexp_ref,
              tri_ref, x1_ref, h2p_ref, rt_ref, z_scr, cnt_scr):
    x = x_ref[pl.ds(r0, SB), :]
    ada = ada_ref[0]
    sh1, sc1, g1 = ada[0:1], ada[1:2], ada[2:3]
    sh2, sc2 = ada[3:4], ada[4:5]

    rinv = lax.rsqrt(jnp.mean(x * x, axis=-1, keepdims=True) + EPS)
    h = (x * rinv) * (n1g_ref[...] * (1.0 + sc1)) + sh1
    hb = h.astype(BF16)
    yield

    def proj(k):
        return jnp.dot(hb, win_ref[:, k * 512:(k + 1) * 512], preferred_element_type=F32)

    z = proj(1) * proj(2)
    z_scr[pl.ds(SUBLANES + r0, SB), :] = z
    z1 = z_scr[pl.ds(SUBLANES + r0 - 1, SB), :]
    z2 = z_scr[pl.ds(SUBLANES + r0 - 2, SB), :]
    yield
    p_b, p_u, p_v = proj(0), proj(3), proj(4)
    yield
    cw = cw_ref[...]
    acc = cb_ref[...] + cw[0:1] * z2
    acc = acc + cw[1:2] * z1
    acc = acc + cw[2:3] * z
    y_conv = p_b * acc

    ug = _gelu_tanh(p_u)
    vg = _gelu_tanh(p_v)
    vr = lax.rsqrt(jnp.mean(vg * vg, axis=-1, keepdims=True) + EPS)
    vn = (vg * vr * spg_ref[...]).astype(BF16)
    lane = lax.broadcasted_iota(I32, (1, LANES), 1)
    first_head = lane < HEAD_DIM
    zero = jnp.zeros((), BF16)
    s_rows = []
    for blk in range(SB // SP_BLOCK):
        cols = []
        for p in range(N_SP_HEADS // 2):
            vv = vn[blk * SP_BLOCK:(blk + 1) * SP_BLOCK, p * LANES:(p + 1) * LANES]
            rhs = jnp.concatenate([jnp.where(first_head, vv, zero), jnp.where(first_head, zero, vv)], axis=0)
            cols.append(jnp.dot(spw_ref[p], rhs, preferred_element_type=F32))
        s_rows.append(jnp.concatenate(cols, axis=-1) + spbias_ref[...])
    y_sp = ug * jnp.concatenate(s_rows, axis=0)
    yield

    y = jnp.concatenate([y_conv, y_sp], axis=-1)
    ms = jnp.dot((y * y).astype(BF16), hred_ref[...], preferred_element_type=F32)
    r_hi, r_lo = _split_bf16(lax.rsqrt(ms + EPS))
    scale = jnp.dot(jnp.concatenate([r_hi, r_lo], axis=-1), hexp_ref[...], preferred_element_type=F32)
    yn = (y * scale * mog_ref[...]).astype(BF16)
    yield
    x1 = x + g1 * jnp.dot(yn, wout_ref[...], preferred_element_type=F32)
    x1_ref[pl.ds(r0, SB), :] = x1
    yield

    r2 = lax.rsqrt(jnp.mean(x1 * x1, axis=-1, keepdims=True) + EPS)
    h2 = (x1 * r2) * (n2g_ref[...] * (1.0 + sc2)) + sh2
    h2_hi, h2_lo = _split_bf16(h2)
    packed = _pack_halves(h2_hi.astype(F32))
    for j in range(ROW_TILES):
        h2p_ref[pl.ds(r0 // SUBLANES, SB // SUBLANES), j, :, :] = (
            packed[:, j * LANES:(j + 1) * LANES].reshape(SB // SUBLANES, SUBLANES, LANES))
    yield

    logits = jnp.dot(h2_hi, rwh_ref[...], preferred_element_type=F32)
    logits += jnp.dot(h2_lo, rwh_ref[...], preferred_element_type=F32)
    logits += jnp.dot(h2_hi, rwl_ref[...], preferred_element_type=F32)
    lg = logits.T[0:N_EXPERTS, :]
    scores = 1.0 / (1.0 + jnp.exp(-lg))
    sel = scores + rb_ref[...]

    gs = []
    for g in range(N_GROUPS):
        a, b, c, d = (sel[g * PER_GROUP + k:g * PER_GROUP + k + 1, :] for k in range(PER_GROUP))
        hi1, lo1 = jnp.maximum(a, b), jnp.minimum(a, b)
        hi2, lo2 = jnp.maximum(c, d), jnp.minimum(c, d)
        gs.append(jnp.maximum(hi1, hi2) + jnp.maximum(jnp.minimum(hi1, hi2), jnp.maximum(lo1, lo2)))
    best = gs[0]
    top_group = jnp.zeros((1, SB), I32)
    for g in range(1, N_GROUPS):
        better = gs[g] > best
        best = jnp.where(better, gs[g], best)
        top_group = jnp.where(better, g, top_group)

    eid_i = lax.broadcasted_iota(I32, (N_EXPERTS, SB), 0)
    eid = eid_i.astype(F32)
    neg_inf = jnp.float32(-jnp.inf)
    masked = jnp.where((eid_i // PER_GROUP) == top_group, sel, neg_inf)
    m1 = jnp.max(masked, axis=0, keepdims=True)
    i1 = jnp.min(jnp.where(masked == m1, eid, float(N_EXPERTS)), axis=0, keepdims=True)
    masked2 = jnp.where(eid == i1, neg_inf, masked)
    m2 = jnp.max(masked2, axis=0, keepdims=True)
    i2 = jnp.min(jnp.where(masked2 == m2, eid, float(N_EXPERTS)), axis=0, keepdims=True)
    pick1 = eid == i1
    pick2 = eid == i2
    w1 = jnp.sum(jnp.where(pick1, scores, 0.0), axis=0, keepdims=True)
    w2 = jnp.sum(jnp.where(pick2, scores, 0.0), axis=0, keepdims=True)
    wsum = w1 + w2
    c1 = w1 / wsum
    c2 = w2 / wsum

    assigned = jnp.where(pick1 | pick2, 1.0, 0.0)
    before = jnp.dot(assigned.astype(BF16), tri_ref[...], preferred_element_type=F32)
    carry = cnt_scr[...]
    rank = before + carry[:, 0:1]
    rank1 = jnp.sum(jnp.where(pick1, rank, 0.0), axis=0, keepdims=True)
    rank2 = jnp.sum(jnp.where(pick2, rank, 0.0), axis=0, keepdims=True)
    cnt_scr[...] = carry + jnp.sum(assigned, axis=1, keepdims=True)
    unused = jnp.zeros((1, SB), F32)
    for r, row in enumerate((i1, i2, c1, c2, rank1, rank2, unused, unused)):
        rt_ref[r:r + 1, pl.ds(r0, SB)] = row
    yield


def _mix_call(x2d, ada_l, consts):
    n_tiles = N_TOK // TM
    full = lambda shape: pl.BlockSpec(shape, lambda i: (0,) * len(shape))
    in_specs = [
        pl.BlockSpec((TM, D_MODEL), lambda i: (i, 0)),
        pl.BlockSpec((1, N_ADA, D_MODEL), lambda i: (i // TILES_PER_SEQ, 0, 0)),
        full((1, D_MODEL)), full((1, D_MODEL)),
        full((D_MODEL, 5 * 512)),
        full((3, D_CONV)), full((1, D_CONV)), full((1, D_SPATIAL)),
        full((N_SP_HEADS // 2, SP_BLOCK, 2 * SP_BLOCK)),
        full((SP_BLOCK, D_SPATIAL)),
        full((1, D_MODEL)),
        full((D_MODEL, D_MODEL)),
        full((D_MODEL, LANES)), full((D_MODEL, LANES)), full((N_EXPERTS, 1)),
        full((D_MODEL, LANES)), full((2 * LANES, D_MODEL)),
        full((SB, SB)),
    ]
    out_specs = [
        pl.BlockSpec((TM, D_MODEL), lambda i: (i, 0)),
        pl.BlockSpec((TM // SUBLANES, ROW_TILES, SUBLANES, LANES), lambda i: (i, 0, 0, 0)),
        pl.BlockSpec((SUBLANES, TM), lambda i: (0, i)),
        pl.BlockSpec((1, N_EXPERTS, LANES), lambda i: (i, 0, 0)),
    ]
    out_shape = [
        jax.ShapeDtypeStruct((N_TOK, D_MODEL), F32),
        jax.ShapeDtypeStruct((N_TOK // SUBLANES, ROW_TILES, SUBLANES, LANES), U32),
        jax.ShapeDtypeStruct((SUBLANES, N_TOK), F32),
        jax.ShapeDtypeStruct((n_tiles, N_EXPERTS, LANES), F32),
    ]
    return pl.pallas_call(
        _mix_kernel,
        grid=(n_tiles,),
        in_specs=in_specs,
        out_specs=out_specs,
        out_shape=out_shape,
        scratch_shapes=[pltpu.VMEM((TM + SUBLANES, D_CONV), F32), pltpu.VMEM((N_EXPERTS, LANES), F32)],
        compiler_params=pltpu.CompilerParams(
            dimension_semantics=("arbitrary",), vmem_limit_bytes=VMEM_LIMIT),
        name="mix",
    )(x2d, ada_l, *consts)


def _row_addr(t):
    return (t // SUBLANES) * (ROW_TILES * SUBLANES) + t % SUBLANES


def _dispatch_kernel(a0_ref, a1_ref, h2p_ref, xs_ref):
    c = pl.program_id(0)
    s = pl.program_id(1)

    @pl.when(s == 0)
    def _():
        xs_ref[...] = jnp.zeros(xs_ref.shape, U32)

    base = c * ROUTE_CHUNK + s * TD

    def body(it, carry):
        for k in range(LOOP_TOKENS):
            t = it * LOOP_TOKENS + k
            src = it * (LOOP_TOKENS * ROW_TILES) + _row_addr(k)
            row = h2p_ref[pl.ds(src, ROW_TILES, stride=SUBLANES), :]
            xs_ref[pl.ds(a0_ref[base + t], ROW_TILES, stride=SUBLANES), :] = row
            xs_ref[pl.ds(a1_ref[base + t], ROW_TILES, stride=SUBLANES), :] = row
        return carry

    lax.fori_loop(0, TD // LOOP_TOKENS, body, 0)


def _dispatch_call(a0, a1, h2p_2d):
    steps = ROUTE_CHUNK // TD
    return pl.pallas_call(
        _dispatch_kernel,
        grid_spec=pltpu.PrefetchScalarGridSpec(
            num_scalar_prefetch=2,
            grid=(N_CHUNKS, steps),
            in_specs=[pl.BlockSpec((TD * ROW_TILES, LANES), lambda c, s, a0, a1: (c * steps + s, 0))],
            out_specs=pl.BlockSpec((CHUNK_ROWS * ROW_TILES, LANES), lambda c, s, a0, a1: (c, 0)),
        ),
        out_shape=jax.ShapeDtypeStruct((TOTAL_ROWS * ROW_TILES, LANES), U32),
        compiler_params=pltpu.CompilerParams(
            dimension_semantics=("arbitrary", "arbitrary"), vmem_limit_bytes=VMEM_LIMIT),
        name="dispatch",
    )(a0, a1, h2p_2d)


def _expert_kernel(tb_ref, gst_ref, xs_hbm, wg_ref, wu_ref, wd_ref, ys_hbm,
                   wg_s, wu_s, wd_s, xbuf, ybuf, sem_in, sem_out):
    e = pl.program_id(0)
    g_first = gst_ref[e]
    g_end = gst_ref[e + 1]
    g_all = gst_ref[N_EXPERTS]

    def in_copy(q, t):
        rows = pl.ds(tb_ref[GROUP * q + t] * ROW_GROUPS, ROW_GROUPS)
        return pltpu.make_async_copy(xs_hbm.at[rows], xbuf.at[q % 2, t], sem_in.at[q % 2, t])

    def out_copy(q, t):
        rows = pl.ds(tb_ref[GROUP * q + t] * ROW_GROUPS, ROW_GROUPS)
        return pltpu.make_async_copy(ybuf.at[q % 2, t], ys_hbm.at[rows], sem_out.at[q % 2, t])

    def for_group(q, action, make):
        for t in range(GROUP):
            @pl.when(tb_ref[GROUP * q + t] >= 0)
            def _():
                if action == "start":
                    make(q, t).start(priority=1)
                else:
                    make(q, t).wait()

    @pl.when(e == 0)
    def _():
        xbuf[...] = jnp.zeros(xbuf.shape, U32)

        @pl.when(g_all > 0)
        def _():
            for_group(0, "start", in_copy)

    wg_s[...] = wg_ref[0, 0].astype(BF16)
    wu_s[...] = wu_ref[0, 0].astype(BF16)
    wd_s[...] = wd_ref[0, 0].astype(BF16)

    def compute(slot, tiles):
        los, his = [], []
        for j in range(ROW_TILES):
            words = jnp.concatenate(
                [xbuf[slot, t, :, j, :, :].reshape(ROW_TILE, LANES) for t in tiles], axis=0)
            lo, hi = _unpack_halves(words)
            los.append(lo)
            his.append(hi)
        xb = jnp.concatenate(los + his, axis=-1).astype(BF16)
        g = jnp.dot(xb, wg_s[...], preferred_element_type=F32)
        u = jnp.dot(xb, wu_s[...], preferred_element_type=F32)
        act = (g * (1.0 / (1.0 + jnp.exp(-g))) * u).astype(BF16)
        y = jnp.dot(act, wd_s[...], preferred_element_type=F32)
        packed = _pack_halves(y.astype(BF16).astype(F32))
        for n, t in enumerate(tiles):
            for j in range(ROW_TILES):
                ybuf[slot, t, :, j, :, :] = packed[n * ROW_TILE:(n + 1) * ROW_TILE,
                                                   j * LANES:(j + 1) * LANES].reshape(
                    ROW_GROUPS, SUBLANES, LANES)

    def body(q, carry):
        slot = q % 2
        for_group(q, "wait", in_copy)

        @pl.when(q + 1 < g_all)
        def _():
            for_group(q + 1, "start", in_copy)

        @pl.when(q >= 2)
        def _():
            for_group(q - 2, "wait", out_copy)

        compute(slot, (0, 1))
        compute(slot, (2, 3))
        for_group(q, "start", out_copy)
        return carry

    lax.fori_loop(g_first, g_end, body, 0)

    @pl.when(e == N_EXPERTS - 1)
    def _():
        @pl.when(g_all >= 2)
        def _():
            for_group(g_all - 2, "wait", out_copy)

        @pl.when(g_all >= 1)
        def _():
            for_group(g_all - 1, "wait", out_copy)


def _expert_call(layer, group_tiles, group_start, xs4, wg, wu, wd):
    group_buf = (2, GROUP, ROW_GROUPS, ROW_TILES, SUBLANES, LANES)
    return pl.pallas_call(
        _expert_kernel,
        grid_spec=pltpu.PrefetchScalarGridSpec(
            num_scalar_prefetch=2,
            grid=(N_EXPERTS,),
            in_specs=[
                pl.BlockSpec(memory_space=pl.ANY),
                pl.BlockSpec((1, 1, D_MODEL, D_EXPERT), lambda e, tb, gs: (layer, e, 0, 0)),
                pl.BlockSpec((1, 1, D_MODEL, D_EXPERT), lambda e, tb, gs: (layer, e, 0, 0)),
                pl.BlockSpec((1, 1, D_EXPERT, D_MODEL), lambda e, tb, gs: (layer, e, 0, 0)),
            ],
            out_specs=pl.BlockSpec(memory_space=pl.ANY),
            scratch_shapes=[
                pltpu.VMEM((D_MODEL, D_EXPERT), BF16),
                pltpu.VMEM((D_MODEL, D_EXPERT), BF16),
                pltpu.VMEM((D_EXPERT, D_MODEL), BF16),
                pltpu.VMEM(group_buf, U32),
                pltpu.VMEM(group_buf, U32),
                pltpu.SemaphoreType.DMA((2, GROUP)),
                pltpu.SemaphoreType.DMA((2, GROUP)),
            ],
        ),
        out_shape=jax.ShapeDtypeStruct((TOTAL_ROWS // SUBLANES, ROW_TILES, SUBLANES, LANES), U32),
        input_output_aliases={2: 0},
        compiler_params=pltpu.CompilerParams(
            dimension_semantics=("arbitrary",), vmem_limit_bytes=VMEM_LIMIT),
        name="experts",
    )(group_tiles, group_start, xs4, wg, wu, wd)


def _combine_kernel(a0_ref, a1_ref, ys_ref, x1_ref, w_ref, ada_ref, fg_ref, out_ref, r0_scr, r1_scr,
                    *, final):
    c = pl.program_id(0)
    s = pl.program_id(1)
    base = c * ROUTE_CHUNK + s * TC

    def body(it, carry):
        for k in range(LOOP_TOKENS):
            t = base + it * LOOP_TOKENS + k
            dst = pl.ds(it * (LOOP_TOKENS * ROW_TILES) + _row_addr(k), ROW_TILES, stride=SUBLANES)
            r0_scr[dst, :] = ys_ref[pl.ds(a0_ref[t], ROW_TILES, stride=SUBLANES), :]
            r1_scr[dst, :] = ys_ref[pl.ds(a1_ref[t], ROW_TILES, stride=SUBLANES), :]
        return carry

    lax.fori_loop(0, TC // LOOP_TOKENS, body, 0)

    w = w_ref[...]
    w0 = w[:, 0:1]
    w1 = w[:, 1:2]
    g2 = ada_ref[0][5:6]
    n_grp = TC // SUBLANES
    los, his = [], []
    for j in range(ROW_TILES):
        parts0 = [r0_scr[pl.ds((tb * ROW_TILES + j) * SUBLANES, SUBLANES), :] for tb in range(n_grp)]
        parts1 = [r1_scr[pl.ds((tb * ROW_TILES + j) * SUBLANES, SUBLANES), :] for tb in range(n_grp)]
        lo0, hi0 = _unpack_halves(jnp.concatenate(parts0, axis=0))
        lo1, hi1 = _unpack_halves(jnp.concatenate(parts1, axis=0))
        los.append(w0 * lo0 + w1 * lo1)
        his.append(w0 * hi0 + w1 * hi1)
    moe = jnp.concatenate(los + his, axis=-1)
    x2 = x1_ref[...] + g2 * moe
    if final:
        r = lax.rsqrt(jnp.mean(x2 * x2, axis=-1, keepdims=True) + EPS)
        x2 = x2 * r * fg_ref[...]
    out_ref[...] = x2


def _combine_call(a0, a1, ys_2d, x1, wcol, ada_l, final_g, final):
    steps = ROUTE_CHUNK // TC
    tok = lambda c, s, a0, a1: (c * steps + s, 0)
    return pl.pallas_call(
        functools.partial(_combine_kernel, final=final),
        grid_spec=pltpu.PrefetchScalarGridSpec(
            num_scalar_prefetch=2,
            grid=(N_CHUNKS, steps),
            in_specs=[
                pl.BlockSpec((CHUNK_ROWS * ROW_TILES, LANES), lambda c, s, a0, a1: (c, 0)),
                pl.BlockSpec((TC, D_MODEL), tok),
                pl.BlockSpec((TC, 2), tok),
                pl.BlockSpec((1, N_ADA, D_MODEL),
                             lambda c, s, a0, a1: ((c * ROUTE_CHUNK + s * TC) // SEQ, 0, 0)),
                pl.BlockSpec((1, D_MODEL), lambda c, s, a0, a1: (0, 0)),
            ],
            out_specs=pl.BlockSpec((TC, D_MODEL), tok),
            scratch_shapes=[pltpu.VMEM((TC * ROW_TILES, LANES), U32),
                            pltpu.VMEM((TC * ROW_TILES, LANES), U32)],
        ),
        out_shape=jax.ShapeDtypeStruct((N_TOK, D_MODEL), F32),
        compiler_params=pltpu.CompilerParams(
            dimension_semantics=("arbitrary", "arbitrary"), vmem_limit_bytes=VMEM_LIMIT),
        name="combine_final" if final else "combine",
    )(a0, a1, ys_2d, x1, wcol, ada_l, final_g)


def _routing_tables(rt, cnt):
    e0 = rt[0].astype(I32)
    e1 = rt[1].astype(I32)
    rank0 = rt[4].astype(I32)
    rank1 = rt[5].astype(I32)
    counts = cnt[TILES_PER_CHUNK - 1::TILES_PER_CHUNK, :, 0].astype(I32)
    n_tiles = (counts + ROW_TILE - 1) // ROW_TILE
    tile_off = jnp.cumsum(n_tiles, axis=1) - n_tiles
    row_off = tile_off * ROW_TILE
    experts = jnp.arange(N_EXPERTS, dtype=I32)[None, :, None]
    off_tok = row_off[:, :, None]
    pick = lambda e: jnp.sum(
        jnp.where(e.reshape(N_CHUNKS, 1, ROUTE_CHUNK) == experts, off_tok, 0), axis=1).reshape(N_TOK)
    slot0 = pick(e0) + rank0
    slot1 = pick(e1) + rank1
    a0 = _row_addr(slot0)
    a1 = _row_addr(slot1)

    nt_flat = n_tiles.T.reshape(-1)
    ends = jnp.cumsum(nt_flat)
    starts = ends - nt_flat
    total = ends[-1]
    step = jnp.minimum(jnp.arange(MAX_ROW_TILES, dtype=I32), total - 1)
    seg = jnp.sum((ends[None, :] <= step[:, None]).astype(I32), axis=1)
    in_seg = seg[:, None] == jnp.arange(N_EXPERTS * N_CHUNKS, dtype=I32)[None, :]
    k = step - jnp.sum(jnp.where(in_seg, starts[None, :], 0), axis=1)
    c = seg % N_CHUNKS
    tile_off_flat = tile_off.T.reshape(-1)
    tile_blk = c * (CHUNK_ROWS // ROW_TILE) + jnp.sum(jnp.where(in_seg, tile_off_flat[None, :], 0), axis=1) + k
    e_count = jnp.sum(n_tiles, axis=0)
    e_start = jnp.cumsum(e_count) - e_count
    g_count = (e_count + GROUP - 1) // GROUP
    g_end = jnp.cumsum(g_count)
    g_start = g_end - g_count
    slot_id = jnp.arange(MAX_GROUPS * GROUP, dtype=I32)
    q = slot_id // GROUP
    owner = jnp.sum((g_end[None, :] <= q[:, None]).astype(I32), axis=1)
    is_owner = owner[:, None] == jnp.arange(N_EXPERTS, dtype=I32)[None, :]
    sel = lambda v: jnp.sum(jnp.where(is_owner, v[None, :], 0), axis=1)
    k = (q - sel(g_start)) * GROUP + slot_id % GROUP
    valid = (owner < N_EXPERTS) & (k < sel(e_count))
    flat = sel(e_start) + k
    pick_tile = flat[:, None] == jnp.arange(MAX_ROW_TILES, dtype=I32)[None, :]
    group_tiles = jnp.where(valid, jnp.sum(jnp.where(pick_tile, tile_blk[None, :], 0), axis=1), -1)
    group_start = jnp.concatenate([g_start, g_end[-1:]])
    return a0, a1, group_tiles.astype(I32), group_start.astype(I32)


def _layer_consts(l, norm1_g, norm2_g, w_in, conv_w, conv_b, sp_norm_g, sp_w, sp_b, mix_out_g,
                  w_out, shared):
    pos = jnp.arange(SP_BLOCK)
    mask = (pos[None, :] // CHUNK) <= (pos[:, None] // CHUNK)
    w = jnp.where(mask[None], sp_w[l], 0.0)
    w_pairs = jnp.concatenate([w[0::2], w[1::2]], axis=-1).astype(BF16)
    sp_bias = jnp.repeat(sp_b[l].T, HEAD_DIM, axis=1)
    return (
        norm1_g[l].reshape(1, D_MODEL), norm2_g[l].reshape(1, D_MODEL),
        w_in[l].astype(BF16),
        conv_w[l], conv_b[l].reshape(1, D_CONV), sp_norm_g[l].reshape(1, D_SPATIAL),
        w_pairs, sp_bias,
        mix_out_g[l].reshape(1, D_MODEL),
        w_out[l].astype(BF16),
    ) + shared


def kernel(x, c, w_ada, b_ada, norm1_g, norm2_g, w_in, conv_w, conv_b, sp_norm_g, sp_w, sp_b, mix_out_g, w_out, router_w, router_b, exp_w_gate, exp_w_up, exp_w_down, final_g):
    ada = _ada_table(c, w_ada, b_ada).reshape(DEPTH, BATCH, N_ADA, D_MODEL)

    rw = jnp.pad(router_w, ((0, 0), (0, LANES - N_EXPERTS)))
    rw_hi = rw.astype(BF16)
    rw_lo = (rw - rw_hi.astype(F32)).astype(BF16)
    head_of = jnp.arange(D_MODEL) // HEAD_DIM
    hred = (head_of[:, None] == jnp.arange(LANES)[None, :]).astype(F32) / HEAD_DIM
    hexp_half = (jnp.arange(LANES)[:, None] == head_of[None, :]).astype(F32)
    hexp = jnp.concatenate([hexp_half, hexp_half], axis=0)
    tri = (jnp.arange(SB)[:, None] < jnp.arange(SB)[None, :])
    shared = (rw_hi, rw_lo, router_b.reshape(N_EXPERTS, 1), hred.astype(BF16), hexp.astype(BF16),
              tri.astype(BF16))
    fg = final_g.reshape(1, D_MODEL)

    xc = x.reshape(N_TOK, D_MODEL)
    for l in range(DEPTH):
        consts = _layer_consts(l, norm1_g, norm2_g, w_in, conv_w, conv_b, sp_norm_g, sp_w, sp_b,
                               mix_out_g, w_out, shared)
        x1, h2p, rt, cnt = _mix_call(xc, ada[l], consts)
        a0, a1, group_tiles, group_start = _routing_tables(rt, cnt)
        xs = _dispatch_call(a0, a1, h2p.reshape(N_TOK * ROW_TILES, LANES))
        ys = _expert_call(l, group_tiles, group_start,
                          xs.reshape(TOTAL_ROWS // SUBLANES, ROW_TILES, SUBLANES, LANES),
                          exp_w_gate, exp_w_up, exp_w_down)
        wcol = rt[2:4].T
        xc = _combine_call(a0, a1, ys.reshape(TOTAL_ROWS * ROW_TILES, LANES), x1, wcol, ada[l], fg,
                           final=(l == DEPTH - 1))
    return xc.reshape(BATCH, SEQ, D_MODEL)
```

```python
import functools

import jax
import jax.numpy as jnp
from jax import lax
from jax.experimental import pallas as pl
from jax.experimental.pallas import tpu as pltpu

F32 = jnp.float32
BF16 = jnp.bfloat16
U32 = jnp.uint32
I32 = jnp.int32

D_MODEL = 1024
BATCH = 8
SEQ = 2048
DEPTH = 4
N_TOK = BATCH * SEQ
CHUNK = 64
HEAD_DIM = 64
D_CONV = 512
D_SPATIAL = 512
N_SP_HEADS = 8
N_MIX_HEADS = 16
SP_BLOCK = 128
N_EXPERTS = 16
N_GROUPS = 4
PER_GROUP = 4
D_EXPERT = 512
N_ADA = 6
EPS = 1e-6

LANES = 128
SUBLANES = 8
HALF = D_MODEL // 2
ROW_TILES = HALF // LANES

TM = 1024
SB = 256
TILES_PER_SEQ = SEQ // TM
ROUTE_CHUNK = 4096
N_CHUNKS = N_TOK // ROUTE_CHUNK
TILES_PER_CHUNK = ROUTE_CHUNK // TM
ROW_TILE = 128
ROW_GROUPS = ROW_TILE // SUBLANES
GROUP = 4
CHUNK_ROWS = 2 * ROUTE_CHUNK + N_EXPERTS * ROW_TILE
TOTAL_ROWS = N_CHUNKS * CHUNK_ROWS
MAX_ROW_TILES = TOTAL_ROWS // ROW_TILE
MAX_GROUPS = MAX_ROW_TILES // GROUP + N_EXPERTS
LOOP_TOKENS = 16
TD = 1024
TC = 512
VMEM_LIMIT = 58 * 1024 * 1024


def _gelu_tanh(x):
    c = 0.7978845608028654
    return 0.5 * x * (1.0 + jnp.tanh(c * (x + 0.044715 * (x * x * x))))


def _split_bf16(x):
    hi = x.astype(BF16)
    lo = (x - hi.astype(F32)).astype(BF16)
    return hi, lo


def _pack_halves(x):
    return pltpu.pack_elementwise([x[:, :HALF], x[:, HALF:]], packed_dtype=BF16)


def _unpack_halves(words):
    lo = pltpu.unpack_elementwise(words, index=0, packed_dtype=BF16, unpacked_dtype=F32)
    hi = pltpu.unpack_elementwise(words, index=1, packed_dtype=BF16, unpacked_dtype=F32)
    return lo, hi


def _ada_kernel(c_ref, w_ref, b_ref, o_ref):
    c = c_ref[...]
    ca = c * (1.0 / (1.0 + jnp.exp(-c)))
    c_hi, c_lo = _split_bf16(ca)
    w_hi, w_lo = _split_bf16(w_ref[0])
    acc = jnp.dot(c_hi, w_hi, preferred_element_type=F32)
    acc += jnp.dot(c_lo, w_hi, preferred_element_type=F32)
    acc += jnp.dot(c_hi, w_lo, preferred_element_type=F32)
    o_ref[0] = acc + b_ref[0]


def _ada_table(c, w_ada, b_ada):
    n_blk = N_ADA
    return pl.pallas_call(
        _ada_kernel,
        grid=(DEPTH, n_blk),
        in_specs=[
            pl.BlockSpec((BATCH, D_MODEL), lambda l, n: (0, 0)),
            pl.BlockSpec((1, D_MODEL, D_MODEL), lambda l, n: (l, 0, n)),
            pl.BlockSpec((1, 1, D_MODEL), lambda l, n: (l, 0, n)),
        ],
        out_specs=pl.BlockSpec((1, BATCH, D_MODEL), lambda l, n: (l, 0, n)),
        out_shape=jax.ShapeDtypeStruct((DEPTH, BATCH, N_ADA * D_MODEL), F32),
        name="ada_table",
    )(c, w_ada, b_ada.reshape(DEPTH, 1, N_ADA * D_MODEL))


def _mix_kernel(x_ref, ada_ref, n1g_ref, n2g_ref, win_ref, cw_ref, cb_ref, spg_ref, spw_ref,
                spbias_ref, mog_ref, wout_ref, rw_ref, rb_ref, hred_ref, hexp_ref,
                tri_ref, x1_ref, h2p_ref, rt_ref, cnt_ref, z_scr, cnt_scr):
    i = pl.program_id(0)

    @pl.when(i % TILES_PER_SEQ == 0)
    def _():
        z_scr[pl.ds(0, SUBLANES), :] = jnp.zeros((SUBLANES, D_CONV), F32)

    @pl.when(i % TILES_PER_CHUNK == 0)
    def _():
        cnt_scr[...] = jnp.zeros((N_EXPERTS, LANES), F32)

    chains = [
        _mix_rows(sb * SB, x_ref, ada_ref, n1g_ref, n2g_ref, win_ref, cw_ref, cb_ref, spg_ref,
                  spw_ref, spbias_ref, mog_ref, wout_ref, rw_ref, rb_ref, hred_ref,
                  hexp_ref, tri_ref, x1_ref, h2p_ref, rt_ref, z_scr, cnt_scr)
        for sb in range(TM // SB)]
    for c in MIX_STAGE_ORDER:
        next(chains[c])
    z_scr[pl.ds(0, SUBLANES), :] = z_scr[pl.ds(TM, SUBLANES), :]
    cnt_ref[0] = cnt_scr[...]


MIX_STAGES = 8
MIX_SKEW = 2
MIX_STAGE_ORDER = tuple(
    c for t in range(MIX_STAGES + MIX_SKEW * (TM // SB - 1)) for c in range(TM // SB)
    if 0 <= t - MIX_SKEW * c < MIX_STAGES)


def _mix_rows(r0, x_ref, ada_ref, n1g_ref, n2g_ref, win_ref, cw_ref, cb_ref, spg_ref, spw_ref,
              spbias_ref, mog_ref, wout_ref, rw_ref, rb_ref, hred_ref, hexp_ref,
              tri_ref, x1_ref, h2p_ref, rt_ref, z_scr, cnt_scr):
    x = x_ref[pl.ds(r0, SB), :]
    ada = ada_ref[0]
    sh1, sc1, g1 = ada[0:1], ada[1:2], ada[2:3]
    sh2, sc2 = ada[3:4], ada[4:5]

    rinv = lax.rsqrt(jnp.mean(x * x, axis=-1, keepdims=True) + EPS)
    h = (x * rinv) * (n1g_ref[...] * (1.0 + sc1)) + sh1
    hb = h.astype(BF16)
    yield

    def proj(k):
        return jnp.dot(hb, win_ref[:, k * 512:(k + 1) * 512], preferred_element_type=F32)

    z = proj(1) * proj(2)
    z_scr[pl.ds(SUBLANES + r0, SB), :] = z
    z1 = z_scr[pl.ds(SUBLANES + r0 - 1, SB), :]
    z2 = z_scr[pl.ds(SUBLANES + r0 - 2, SB), :]
    yield
    p_b, p_u, p_v = proj(0), proj(3), proj(4)
    yield
    cw = cw_ref[...]
    acc = cb_ref[...] + cw[0:1] * z2
    acc = acc + cw[1:2] * z1
    acc = acc + cw[2:3] * z
    y_conv = p_b * acc

    ug = _gelu_tanh(p_u)
    vg = _gelu_tanh(p_v)
    vr = lax.rsqrt(jnp.mean(vg * vg, axis=-1, keepdims=True) + EPS)
    vn = (vg * vr * spg_ref[...]).astype(BF16)
    lane = lax.broadcasted_iota(I32, (1, LANES), 1)
    first_head = lane < HEAD_DIM
    zero = jnp.zeros((), BF16)
    s_rows = []
    for blk in range(SB // SP_BLOCK):
        cols = []
        for p in range(N_SP_HEADS // 2):
            vv = vn[blk * SP_BLOCK:(blk + 1) * SP_BLOCK, p * LANES:(p + 1) * LANES]
            rhs = jnp.concatenate([jnp.where(first_head, vv, zero), jnp.where(first_head, zero, vv)], axis=0)
            cols.append(jnp.dot(spw_ref[p], rhs, preferred_element_type=F32))
        s_rows.append(jnp.concatenate(cols, axis=-1) + spbias_ref[...])
    y_sp = ug * jnp.concatenate(s_rows, axis=0)
    yield

    y = jnp.concatenate([y_conv, y_sp], axis=-1)
    ms = jnp.dot((y * y).astype(BF16), hred_ref[...], preferred_element_type=F32)
    r_hi, r_lo = _split_bf16(lax.rsqrt(ms + EPS))
    scale = jnp.dot(jnp.concatenate([r_hi, r_lo], axis=-1), hexp_ref[...], preferred_element_type=F32)
    yn = (y * scale * mog_ref[...]).astype(BF16)
    yield
    x1 = x + g1 * jnp.dot(yn, wout_ref[...], preferred_element_type=F32)
    x1_ref[pl.ds(r0, SB), :] = x1
    yield

    r2 = lax.rsqrt(jnp.mean(x1 * x1, axis=-1, keepdims=True) + EPS)
    h2 = (x1 * r2) * (n2g_ref[...] * (1.0 + sc2)) + sh2
    h2_hi = h2.astype(BF16)
    packed = _pack_halves(h2_hi.astype(F32))
    for j in range(ROW_TILES):
        h2p_ref[pl.ds(r0 // SUBLANES, SB // SUBLANES), j, :, :] = (
            packed[:, j * LANES:(j + 1) * LANES].reshape(SB // SUBLANES, SUBLANES, LANES))
    yield

    logits_t = jnp.dot(h2_hi, rw_ref[...], preferred_element_type=F32).T
    lg = logits_t[0:N_EXPERTS, :] + logits_t[N_EXPERTS:2 * N_EXPERTS, :]
    scores = 1.0 / (1.0 + jnp.exp(-lg))
    sel = scores + rb_ref[...]

    gs = []
    for g in range(N_GROUPS):
        a, b, c, d = (sel[g * PER_GROUP + k:g * PER_GROUP + k + 1, :] for k in range(PER_GROUP))
        hi1, lo1 = jnp.maximum(a, b), jnp.minimum(a, b)
        hi2, lo2 = jnp.maximum(c, d), jnp.minimum(c, d)
        gs.append(jnp.maximum(hi1, hi2) + jnp.maximum(jnp.minimum(hi1, hi2), jnp.maximum(lo1, lo2)))
    best = gs[0]
    top_group = jnp.zeros((1, SB), I32)
    for g in range(1, N_GROUPS):
        better = gs[g] > best
        best = jnp.where(better, gs[g], best)
        top_group = jnp.where(better, g, top_group)

    eid_i = lax.broadcasted_iota(I32, (N_EXPERTS, SB), 0)
    eid = eid_i.astype(F32)
    neg_inf = jnp.float32(-jnp.inf)
    masked = jnp.where((eid_i // PER_GROUP) == top_group, sel, neg_inf)
    m1 = jnp.max(masked, axis=0, keepdims=True)
    i1 = jnp.min(jnp.where(masked == m1, eid, float(N_EXPERTS)), axis=0, keepdims=True)
    masked2 = jnp.where(eid == i1, neg_inf, masked)
    m2 = jnp.max(masked2, axis=0, keepdims=True)
    i2 = jnp.min(jnp.where(masked2 == m2, eid, float(N_EXPERTS)), axis=0, keepdims=True)
    pick1 = eid == i1
    pick2 = eid == i2
    w1 = jnp.sum(jnp.where(pick1, scores, 0.0), axis=0, keepdims=True)
    w2 = jnp.sum(jnp.where(pick2, scores, 0.0), axis=0, keepdims=True)
    wsum = w1 + w2
    c1 = w1 / wsum
    c2 = w2 / wsum

    assigned = jnp.where(pick1 | pick2, 1.0, 0.0)
    before = jnp.dot(assigned.astype(BF16), tri_ref[...], preferred_element_type=F32)
    carry = cnt_scr[...]
    rank = before + carry[:, 0:1]
    rank1 = jnp.sum(jnp.where(pick1, rank, 0.0), axis=0, keepdims=True)
    rank2 = jnp.sum(jnp.where(pick2, rank, 0.0), axis=0, keepdims=True)
    cnt_scr[...] = carry + jnp.sum(assigned, axis=1, keepdims=True)
    unused = jnp.zeros((1, SB), F32)
    for r, row in enumerate((i1, i2, c1, c2, rank1, rank2, unused, unused)):
        rt_ref[r:r + 1, pl.ds(r0, SB)] = row
    yield


def _mix_call(x2d, ada_l, consts):
    n_tiles = N_TOK // TM
    full = lambda shape: pl.BlockSpec(shape, lambda i: (0,) * len(shape))
    in_specs = [
        pl.BlockSpec((TM, D_MODEL), lambda i: (i, 0)),
        pl.BlockSpec((1, N_ADA, D_MODEL), lambda i: (i // TILES_PER_SEQ, 0, 0)),
        full((1, D_MODEL)), full((1, D_MODEL)),
        full((D_MODEL, 5 * 512)),
        full((3, D_CONV)), full((1, D_CONV)), full((1, D_SPATIAL)),
        full((N_SP_HEADS // 2, SP_BLOCK, 2 * SP_BLOCK)),
        full((SP_BLOCK, D_SPATIAL)),
        full((1, D_MODEL)),
        full((D_MODEL, D_MODEL)),
        full((D_MODEL, LANES)), full((N_EXPERTS, 1)),
        full((D_MODEL, LANES)), full((2 * LANES, D_MODEL)),
        full((SB, SB)),
    ]
    out_specs = [
        pl.BlockSpec((TM, D_MODEL), lambda i: (i, 0)),
        pl.BlockSpec((TM // SUBLANES, ROW_TILES, SUBLANES, LANES), lambda i: (i, 0, 0, 0)),
        pl.BlockSpec((SUBLANES, TM), lambda i: (0, i)),
        pl.BlockSpec((1, N_EXPERTS, LANES), lambda i: (i, 0, 0)),
    ]
    out_shape = [
        jax.ShapeDtypeStruct((N_TOK, D_MODEL), F32),
        jax.ShapeDtypeStruct((N_TOK // SUBLANES, ROW_TILES, SUBLANES, LANES), U32),
        jax.ShapeDtypeStruct((SUBLANES, N_TOK), F32),
        jax.ShapeDtypeStruct((n_tiles, N_EXPERTS, LANES), F32),
    ]
    return pl.pallas_call(
        _mix_kernel,
        grid=(n_tiles,),
        in_specs=in_specs,
        out_specs=out_specs,
        out_shape=out_shape,
        scratch_shapes=[pltpu.VMEM((TM + SUBLANES, D_CONV), F32), pltpu.VMEM((N_EXPERTS, LANES), F32)],
        compiler_params=pltpu.CompilerParams(
            dimension_semantics=("arbitrary",), vmem_limit_bytes=VMEM_LIMIT),
        name="mix",
    )(x2d, ada_l, *consts)


def _row_addr(t):
    return (t // SUBLANES) * (ROW_TILES * SUBLANES) + t % SUBLANES


def _dispatch_kernel(a0_ref, a1_ref, h2p_ref, xs_ref):
    c = pl.program_id(0)
    s = pl.program_id(1)

    @pl.when(s == 0)
    def _():
        xs_ref[...] = jnp.zeros(xs_ref.shape, U32)

    base = c * ROUTE_CHUNK + s * TD

    def body(it, carry):
        for k in range(LOOP_TOKENS):
            t = it * LOOP_TOKENS + k
            src = it * (LOOP_TOKENS * ROW_TILES) + _row_addr(k)
            row = h2p_ref[pl.ds(src, ROW_TILES, stride=SUBLANES), :]
            xs_ref[pl.ds(a0_ref[base + t], ROW_TILES, stride=SUBLANES), :] = row
            xs_ref[pl.ds(a1_ref[base + t], ROW_TILES, stride=SUBLANES), :] = row
        return carry

    lax.fori_loop(0, TD // LOOP_TOKENS, body, 0)


def _dispatch_call(a0, a1, h2p_2d):
    steps = ROUTE_CHUNK // TD
    return pl.pallas_call(
        _dispatch_kernel,
        grid_spec=pltpu.PrefetchScalarGridSpec(
            num_scalar_prefetch=2,
            grid=(N_CHUNKS, steps),
            in_specs=[pl.BlockSpec((TD * ROW_TILES, LANES), lambda c, s, a0, a1: (c * steps + s, 0))],
            out_specs=pl.BlockSpec((CHUNK_ROWS * ROW_TILES, LANES), lambda c, s, a0, a1: (c, 0)),
        ),
        out_shape=jax.ShapeDtypeStruct((TOTAL_ROWS * ROW_TILES, LANES), U32),
        compiler_params=pltpu.CompilerParams(
            dimension_semantics=("arbitrary", "arbitrary"), vmem_limit_bytes=VMEM_LIMIT),
        name="dispatch",
    )(a0, a1, h2p_2d)


def _expert_kernel(tb_ref, gst_ref, xs_hbm, wg_ref, wu_ref, wd_ref, ys_hbm,
                   wg_s, wu_s, wd_s, xbuf, ybuf, sem_in, sem_out):
    e = pl.program_id(0)
    g_first = gst_ref[e]
    g_end = gst_ref[e + 1]
    g_all = gst_ref[N_EXPERTS]

    def in_copy(q, t):
        rows = pl.ds(tb_ref[GROUP * q + t] * ROW_GROUPS, ROW_GROUPS)
        return pltpu.make_async_copy(xs_hbm.at[rows], xbuf.at[q % 2, t], sem_in.at[q % 2, t])

    def out_copy(q, t):
        rows = pl.ds(tb_ref[GROUP * q + t] * ROW_GROUPS, ROW_GROUPS)
        return pltpu.make_async_copy(ybuf.at[q % 2, t], ys_hbm.at[rows], sem_out.at[q % 2, t])

    def for_group(q, action, make):
        for t in range(GROUP):
            @pl.when(tb_ref[GROUP * q + t] >= 0)
            def _():
                if action == "start":
                    make(q, t).start(priority=1)
                else:
                    make(q, t).wait()

    @pl.when(e == 0)
    def _():
        xbuf[...] = jnp.zeros(xbuf.shape, U32)

        @pl.when(g_all > 0)
        def _():
            for_group(0, "start", in_copy)

    wg_s[...] = wg_ref[0, 0].astype(BF16)
    wu_s[...] = wu_ref[0, 0].astype(BF16)
    wd_s[...] = wd_ref[0, 0].astype(BF16)

    def compute(slot, tiles):
        los, his = [], []
        for j in range(ROW_TILES):
            words = jnp.concatenate(
                [xbuf[slot, t, :, j, :, :].reshape(ROW_TILE, LANES) for t in tiles], axis=0)
            lo, hi = _unpack_halves(words)
            los.append(lo)
            his.append(hi)
        xb = jnp.concatenate(los + his, axis=-1).astype(BF16)
        g = jnp.dot(xb, wg_s[...], preferred_element_type=F32)
        u = jnp.dot(xb, wu_s[...], preferred_element_type=F32)
        act = (g * (1.0 / (1.0 + jnp.exp(-g))) * u).astype(BF16)
        y = jnp.dot(act, wd_s[...], preferred_element_type=F32)
        packed = _pack_halves(y.astype(BF16).astype(F32))
        for n, t in enumerate(tiles):
            for j in range(ROW_TILES):
                ybuf[slot, t, :, j, :, :] = packed[n * ROW_TILE:(n + 1) * ROW_TILE,
                                                   j * LANES:(j + 1) * LANES].reshape(
                    ROW_GROUPS, SUBLANES, LANES)

    def body(q, carry):
        slot = q % 2
        for_group(q, "wait", in_copy)

        @pl.when(q + 1 < g_all)
        def _():
            for_group(q + 1, "start", in_copy)

        @pl.when(q >= 2)
        def _():
            for_group(q - 2, "wait", out_copy)

        compute(slot, (0, 1, 2, 3))
        for_group(q, "start", out_copy)
        return carry

    lax.fori_loop(g_first, g_end, body, 0)

    @pl.when(e == N_EXPERTS - 1)
    def _():
        @pl.when(g_all >= 2)
        def _():
            for_group(g_all - 2, "wait", out_copy)

        @pl.when(g_all >= 1)
        def _():
            for_group(g_all - 1, "wait", out_copy)


def _expert_call(layer, group_tiles, group_start, xs4, wg, wu, wd):
    group_buf = (2, GROUP, ROW_GROUPS, ROW_TILES, SUBLANES, LANES)
    return pl.pallas_call(
        _expert_kernel,
        grid_spec=pltpu.PrefetchScalarGridSpec(
            num_scalar_prefetch=2,
            grid=(N_EXPERTS,),
            in_specs=[
                pl.BlockSpec(memory_space=pl.ANY),
                pl.BlockSpec((1, 1, D_MODEL, D_EXPERT), lambda e, tb, gs: (layer, e, 0, 0)),
                pl.BlockSpec((1, 1, D_MODEL, D_EXPERT), lambda e, tb, gs: (layer, e, 0, 0)),
                pl.BlockSpec((1, 1, D_EXPERT, D_MODEL), lambda e, tb, gs: (layer, e, 0, 0)),
            ],
            out_specs=pl.BlockSpec(memory_space=pl.ANY),
            scratch_shapes=[
                pltpu.VMEM((D_MODEL, D_EXPERT), BF16),
                pltpu.VMEM((D_MODEL, D_EXPERT), BF16),
                pltpu.VMEM((D_EXPERT, D_MODEL), BF16),
                pltpu.VMEM(group_buf, U32),
                pltpu.VMEM(group_buf, U32),
                pltpu.SemaphoreType.DMA((2, GROUP)),
                pltpu.SemaphoreType.DMA((2, GROUP)),
            ],
        ),
        out_shape=jax.ShapeDtypeStruct((TOTAL_ROWS // SUBLANES, ROW_TILES, SUBLANES, LANES), U32),
        input_output_aliases={2: 0},
        compiler_params=pltpu.CompilerParams(
            dimension_semantics=("arbitrary",), vmem_limit_bytes=VMEM_LIMIT),
        name="experts",
    )(group_tiles, group_start, xs4, wg, wu, wd)


def _combine_kernel(a0_ref, a1_ref, ys_ref, x1_ref, w_ref, ada_ref, fg_ref, out_ref, r0_scr, r1_scr,
                    *, final):
    c = pl.program_id(0)
    s = pl.program_id(1)
    base = c * ROUTE_CHUNK + s * TC

    def body(it, carry):
        for k in range(LOOP_TOKENS):
            t = base + it * LOOP_TOKENS + k
            dst = pl.ds(it * (LOOP_TOKENS * ROW_TILES) + _row_addr(k), ROW_TILES, stride=SUBLANES)
            r0_scr[dst, :] = ys_ref[pl.ds(a0_ref[t], ROW_TILES, stride=SUBLANES), :]
            r1_scr[dst, :] = ys_ref[pl.ds(a1_ref[t], ROW_TILES, stride=SUBLANES), :]
        return carry

    lax.fori_loop(0, TC // LOOP_TOKENS, body, 0)

    w = w_ref[...]
    w0 = w[:, 0:1]
    w1 = w[:, 1:2]
    g2 = ada_ref[0][5:6]
    n_grp = TC // SUBLANES
    los, his = [], []
    for j in range(ROW_TILES):
        parts0 = [r0_scr[pl.ds((tb * ROW_TILES + j) * SUBLANES, SUBLANES), :] for tb in range(n_grp)]
        parts1 = [r1_scr[pl.ds((tb * ROW_TILES + j) * SUBLANES, SUBLANES), :] for tb in range(n_grp)]
        lo0, hi0 = _unpack_halves(jnp.concatenate(parts0, axis=0))
        lo1, hi1 = _unpack_halves(jnp.concatenate(parts1, axis=0))
        los.append(w0 * lo0 + w1 * lo1)
        his.append(w0 * hi0 + w1 * hi1)
    moe = jnp.concatenate(los + his, axis=-1)
    x2 = x1_ref[...] + g2 * moe
    if final:
        r = lax.rsqrt(jnp.mean(x2 * x2, axis=-1, keepdims=True) + EPS)
        x2 = x2 * r * fg_ref[...]
    out_ref[...] = x2


def _combine_call(a0, a1, ys_2d, x1, wcol, ada_l, final_g, final):
    steps = ROUTE_CHUNK // TC
    tok = lambda c, s, a0, a1: (c * steps + s, 0)
    return pl.pallas_call(
        functools.partial(_combine_kernel, final=final),
        grid_spec=pltpu.PrefetchScalarGridSpec(
            num_scalar_prefetch=2,
            grid=(N_CHUNKS, steps),
            in_specs=[
                pl.BlockSpec((CHUNK_ROWS * ROW_TILES, LANES), lambda c, s, a0, a1: (c, 0)),
                pl.BlockSpec((TC, D_MODEL), tok),
                pl.BlockSpec((TC, 2), tok),
                pl.BlockSpec((1, N_ADA, D_MODEL),
                             lambda c, s, a0, a1: ((c * ROUTE_CHUNK + s * TC) // SEQ, 0, 0)),
                pl.BlockSpec((1, D_MODEL), lambda c, s, a0, a1: (0, 0)),
            ],
            out_specs=pl.BlockSpec((TC, D_MODEL), tok),
            scratch_shapes=[pltpu.VMEM((TC * ROW_TILES, LANES), U32),
                            pltpu.VMEM((TC * ROW_TILES, LANES), U32)],
        ),
        out_shape=jax.ShapeDtypeStruct((N_TOK, D_MODEL), F32),
        compiler_params=pltpu.CompilerParams(
            dimension_semantics=("arbitrary", "arbitrary"), vmem_limit_bytes=VMEM_LIMIT),
        name="combine_final" if final else "combine",
    )(a0, a1, ys_2d, x1, wcol, ada_l, final_g)


def _routing_tables(rt, cnt):
    e0 = rt[0].astype(I32)
    e1 = rt[1].astype(I32)
    rank0 = rt[4].astype(I32)
    rank1 = rt[5].astype(I32)
    counts = cnt[TILES_PER_CHUNK - 1::TILES_PER_CHUNK, :, 0].astype(I32)
    n_tiles = (counts + ROW_TILE - 1) // ROW_TILE
    tile_off = jnp.cumsum(n_tiles, axis=1) - n_tiles
    row_off = tile_off * ROW_TILE
    experts = jnp.arange(N_EXPERTS, dtype=I32)[None, :, None]
    off_tok = row_off[:, :, None]
    pick = lambda e: jnp.sum(
        jnp.where(e.reshape(N_CHUNKS, 1, ROUTE_CHUNK) == experts, off_tok, 0), axis=1).reshape(N_TOK)
    slot0 = pick(e0) + rank0
    slot1 = pick(e1) + rank1
    a0 = _row_addr(slot0)
    a1 = _row_addr(slot1)

    nt_flat = n_tiles.T.reshape(-1)
    ends = jnp.cumsum(nt_flat)
    starts = ends - nt_flat
    total = ends[-1]
    step = jnp.minimum(jnp.arange(MAX_ROW_TILES, dtype=I32), total - 1)
    seg = jnp.sum((ends[None, :] <= step[:, None]).astype(I32), axis=1)
    in_seg = seg[:, None] == jnp.arange(N_EXPERTS * N_CHUNKS, dtype=I32)[None, :]
    k = step - jnp.sum(jnp.where(in_seg, starts[None, :], 0), axis=1)
    c = seg % N_CHUNKS
    tile_off_flat = tile_off.T.reshape(-1)
    tile_blk = c * (CHUNK_ROWS // ROW_TILE) + jnp.sum(jnp.where(in_seg, tile_off_flat[None, :], 0), axis=1) + k
    e_count = jnp.sum(n_tiles, axis=0)
    e_start = jnp.cumsum(e_count) - e_count
    g_count = (e_count + GROUP - 1) // GROUP
    g_end = jnp.cumsum(g_count)
    g_start = g_end - g_count
    slot_id = jnp.arange(MAX_GROUPS * GROUP, dtype=I32)
    q = slot_id // GROUP
    owner = jnp.sum((g_end[None, :] <= q[:, None]).astype(I32), axis=1)
    is_owner = owner[:, None] == jnp.arange(N_EXPERTS, dtype=I32)[None, :]
    sel = lambda v: jnp.sum(jnp.where(is_owner, v[None, :], 0), axis=1)
    k = (q - sel(g_start)) * GROUP + slot_id % GROUP
    valid = (owner < N_EXPERTS) & (k < sel(e_count))
    flat = sel(e_start) + k
    pick_tile = flat[:, None] == jnp.arange(MAX_ROW_TILES, dtype=I32)[None, :]
    group_tiles = jnp.where(valid, jnp.sum(jnp.where(pick_tile, tile_blk[None, :], 0), axis=1), -1)
    group_start = jnp.concatenate([g_start, g_end[-1:]])
    return a0, a1, group_tiles.astype(I32), group_start.astype(I32)


def _layer_consts(l, norm1_g, norm2_g, w_in, conv_w, conv_b, sp_norm_g, sp_w, sp_b, mix_out_g,
                  w_out, shared):
    pos = jnp.arange(SP_BLOCK)
    mask = (pos[None, :] // CHUNK) <= (pos[:, None] // CHUNK)
    w = jnp.where(mask[None], sp_w[l], 0.0)
    w_pairs = jnp.concatenate([w[0::2], w[1::2]], axis=-1).astype(BF16)
    sp_bias = jnp.repeat(sp_b[l].T, HEAD_DIM, axis=1)
    return (
        norm1_g[l].reshape(1, D_MODEL), norm2_g[l].reshape(1, D_MODEL),
        w_in[l].astype(BF16),
        conv_w[l], conv_b[l].reshape(1, D_CONV), sp_norm_g[l].reshape(1, D_SPATIAL),
        w_pairs, sp_bias,
        mix_out_g[l].reshape(1, D_MODEL),
        w_out[l].astype(BF16),
    ) + shared


def kernel(x, c, w_ada, b_ada, norm1_g, norm2_g, w_in, conv_w, conv_b, sp_norm_g, sp_w, sp_b, mix_out_g, w_out, router_w, router_b, exp_w_gate, exp_w_up, exp_w_down, final_g):
    ada = _ada_table(c, w_ada, b_ada).reshape(DEPTH, BATCH, N_ADA, D_MODEL)

    rw_hi = router_w.astype(BF16)
    rw_lo = (router_w - rw_hi.astype(F32)).astype(BF16)
    rw_cat = jnp.pad(jnp.concatenate([rw_hi, rw_lo], axis=1), ((0, 0), (0, LANES - 2 * N_EXPERTS)))
    head_of = jnp.arange(D_MODEL) // HEAD_DIM
    hred = (head_of[:, None] == jnp.arange(LANES)[None, :]).astype(F32) / HEAD_DIM
    hexp_half = (jnp.arange(LANES)[:, None] == head_of[None, :]).astype(F32)
    hexp = jnp.concatenate([hexp_half, hexp_half], axis=0)
    tri = (jnp.arange(SB)[:, None] < jnp.arange(SB)[None, :])
    shared = (rw_cat, router_b.reshape(N_EXPERTS, 1), hred.astype(BF16), hexp.astype(BF16),
              tri.astype(BF16))
    fg = final_g.reshape(1, D_MODEL)

    xc = x.reshape(N_TOK, D_MODEL)
    for l in range(DEPTH):
        consts = _layer_consts(l, norm1_g, norm2_g, w_in, conv_w, conv_b, sp_norm_g, sp_w, sp_b,
                               mix_out_g, w_out, shared)
        x1, h2p, rt, cnt = _mix_call(xc, ada[l], consts)
        a0, a1, group_tiles, group_start = _routing_tables(rt, cnt)
        xs = _dispatch_call(a0, a1, h2p.reshape(N_TOK * ROW_TILES, LANES))
        ys = _expert_call(l, group_tiles, group_start,
                          xs.reshape(TOTAL_ROWS // SUBLANES, ROW_TILES, SUBLANES, LANES),
                          exp_w_gate, exp_w_up, exp_w_down)
        wcol = rt[2:4].T
        xc = _combine_call(a0, a1, ys.reshape(TOTAL_ROWS * ROW_TILES, LANES), x1, wcol, ada[l], fg,
                           final=(l == DEPTH - 1))
    return xc.reshape(BATCH, SEQ, D_MODEL)
```

```python
import functools

import jax
import jax.numpy as jnp
from jax import lax
from jax.experimental import pallas as pl
from jax.experimental.pallas import tpu as pltpu

F32 = jnp.float32
BF16 = jnp.bfloat16
U32 = jnp.uint32
I32 = jnp.int32

D_MODEL = 1024
BATCH = 8
SEQ = 2048
DEPTH = 4
N_TOK = BATCH * SEQ
CHUNK = 64
HEAD_DIM = 64
D_CONV = 512
D_SPATIAL = 512
N_SP_HEADS = 8
N_MIX_HEADS = 16
SP_BLOCK = 128
N_EXPERTS = 16
N_GROUPS = 4
PER_GROUP = 4
D_EXPERT = 512
N_ADA = 6
EPS = 1e-6

LANES = 128
SUBLANES = 8
HALF = D_MODEL // 2
ROW_TILES = HALF // LANES

TM = 1024
SB = 256
TILES_PER_SEQ = SEQ // TM
ROUTE_CHUNK = 4096
N_CHUNKS = N_TOK // ROUTE_CHUNK
TILES_PER_CHUNK = ROUTE_CHUNK // TM
ROW_TILE = 128
ROW_GROUPS = ROW_TILE // SUBLANES
GROUP = 4
CHUNK_ROWS = 2 * ROUTE_CHUNK + N_EXPERTS * ROW_TILE
TOTAL_ROWS = N_CHUNKS * CHUNK_ROWS
MAX_ROW_TILES = TOTAL_ROWS // ROW_TILE
MAX_GROUPS = MAX_ROW_TILES // GROUP + N_EXPERTS
LOOP_TOKENS = 16
TD = 1024
TC = 512
COMBINE_STEPS = ROUTE_CHUNK // TC
VMEM_LIMIT = 58 * 1024 * 1024


def _gelu_tanh(x):
    c = 0.7978845608028654
    return 0.5 * x * (1.0 + jnp.tanh(c * (x + 0.044715 * (x * x * x))))


def _split_bf16(x):
    hi = x.astype(BF16)
    lo = (x - hi.astype(F32)).astype(BF16)
    return hi, lo


def _pack_halves(x):
    return pltpu.pack_elementwise([x[:, :HALF], x[:, HALF:]], packed_dtype=BF16)


def _unpack_halves(words):
    lo = pltpu.unpack_elementwise(words, index=0, packed_dtype=BF16, unpacked_dtype=F32)
    hi = pltpu.unpack_elementwise(words, index=1, packed_dtype=BF16, unpacked_dtype=F32)
    return lo, hi


def _ada_kernel(c_ref, w_ref, b_ref, o_ref):
    c = c_ref[...]
    ca = c * (1.0 / (1.0 + jnp.exp(-c)))
    c_hi, c_lo = _split_bf16(ca)
    w_hi, w_lo = _split_bf16(w_ref[0])
    acc = jnp.dot(c_hi, w_hi, preferred_element_type=F32)
    acc += jnp.dot(c_lo, w_hi, preferred_element_type=F32)
    acc += jnp.dot(c_hi, w_lo, preferred_element_type=F32)
    o_ref[0] = acc + b_ref[0]


def _ada_table(c, w_ada, b_ada):
    n_blk = N_ADA
    return pl.pallas_call(
        _ada_kernel,
        grid=(DEPTH, n_blk),
        in_specs=[
            pl.BlockSpec((BATCH, D_MODEL), lambda l, n: (0, 0)),
            pl.BlockSpec((1, D_MODEL, D_MODEL), lambda l, n: (l, 0, n)),
            pl.BlockSpec((1, 1, D_MODEL), lambda l, n: (l, 0, n)),
        ],
        out_specs=pl.BlockSpec((1, BATCH, D_MODEL), lambda l, n: (l, 0, n)),
        out_shape=jax.ShapeDtypeStruct((DEPTH, BATCH, N_ADA * D_MODEL), F32),
        name="ada_table",
    )(c, w_ada, b_ada.reshape(DEPTH, 1, N_ADA * D_MODEL))


def _mix_kernel(x_ref, ada_ref, n1g_ref, n2g_ref, win_ref, cw_ref, cb_ref, spg_ref, spw_ref,
                spbias_ref, mog_ref, wout_ref, rw_ref, rb_ref, hred_ref, hexp_ref,
                tri_ref, x1_ref, h2p_ref, rt_ref, cnt_ref, z_scr, cnt_scr):
    i = pl.program_id(0)

    @pl.when(i % TILES_PER_SEQ == 0)
    def _():
        z_scr[pl.ds(0, SUBLANES), :] = jnp.zeros((SUBLANES, D_CONV), F32)

    @pl.when(i % TILES_PER_CHUNK == 0)
    def _():
        cnt_scr[...] = jnp.zeros((N_EXPERTS, LANES), F32)

    chains = [
        _mix_rows(sb * SB, x_ref, ada_ref, n1g_ref, n2g_ref, win_ref, cw_ref, cb_ref, spg_ref,
                  spw_ref, spbias_ref, mog_ref, wout_ref, rw_ref, rb_ref, hred_ref,
                  hexp_ref, tri_ref, x1_ref, h2p_ref, rt_ref, z_scr, cnt_scr)
        for sb in range(TM // SB)]
    for c in MIX_STAGE_ORDER:
        next(chains[c])
    z_scr[pl.ds(0, SUBLANES), :] = z_scr[pl.ds(TM, SUBLANES), :]
    cnt_ref[0] = cnt_scr[...]


MIX_STAGES = 8
MIX_SKEW = 2
MIX_STAGE_ORDER = tuple(
    c for t in range(MIX_STAGES + MIX_SKEW * (TM // SB - 1)) for c in range(TM // SB)
    if 0 <= t - MIX_SKEW * c < MIX_STAGES)


def _mix_rows(r0, x_ref, ada_ref, n1g_ref, n2g_ref, win_ref, cw_ref, cb_ref, spg_ref, spw_ref,
              spbias_ref, mog_ref, wout_ref, rw_ref, rb_ref, hred_ref, hexp_ref,
              tri_ref, x1_ref, h2p_ref, rt_ref, z_scr, cnt_scr):
    x = x_ref[pl.ds(r0, SB), :]
    ada = ada_ref[0]
    sh1, sc1, g1 = ada[0:1], ada[1:2], ada[2:3]
    sh2, sc2 = ada[3:4], ada[4:5]

    rinv = lax.rsqrt(jnp.mean(x * x, axis=-1, keepdims=True) + EPS)
    h = (x * rinv) * (n1g_ref[...] * (1.0 + sc1)) + sh1
    hb = h.astype(BF16)
    yield

    def proj(k):
        return jnp.dot(hb, win_ref[:, k * 512:(k + 1) * 512], preferred_element_type=F32)

    z = proj(1) * proj(2)
    z_scr[pl.ds(SUBLANES + r0, SB), :] = z
    z1 = z_scr[pl.ds(SUBLANES + r0 - 1, SB), :]
    z2 = z_scr[pl.ds(SUBLANES + r0 - 2, SB), :]
    yield
    p_b, p_u, p_v = proj(0), proj(3), proj(4)
    yield
    cw = cw_ref[...]
    acc = cb_ref[...] + cw[0:1] * z2
    acc = acc + cw[1:2] * z1
    acc = acc + cw[2:3] * z
    y_conv = p_b * acc

    ug = _gelu_tanh(p_u)
    vg = _gelu_tanh(p_v)
    vr = lax.rsqrt(jnp.mean(vg * vg, axis=-1, keepdims=True) + EPS)
    vn = (vg * vr * spg_ref[...]).astype(BF16)
    lane = lax.broadcasted_iota(I32, (1, LANES), 1)
    first_head = lane < HEAD_DIM
    zero = jnp.zeros((), BF16)
    s_rows = []
    for blk in range(SB // SP_BLOCK):
        cols = []
        for p in range(N_SP_HEADS // 2):
            vv = vn[blk * SP_BLOCK:(blk + 1) * SP_BLOCK, p * LANES:(p + 1) * LANES]
            rhs = jnp.concatenate([jnp.where(first_head, vv, zero), jnp.where(first_head, zero, vv)], axis=0)
            cols.append(jnp.dot(spw_ref[p], rhs, preferred_element_type=F32))
        s_rows.append(jnp.concatenate(cols, axis=-1) + spbias_ref[...])
    y_sp = ug * jnp.concatenate(s_rows, axis=0)
    yield

    y = jnp.concatenate([y_conv, y_sp], axis=-1)
    ms = jnp.dot((y * y).astype(BF16), hred_ref[...], preferred_element_type=F32)
    r_hi, r_lo = _split_bf16(lax.rsqrt(ms + EPS))
    scale = jnp.dot(jnp.concatenate([r_hi, r_lo], axis=-1), hexp_ref[...], preferred_element_type=F32)
    yn = (y * scale * mog_ref[...]).astype(BF16)
    yield
    x1 = x + g1 * jnp.dot(yn, wout_ref[...], preferred_element_type=F32)
    x1_ref[pl.ds(r0, SB), :] = x1
    yield

    r2 = lax.rsqrt(jnp.mean(x1 * x1, axis=-1, keepdims=True) + EPS)
    h2 = (x1 * r2) * (n2g_ref[...] * (1.0 + sc2)) + sh2
    h2_hi = h2.astype(BF16)
    packed = _pack_halves(h2_hi.astype(F32))
    for j in range(ROW_TILES):
        h2p_ref[pl.ds(r0 // SUBLANES, SB // SUBLANES), j, :, :] = (
            packed[:, j * LANES:(j + 1) * LANES].reshape(SB // SUBLANES, SUBLANES, LANES))
    yield

    logits_t = jnp.dot(h2_hi, rw_ref[...], preferred_element_type=F32).T
    lg = logits_t[0:N_EXPERTS, :] + logits_t[N_EXPERTS:2 * N_EXPERTS, :]
    scores = 1.0 / (1.0 + jnp.exp(-lg))
    sel = scores + rb_ref[...]

    gs = []
    for g in range(N_GROUPS):
        a, b, c, d = (sel[g * PER_GROUP + k:g * PER_GROUP + k + 1, :] for k in range(PER_GROUP))
        hi1, lo1 = jnp.maximum(a, b), jnp.minimum(a, b)
        hi2, lo2 = jnp.maximum(c, d), jnp.minimum(c, d)
        gs.append(jnp.maximum(hi1, hi2) + jnp.maximum(jnp.minimum(hi1, hi2), jnp.maximum(lo1, lo2)))
    best = gs[0]
    top_group = jnp.zeros((1, SB), I32)
    for g in range(1, N_GROUPS):
        better = gs[g] > best
        best = jnp.where(better, gs[g], best)
        top_group = jnp.where(better, g, top_group)

    eid_i = lax.broadcasted_iota(I32, (N_EXPERTS, SB), 0)
    eid = eid_i.astype(F32)
    neg_inf = jnp.float32(-jnp.inf)
    masked = jnp.where((eid_i // PER_GROUP) == top_group, sel, neg_inf)
    m1 = jnp.max(masked, axis=0, keepdims=True)
    i1 = jnp.min(jnp.where(masked == m1, eid, float(N_EXPERTS)), axis=0, keepdims=True)
    masked2 = jnp.where(eid == i1, neg_inf, masked)
    m2 = jnp.max(masked2, axis=0, keepdims=True)
    i2 = jnp.min(jnp.where(masked2 == m2, eid, float(N_EXPERTS)), axis=0, keepdims=True)
    pick1 = eid == i1
    pick2 = eid == i2
    w1 = jnp.sum(jnp.where(pick1, scores, 0.0), axis=0, keepdims=True)
    w2 = jnp.sum(jnp.where(pick2, scores, 0.0), axis=0, keepdims=True)
    wsum = w1 + w2
    c1 = w1 / wsum
    c2 = w2 / wsum

    assigned = jnp.where(pick1 | pick2, 1.0, 0.0)
    before = jnp.dot(assigned.astype(BF16), tri_ref[...], preferred_element_type=F32)
    carry = cnt_scr[...]
    rank = before + carry[:, 0:1]
    rank1 = jnp.sum(jnp.where(pick1, rank, 0.0), axis=0, keepdims=True)
    rank2 = jnp.sum(jnp.where(pick2, rank, 0.0), axis=0, keepdims=True)
    cnt_scr[...] = carry + jnp.sum(assigned, axis=1, keepdims=True)
    unused = jnp.zeros((1, SB), F32)
    for r, row in enumerate((i1, i2, c1, c2, rank1, rank2, unused, unused)):
        rt_ref[r:r + 1, pl.ds(r0, SB)] = row
    yield


def _mix_call(x2d, ada_l, consts):
    n_tiles = N_TOK // TM
    full = lambda shape: pl.BlockSpec(shape, lambda i: (0,) * len(shape))
    in_specs = [
        pl.BlockSpec((TM, D_MODEL), lambda i: (i, 0)),
        pl.BlockSpec((1, N_ADA, D_MODEL), lambda i: (i // TILES_PER_SEQ, 0, 0)),
        full((1, D_MODEL)), full((1, D_MODEL)),
        full((D_MODEL, 5 * 512)),
        full((3, D_CONV)), full((1, D_CONV)), full((1, D_SPATIAL)),
        full((N_SP_HEADS // 2, SP_BLOCK, 2 * SP_BLOCK)),
        full((SP_BLOCK, D_SPATIAL)),
        full((1, D_MODEL)),
        full((D_MODEL, D_MODEL)),
        full((D_MODEL, LANES)), full((N_EXPERTS, 1)),
        full((D_MODEL, LANES)), full((2 * LANES, D_MODEL)),
        full((SB, SB)),
    ]
    out_specs = [
        pl.BlockSpec((TM, D_MODEL), lambda i: (i, 0)),
        pl.BlockSpec((TM // SUBLANES, ROW_TILES, SUBLANES, LANES), lambda i: (i, 0, 0, 0)),
        pl.BlockSpec((SUBLANES, TM), lambda i: (0, i)),
        pl.BlockSpec((1, N_EXPERTS, LANES), lambda i: (i, 0, 0)),
    ]
    out_shape = [
        jax.ShapeDtypeStruct((N_TOK, D_MODEL), F32),
        jax.ShapeDtypeStruct((N_TOK // SUBLANES, ROW_TILES, SUBLANES, LANES), U32),
        jax.ShapeDtypeStruct((SUBLANES, N_TOK), F32),
        jax.ShapeDtypeStruct((n_tiles, N_EXPERTS, LANES), F32),
    ]
    return pl.pallas_call(
        _mix_kernel,
        grid=(n_tiles,),
        in_specs=in_specs,
        out_specs=out_specs,
        out_shape=out_shape,
        scratch_shapes=[pltpu.VMEM((TM + SUBLANES, D_CONV), F32), pltpu.VMEM((N_EXPERTS, LANES), F32)],
        compiler_params=pltpu.CompilerParams(
            dimension_semantics=("arbitrary",), vmem_limit_bytes=VMEM_LIMIT),
        name="mix",
    )(x2d, ada_l, *consts)


def _row_addr(t):
    return (t // SUBLANES) * (ROW_TILES * SUBLANES) + t % SUBLANES


def _dispatch_kernel(a0_ref, a1_ref, h2p_ref, xs_ref):
    c = pl.program_id(0)
    s = pl.program_id(1)

    @pl.when(s == 0)
    def _():
        xs_ref[...] = jnp.zeros(xs_ref.shape, U32)

    base = c * ROUTE_CHUNK + s * TD

    def body(it, carry):
        for k in range(LOOP_TOKENS):
            t = it * LOOP_TOKENS + k
            src = it * (LOOP_TOKENS * ROW_TILES) + _row_addr(k)
            row = h2p_ref[pl.ds(src, ROW_TILES, stride=SUBLANES), :]
            xs_ref[pl.ds(a0_ref[base + t], ROW_TILES, stride=SUBLANES), :] = row
            xs_ref[pl.ds(a1_ref[base + t], ROW_TILES, stride=SUBLANES), :] = row
        return carry

    lax.fori_loop(0, TD // LOOP_TOKENS, body, 0)


def _dispatch_call(a0, a1, h2p_2d):
    steps = ROUTE_CHUNK // TD
    return pl.pallas_call(
        _dispatch_kernel,
        grid_spec=pltpu.PrefetchScalarGridSpec(
            num_scalar_prefetch=2,
            grid=(N_CHUNKS, steps),
            in_specs=[pl.BlockSpec((TD * ROW_TILES, LANES), lambda c, s, a0, a1: (c * steps + s, 0))],
            out_specs=pl.BlockSpec((CHUNK_ROWS * ROW_TILES, LANES), lambda c, s, a0, a1: (c, 0)),
        ),
        out_shape=jax.ShapeDtypeStruct((TOTAL_ROWS * ROW_TILES, LANES), U32),
        compiler_params=pltpu.CompilerParams(
            dimension_semantics=("arbitrary", "arbitrary"), vmem_limit_bytes=VMEM_LIMIT),
        name="dispatch",
    )(a0, a1, h2p_2d)


def _expert_kernel(tb_ref, gst_ref, xs_hbm, wg_ref, wu_ref, wd_ref, ys_hbm,
                   wg_s, wu_s, wd_s, xbuf, ybuf, sem_in, sem_out):
    e = pl.program_id(0)
    g_first = gst_ref[e]
    g_end = gst_ref[e + 1]
    g_all = gst_ref[N_EXPERTS]

    def in_copy(q, t):
        rows = pl.ds(tb_ref[GROUP * q + t] * ROW_GROUPS, ROW_GROUPS)
        return pltpu.make_async_copy(xs_hbm.at[rows], xbuf.at[q % 2, t], sem_in.at[q % 2, t])

    def out_copy(q, t):
        rows = pl.ds(tb_ref[GROUP * q + t] * ROW_GROUPS, ROW_GROUPS)
        return pltpu.make_async_copy(ybuf.at[q % 2, t], ys_hbm.at[rows], sem_out.at[q % 2, t])

    def for_group(q, action, make):
        for t in range(GROUP):
            @pl.when(tb_ref[GROUP * q + t] >= 0)
            def _():
                if action == "start":
                    make(q, t).start(priority=1)
                else:
                    make(q, t).wait()

    @pl.when(e == 0)
    def _():
        xbuf[...] = jnp.zeros(xbuf.shape, U32)

        @pl.when(g_all > 0)
        def _():
            for_group(0, "start", in_copy)

    wg_s[...] = wg_ref[0, 0].astype(BF16)
    wu_s[...] = wu_ref[0, 0].astype(BF16)
    wd_s[...] = wd_ref[0, 0].astype(BF16)

    def compute(slot, tiles):
        los, his = [], []
        for j in range(ROW_TILES):
            words = jnp.concatenate(
                [xbuf[slot, t, :, j, :, :].reshape(ROW_TILE, LANES) for t in tiles], axis=0)
            lo, hi = _unpack_halves(words)
            los.append(lo)
            his.append(hi)
        xb = jnp.concatenate(los + his, axis=-1).astype(BF16)
        g = jnp.dot(xb, wg_s[...], preferred_element_type=F32)
        u = jnp.dot(xb, wu_s[...], preferred_element_type=F32)
        act = (g * (1.0 / (1.0 + jnp.exp(-g))) * u).astype(BF16)
        y = jnp.dot(act, wd_s[...], preferred_element_type=F32)
        packed = _pack_halves(y.astype(BF16).astype(F32))
        for n, t in enumerate(tiles):
            for j in range(ROW_TILES):
                ybuf[slot, t, :, j, :, :] = packed[n * ROW_TILE:(n + 1) * ROW_TILE,
                                                   j * LANES:(j + 1) * LANES].reshape(
                    ROW_GROUPS, SUBLANES, LANES)

    def body(q, carry):
        slot = q % 2
        for_group(q, "wait", in_copy)

        @pl.when(q + 1 < g_all)
        def _():
            for_group(q + 1, "start", in_copy)

        @pl.when(q >= 2)
        def _():
            for_group(q - 2, "wait", out_copy)

        compute(slot, (0, 1, 2, 3))
        for_group(q, "start", out_copy)
        return carry

    lax.fori_loop(g_first, g_end, body, 0)

    @pl.when(e == N_EXPERTS - 1)
    def _():
        @pl.when(g_all >= 2)
        def _():
            for_group(g_all - 2, "wait", out_copy)

        @pl.when(g_all >= 1)
        def _():
            for_group(g_all - 1, "wait", out_copy)


def _expert_call(layer, group_tiles, group_start, xs4, wg, wu, wd):
    group_buf = (2, GROUP, ROW_GROUPS, ROW_TILES, SUBLANES, LANES)
    return pl.pallas_call(
        _expert_kernel,
        grid_spec=pltpu.PrefetchScalarGridSpec(
            num_scalar_prefetch=2,
            grid=(N_EXPERTS,),
            in_specs=[
                pl.BlockSpec(memory_space=pl.ANY),
                pl.BlockSpec((1, 1, D_MODEL, D_EXPERT), lambda e, tb, gs: (layer, e, 0, 0)),
                pl.BlockSpec((1, 1, D_MODEL, D_EXPERT), lambda e, tb, gs: (layer, e, 0, 0)),
                pl.BlockSpec((1, 1, D_EXPERT, D_MODEL), lambda e, tb, gs: (layer, e, 0, 0)),
            ],
            out_specs=pl.BlockSpec(memory_space=pl.ANY),
            scratch_shapes=[
                pltpu.VMEM((D_MODEL, D_EXPERT), BF16),
                pltpu.VMEM((D_MODEL, D_EXPERT), BF16),
                pltpu.VMEM((D_EXPERT, D_MODEL), BF16),
                pltpu.VMEM(group_buf, U32),
                pltpu.VMEM(group_buf, U32),
                pltpu.SemaphoreType.DMA((2, GROUP)),
                pltpu.SemaphoreType.DMA((2, GROUP)),
            ],
        ),
        out_shape=jax.ShapeDtypeStruct((TOTAL_ROWS // SUBLANES, ROW_TILES, SUBLANES, LANES), U32),
        input_output_aliases={2: 0},
        compiler_params=pltpu.CompilerParams(
            dimension_semantics=("arbitrary",), vmem_limit_bytes=VMEM_LIMIT),
        name="experts",
    )(group_tiles, group_start, xs4, wg, wu, wd)


def _combine_kernel(a0_ref, a1_ref, ys_hbm, x1_ref, w_ref, ada_ref, fg_ref, out_ref,
                    ys_buf, r0_scr, r1_scr, sem, *, final):
    c = pl.program_id(0)
    s = pl.program_id(1)
    base = c * ROUTE_CHUNK + s * TC
    region = CHUNK_ROWS * ROW_TILES
    piece = region // COMBINE_STEPS

    def piece_copy(chunk, p):
        src = ys_hbm.at[pl.ds(chunk * region + p * piece, piece)]
        dst = ys_buf.at[pl.ds((chunk % 2) * region + p * piece, piece)]
        return pltpu.make_async_copy(src, dst, sem.at[chunk % 2, p])

    @pl.when((c == 0) & (s == 0))
    def _():
        for p in range(COMBINE_STEPS):
            piece_copy(0, p).start()
        for p in range(COMBINE_STEPS):
            piece_copy(0, p).wait()

    @pl.when(c + 1 < N_CHUNKS)
    def _():
        for p in range(COMBINE_STEPS):
            @pl.when(s == p)
            def _():
                piece_copy(c + 1, p).start()

    off = (c % 2) * region

    def body(it, carry):
        for k in range(LOOP_TOKENS):
            t = base + it * LOOP_TOKENS + k
            dst = pl.ds(it * (LOOP_TOKENS * ROW_TILES) + _row_addr(k), ROW_TILES, stride=SUBLANES)
            r0_scr[dst, :] = ys_buf[pl.ds(off + a0_ref[t], ROW_TILES, stride=SUBLANES), :]
            r1_scr[dst, :] = ys_buf[pl.ds(off + a1_ref[t], ROW_TILES, stride=SUBLANES), :]
        return carry

    lax.fori_loop(0, TC // LOOP_TOKENS, body, 0)

    w = w_ref[...]
    w0 = w[:, 0:1]
    w1 = w[:, 1:2]
    g2 = ada_ref[0][5:6]
    n_grp = TC // SUBLANES
    los, his = [], []
    for j in range(ROW_TILES):
        parts0 = [r0_scr[pl.ds((tb * ROW_TILES + j) * SUBLANES, SUBLANES), :] for tb in range(n_grp)]
        parts1 = [r1_scr[pl.ds((tb * ROW_TILES + j) * SUBLANES, SUBLANES), :] for tb in range(n_grp)]
        lo0, hi0 = _unpack_halves(jnp.concatenate(parts0, axis=0))
        lo1, hi1 = _unpack_halves(jnp.concatenate(parts1, axis=0))
        los.append(w0 * lo0 + w1 * lo1)
        his.append(w0 * hi0 + w1 * hi1)
    x2 = x1_ref[...] + g2 * jnp.concatenate(los + his, axis=-1)
    if final:
        r = lax.rsqrt(jnp.mean(x2 * x2, axis=-1, keepdims=True) + EPS)
        x2 = x2 * r * fg_ref[...]
    out_ref[...] = x2

    @pl.when((s == COMBINE_STEPS - 1) & (c + 1 < N_CHUNKS))
    def _():
        for p in range(COMBINE_STEPS):
            piece_copy(c + 1, p).wait()


def _combine_call(a0, a1, ys_2d, x1, wcol, ada_l, final_g, final):
    tok = lambda c, s, a0, a1: (c * COMBINE_STEPS + s, 0)
    return pl.pallas_call(
        functools.partial(_combine_kernel, final=final),
        grid_spec=pltpu.PrefetchScalarGridSpec(
            num_scalar_prefetch=2,
            grid=(N_CHUNKS, COMBINE_STEPS),
            in_specs=[
                pl.BlockSpec(memory_space=pl.ANY),
                pl.BlockSpec((TC, D_MODEL), tok),
                pl.BlockSpec((TC, 2), tok),
                pl.BlockSpec((1, N_ADA, D_MODEL),
                             lambda c, s, a0, a1: ((c * ROUTE_CHUNK + s * TC) // SEQ, 0, 0)),
                pl.BlockSpec((1, D_MODEL), lambda c, s, a0, a1: (0, 0)),
            ],
            out_specs=pl.BlockSpec((TC, D_MODEL), tok),
            scratch_shapes=[pltpu.VMEM((2 * CHUNK_ROWS * ROW_TILES, LANES), U32),
                            pltpu.VMEM((TC * ROW_TILES, LANES), U32),
                            pltpu.VMEM((TC * ROW_TILES, LANES), U32),
                            pltpu.SemaphoreType.DMA((2, COMBINE_STEPS))],
        ),
        out_shape=jax.ShapeDtypeStruct((N_TOK, D_MODEL), F32),
        compiler_params=pltpu.CompilerParams(
            dimension_semantics=("arbitrary", "arbitrary"), vmem_limit_bytes=VMEM_LIMIT),
        name="combine_final" if final else "combine",
    )(a0, a1, ys_2d, x1, wcol, ada_l, final_g)


def _routing_tables(rt, cnt):
    e0 = rt[0].astype(I32)
    e1 = rt[1].astype(I32)
    rank0 = rt[4].astype(I32)
    rank1 = rt[5].astype(I32)
    counts = cnt[TILES_PER_CHUNK - 1::TILES_PER_CHUNK, :, 0].astype(I32)
    n_tiles = (counts + ROW_TILE - 1) // ROW_TILE
    tile_off = jnp.cumsum(n_tiles, axis=1) - n_tiles
    row_off = tile_off * ROW_TILE
    experts = jnp.arange(N_EXPERTS, dtype=I32)[None, :, None]
    off_tok = row_off[:, :, None]
    pick = lambda e: jnp.sum(
        jnp.where(e.reshape(N_CHUNKS, 1, ROUTE_CHUNK) == experts, off_tok, 0), axis=1).reshape(N_TOK)
    slot0 = pick(e0) + rank0
    slot1 = pick(e1) + rank1
    a0 = _row_addr(slot0)
    a1 = _row_addr(slot1)

    nt_flat = n_tiles.T.reshape(-1)
    ends = jnp.cumsum(nt_flat)
    starts = ends - nt_flat
    total = ends[-1]
    step = jnp.minimum(jnp.arange(MAX_ROW_TILES, dtype=I32), total - 1)
    seg = jnp.sum((ends[None, :] <= step[:, None]).astype(I32), axis=1)
    in_seg = seg[:, None] == jnp.arange(N_EXPERTS * N_CHUNKS, dtype=I32)[None, :]
    k = step - jnp.sum(jnp.where(in_seg, starts[None, :], 0), axis=1)
    c = seg % N_CHUNKS
    tile_off_flat = tile_off.T.reshape(-1)
    tile_blk = c * (CHUNK_ROWS // ROW_TILE) + jnp.sum(jnp.where(in_seg, tile_off_flat[None, :], 0), axis=1) + k
    e_count = jnp.sum(n_tiles, axis=0)
    e_start = jnp.cumsum(e_count) - e_count
    g_count = (e_count + GROUP - 1) // GROUP
    g_end = jnp.cumsum(g_count)
    g_start = g_end - g_count
    slot_id = jnp.arange(MAX_GROUPS * GROUP, dtype=I32)
    q = slot_id // GROUP
    owner = jnp.sum((g_end[None, :] <= q[:, None]).astype(I32), axis=1)
    is_owner = owner[:, None] == jnp.arange(N_EXPERTS, dtype=I32)[None, :]
    sel = lambda v: jnp.sum(jnp.where(is_owner, v[None, :], 0), axis=1)
    k = (q - sel(g_start)) * GROUP + slot_id % GROUP
    valid = (owner < N_EXPERTS) & (k < sel(e_count))
    flat = sel(e_start) + k
    pick_tile = flat[:, None] == jnp.arange(MAX_ROW_TILES, dtype=I32)[None, :]
    group_tiles = jnp.where(valid, jnp.sum(jnp.where(pick_tile, tile_blk[None, :], 0), axis=1), -1)
    group_start = jnp.concatenate([g_start, g_end[-1:]])
    return a0, a1, group_tiles.astype(I32), group_start.astype(I32)


def _layer_consts(l, norm1_g, norm2_g, w_in, conv_w, conv_b, sp_norm_g, sp_w, sp_b, mix_out_g,
                  w_out, shared):
    pos = jnp.arange(SP_BLOCK)
    mask = (pos[None, :] // CHUNK) <= (pos[:, None] // CHUNK)
    w = jnp.where(mask[None], sp_w[l], 0.0)
    w_pairs = jnp.concatenate([w[0::2], w[1::2]], axis=-1).astype(BF16)
    sp_bias = jnp.repeat(sp_b[l].T, HEAD_DIM, axis=1)
    return (
        norm1_g[l].reshape(1, D_MODEL), norm2_g[l].reshape(1, D_MODEL),
        w_in[l].astype(BF16),
        conv_w[l], conv_b[l].reshape(1, D_CONV), sp_norm_g[l].reshape(1, D_SPATIAL),
        w_pairs, sp_bias,
        mix_out_g[l].reshape(1, D_MODEL),
        w_out[l].astype(BF16),
    ) + shared


def kernel(x, c, w_ada, b_ada, norm1_g, norm2_g, w_in, conv_w, conv_b, sp_norm_g, sp_w, sp_b, mix_out_g, w_out, router_w, router_b, exp_w_gate, exp_w_up, exp_w_down, final_g):
    ada = _ada_table(c, w_ada, b_ada).reshape(DEPTH, BATCH, N_ADA, D_MODEL)

    rw_hi = router_w.astype(BF16)
    rw_lo = (router_w - rw_hi.astype(F32)).astype(BF16)
    rw_cat = jnp.pad(jnp.concatenate([rw_hi, rw_lo], axis=1), ((0, 0), (0, LANES - 2 * N_EXPERTS)))
    head_of = jnp.arange(D_MODEL) // HEAD_DIM
    hred = (head_of[:, None] == jnp.arange(LANES)[None, :]).astype(F32) / HEAD_DIM
    hexp_half = (jnp.arange(LANES)[:, None] == head_of[None, :]).astype(F32)
    hexp = jnp.concatenate([hexp_half, hexp_half], axis=0)
    tri = (jnp.arange(SB)[:, None] < jnp.arange(SB)[None, :])
    shared = (rw_cat, router_b.reshape(N_EXPERTS, 1), hred.astype(BF16), hexp.astype(BF16),
              tri.astype(BF16))
    fg = final_g.reshape(1, D_MODEL)

    xc = x.reshape(N_TOK, D_MODEL)
    for l in range(DEPTH):
        consts = _layer_consts(l, norm1_g, norm2_g, w_in, conv_w, conv_b, sp_norm_g, sp_w, sp_b,
                               mix_out_g, w_out, shared)
        x1, h2p, rt, cnt = _mix_call(xc, ada[l], consts)
        a0, a1, group_tiles, group_start = _routing_tables(rt, cnt)
        xs = _dispatch_call(a0, a1, h2p.reshape(N_TOK * ROW_TILES, LANES))
        ys = _expert_call(l, group_tiles, group_start,
                          xs.reshape(TOTAL_ROWS // SUBLANES, ROW_TILES, SUBLANES, LANES),
                          exp_w_gate, exp_w_up, exp_w_down)
        wcol = rt[2:4].T
        xc = _combine_call(a0, a1, ys.reshape(TOTAL_ROWS * ROW_TILES, LANES), x1, wcol, ada[l], fg,
                           final=(l == DEPTH - 1))
    return xc.reshape(BATCH, SEQ, D_MODEL)
```

```python
import functools

import jax
import jax.numpy as jnp
from jax import lax
from jax.experimental import pallas as pl
from jax.experimental.pallas import tpu as pltpu

F32 = jnp.float32
BF16 = jnp.bfloat16
U32 = jnp.uint32
I32 = jnp.int32

D_MODEL = 1024
BATCH = 8
SEQ = 2048
DEPTH = 4
N_TOK = BATCH * SEQ
CHUNK = 64
HEAD_DIM = 64
D_CONV = 512
D_SPATIAL = 512
N_SP_HEADS = 8
N_MIX_HEADS = 16
SP_BLOCK = 128
N_EXPERTS = 16
N_GROUPS = 4
PER_GROUP = 4
D_EXPERT = 512
N_ADA = 6
EPS = 1e-6

LANES = 128
SUBLANES = 8
HALF = D_MODEL // 2
ROW_TILES = HALF // LANES

TM = 1024
SB = 256
TILES_PER_SEQ = SEQ // TM
ROUTE_CHUNK = 4096
N_CHUNKS = N_TOK // ROUTE_CHUNK
TILES_PER_CHUNK = ROUTE_CHUNK // TM
ROW_TILE = 128
ROW_GROUPS = ROW_TILE // SUBLANES
GROUP = 4
CHUNK_ROWS = 2 * ROUTE_CHUNK + N_EXPERTS * ROW_TILE
TOTAL_ROWS = N_CHUNKS * CHUNK_ROWS
MAX_ROW_TILES = TOTAL_ROWS // ROW_TILE
MAX_GROUPS = MAX_ROW_TILES // GROUP + N_EXPERTS
TD = 1024
TC = 512
COMBINE_STEPS = ROUTE_CHUNK // TC
VMEM_LIMIT = 58 * 1024 * 1024


def _gelu_tanh(x):
    c = 0.7978845608028654
    return 0.5 * x * (1.0 + jnp.tanh(c * (x + 0.044715 * (x * x * x))))


def _split_bf16(x):
    hi = x.astype(BF16)
    lo = (x - hi.astype(F32)).astype(BF16)
    return hi, lo


def _pack_halves(x):
    return pltpu.pack_elementwise([x[:, :HALF], x[:, HALF:]], packed_dtype=BF16)


def _unpack_halves(words):
    lo = pltpu.unpack_elementwise(words, index=0, packed_dtype=BF16, unpacked_dtype=F32)
    hi = pltpu.unpack_elementwise(words, index=1, packed_dtype=BF16, unpacked_dtype=F32)
    return lo, hi


def _ada_kernel(c_ref, w_ref, b_ref, o_ref):
    c = c_ref[...]
    ca = c * (1.0 / (1.0 + jnp.exp(-c)))
    c_hi, c_lo = _split_bf16(ca)
    w_hi, w_lo = _split_bf16(w_ref[0])
    acc = jnp.dot(c_hi, w_hi, preferred_element_type=F32)
    acc += jnp.dot(c_lo, w_hi, preferred_element_type=F32)
    acc += jnp.dot(c_hi, w_lo, preferred_element_type=F32)
    o_ref[0] = acc + b_ref[0]


def _ada_table(c, w_ada, b_ada):
    n_blk = N_ADA
    return pl.pallas_call(
        _ada_kernel,
        grid=(DEPTH, n_blk),
        in_specs=[
            pl.BlockSpec((BATCH, D_MODEL), lambda l, n: (0, 0)),
            pl.BlockSpec((1, D_MODEL, D_MODEL), lambda l, n: (l, 0, n)),
            pl.BlockSpec((1, 1, D_MODEL), lambda l, n: (l, 0, n)),
        ],
        out_specs=pl.BlockSpec((1, BATCH, D_MODEL), lambda l, n: (l, 0, n)),
        out_shape=jax.ShapeDtypeStruct((DEPTH, BATCH, N_ADA * D_MODEL), F32),
        name="ada_table",
    )(c, w_ada, b_ada.reshape(DEPTH, 1, N_ADA * D_MODEL))


def _mix_kernel(x_ref, ada_ref, n1g_ref, n2g_ref, win_ref, cw_ref, cb_ref, spg_ref, spw_ref,
                spbias_ref, mog_ref, wout_ref, rw_ref, rb_ref, hred_ref, hexp_ref,
                tri_ref, x1_ref, h2p_ref, rt_ref, cnt_ref, z_scr, cnt_scr):
    i = pl.program_id(0)

    @pl.when(i % TILES_PER_SEQ == 0)
    def _():
        z_scr[pl.ds(0, SUBLANES), :] = jnp.zeros((SUBLANES, D_CONV), F32)

    @pl.when(i % TILES_PER_CHUNK == 0)
    def _():
        cnt_scr[...] = jnp.zeros((N_EXPERTS, LANES), F32)

    chains = [
        _mix_rows(sb * SB, x_ref, ada_ref, n1g_ref, n2g_ref, win_ref, cw_ref, cb_ref, spg_ref,
                  spw_ref, spbias_ref, mog_ref, wout_ref, rw_ref, rb_ref, hred_ref,
                  hexp_ref, tri_ref, x1_ref, h2p_ref, rt_ref, z_scr, cnt_scr)
        for sb in range(TM // SB)]
    for c in MIX_STAGE_ORDER:
        next(chains[c])
    z_scr[pl.ds(0, SUBLANES), :] = z_scr[pl.ds(TM, SUBLANES), :]
    cnt_ref[0] = cnt_scr[...]


MIX_STAGES = 8
MIX_SKEW = 2
MIX_STAGE_ORDER = tuple(
    c for t in range(MIX_STAGES + MIX_SKEW * (TM // SB - 1)) for c in range(TM // SB)
    if 0 <= t - MIX_SKEW * c < MIX_STAGES)


def _mix_rows(r0, x_ref, ada_ref, n1g_ref, n2g_ref, win_ref, cw_ref, cb_ref, spg_ref, spw_ref,
              spbias_ref, mog_ref, wout_ref, rw_ref, rb_ref, hred_ref, hexp_ref,
              tri_ref, x1_ref, h2p_ref, rt_ref, z_scr, cnt_scr):
    x = x_ref[pl.ds(r0, SB), :]
    ada = ada_ref[0]
    sh1, sc1, g1 = ada[0:1], ada[1:2], ada[2:3]
    sh2, sc2 = ada[3:4], ada[4:5]

    rinv = lax.rsqrt(jnp.mean(x * x, axis=-1, keepdims=True) + EPS)
    h = (x * rinv) * (n1g_ref[...] * (1.0 + sc1)) + sh1
    hb = h.astype(BF16)
    yield

    def proj(k):
        return jnp.dot(hb, win_ref[:, k * 512:(k + 1) * 512], preferred_element_type=F32)

    z = proj(1) * proj(2)
    z_scr[pl.ds(SUBLANES + r0, SB), :] = z
    z1 = z_scr[pl.ds(SUBLANES + r0 - 1, SB), :]
    z2 = z_scr[pl.ds(SUBLANES + r0 - 2, SB), :]
    yield
    p_b, p_u, p_v = proj(0), proj(3), proj(4)
    yield
    cw = cw_ref[...]
    acc = cb_ref[...] + cw[0:1] * z2
    acc = acc + cw[1:2] * z1
    acc = acc + cw[2:3] * z
    y_conv = p_b * acc

    ug = _gelu_tanh(p_u)
    vg = _gelu_tanh(p_v)
    vr = lax.rsqrt(jnp.mean(vg * vg, axis=-1, keepdims=True) + EPS)
    vn = (vg * vr * spg_ref[...]).astype(BF16)
    lane = lax.broadcasted_iota(I32, (1, LANES), 1)
    first_head = lane < HEAD_DIM
    zero = jnp.zeros((), BF16)
    s_rows = []
    for blk in range(SB // SP_BLOCK):
        cols = []
        for p in range(N_SP_HEADS // 2):
            vv = vn[blk * SP_BLOCK:(blk + 1) * SP_BLOCK, p * LANES:(p + 1) * LANES]
            rhs = jnp.concatenate([jnp.where(first_head, vv, zero), jnp.where(first_head, zero, vv)], axis=0)
            cols.append(jnp.dot(spw_ref[p], rhs, preferred_element_type=F32))
        s_rows.append(jnp.concatenate(cols, axis=-1) + spbias_ref[...])
    y_sp = ug * jnp.concatenate(s_rows, axis=0)
    yield

    y = jnp.concatenate([y_conv, y_sp], axis=-1)
    ms = jnp.dot((y * y).astype(BF16), hred_ref[...], preferred_element_type=F32)
    r_hi, r_lo = _split_bf16(lax.rsqrt(ms + EPS))
    scale = jnp.dot(jnp.concatenate([r_hi, r_lo], axis=-1), hexp_ref[...], preferred_element_type=F32)
    yn = (y * scale * mog_ref[...]).astype(BF16)
    yield
    x1 = x + g1 * jnp.dot(yn, wout_ref[...], preferred_element_type=F32)
    x1_ref[pl.ds(r0, SB), :] = x1
    yield

    r2 = lax.rsqrt(jnp.mean(x1 * x1, axis=-1, keepdims=True) + EPS)
    h2 = (x1 * r2) * (n2g_ref[...] * (1.0 + sc2)) + sh2
    h2_hi = h2.astype(BF16)
    packed = _pack_halves(h2_hi.astype(F32))
    for j in range(ROW_TILES):
        h2p_ref[pl.ds(r0 // SUBLANES, SB // SUBLANES), j, :, :] = (
            packed[:, j * LANES:(j + 1) * LANES].reshape(SB // SUBLANES, SUBLANES, LANES))
    yield

    logits_t = jnp.dot(h2_hi, rw_ref[...], preferred_element_type=F32).T
    lg = logits_t[0:N_EXPERTS, :] + logits_t[N_EXPERTS:2 * N_EXPERTS, :]
    scores = 1.0 / (1.0 + jnp.exp(-lg))
    sel = scores + rb_ref[...]

    gs = []
    for g in range(N_GROUPS):
        a, b, c, d = (sel[g * PER_GROUP + k:g * PER_GROUP + k + 1, :] for k in range(PER_GROUP))
        hi1, lo1 = jnp.maximum(a, b), jnp.minimum(a, b)
        hi2, lo2 = jnp.maximum(c, d), jnp.minimum(c, d)
        gs.append(jnp.maximum(hi1, hi2) + jnp.maximum(jnp.minimum(hi1, hi2), jnp.maximum(lo1, lo2)))
    best = gs[0]
    top_group = jnp.zeros((1, SB), I32)
    for g in range(1, N_GROUPS):
        better = gs[g] > best
        best = jnp.where(better, gs[g], best)
        top_group = jnp.where(better, g, top_group)

    eid_i = lax.broadcasted_iota(I32, (N_EXPERTS, SB), 0)
    eid = eid_i.astype(F32)
    neg_inf = jnp.float32(-jnp.inf)
    masked = jnp.where((eid_i // PER_GROUP) == top_group, sel, neg_inf)
    m1 = jnp.max(masked, axis=0, keepdims=True)
    i1 = jnp.min(jnp.where(masked == m1, eid, float(N_EXPERTS)), axis=0, keepdims=True)
    masked2 = jnp.where(eid == i1, neg_inf, masked)
    m2 = jnp.max(masked2, axis=0, keepdims=True)
    i2 = jnp.min(jnp.where(masked2 == m2, eid, float(N_EXPERTS)), axis=0, keepdims=True)
    pick1 = eid == i1
    pick2 = eid == i2
    w1 = jnp.sum(jnp.where(pick1, scores, 0.0), axis=0, keepdims=True)
    w2 = jnp.sum(jnp.where(pick2, scores, 0.0), axis=0, keepdims=True)
    wsum = w1 + w2
    c1 = w1 / wsum
    c2 = w2 / wsum

    assigned = jnp.where(pick1 | pick2, 1.0, 0.0)
    before = jnp.dot(assigned.astype(BF16), tri_ref[...], preferred_element_type=F32)
    carry = cnt_scr[...]
    rank = before + carry[:, 0:1]
    rank1 = jnp.sum(jnp.where(pick1, rank, 0.0), axis=0, keepdims=True)
    rank2 = jnp.sum(jnp.where(pick2, rank, 0.0), axis=0, keepdims=True)
    cnt_scr[...] = carry + jnp.sum(assigned, axis=1, keepdims=True)
    unused = jnp.zeros((1, SB), F32)
    for r, row in enumerate((i1, i2, c1, c2, rank1, rank2, unused, unused)):
        rt_ref[r:r + 1, pl.ds(r0, SB)] = row
    yield


def _mix_call(x2d, ada_l, consts):
    n_tiles = N_TOK // TM
    full = lambda shape: pl.BlockSpec(shape, lambda i: (0,) * len(shape))
    in_specs = [
        pl.BlockSpec((TM, D_MODEL), lambda i: (i, 0)),
        pl.BlockSpec((1, N_ADA, D_MODEL), lambda i: (i // TILES_PER_SEQ, 0, 0)),
        full((1, D_MODEL)), full((1, D_MODEL)),
        full((D_MODEL, 5 * 512)),
        full((3, D_CONV)), full((1, D_CONV)), full((1, D_SPATIAL)),
        full((N_SP_HEADS // 2, SP_BLOCK, 2 * SP_BLOCK)),
        full((SP_BLOCK, D_SPATIAL)),
        full((1, D_MODEL)),
        full((D_MODEL, D_MODEL)),
        full((D_MODEL, LANES)), full((N_EXPERTS, 1)),
        full((D_MODEL, LANES)), full((2 * LANES, D_MODEL)),
        full((SB, SB)),
    ]
    out_specs = [
        pl.BlockSpec((TM, D_MODEL), lambda i: (i, 0)),
        pl.BlockSpec((TM // SUBLANES, ROW_TILES, SUBLANES, LANES), lambda i: (i, 0, 0, 0)),
        pl.BlockSpec((SUBLANES, TM), lambda i: (0, i)),
        pl.BlockSpec((1, N_EXPERTS, LANES), lambda i: (i, 0, 0)),
    ]
    out_shape = [
        jax.ShapeDtypeStruct((N_TOK, D_MODEL), F32),
        jax.ShapeDtypeStruct((N_TOK // SUBLANES, ROW_TILES, SUBLANES, LANES), U32),
        jax.ShapeDtypeStruct((SUBLANES, N_TOK), F32),
        jax.ShapeDtypeStruct((n_tiles, N_EXPERTS, LANES), F32),
    ]
    return pl.pallas_call(
        _mix_kernel,
        grid=(n_tiles,),
        in_specs=in_specs,
        out_specs=out_specs,
        out_shape=out_shape,
        scratch_shapes=[pltpu.VMEM((TM + SUBLANES, D_CONV), F32), pltpu.VMEM((N_EXPERTS, LANES), F32)],
        compiler_params=pltpu.CompilerParams(
            dimension_semantics=("arbitrary",), vmem_limit_bytes=VMEM_LIMIT),
        name="mix",
    )(x2d, ada_l, *consts)


def _row_addr(t):
    return (t // SUBLANES) * (ROW_TILES * SUBLANES) + t % SUBLANES


def _dispatch_kernel(a0_ref, a1_ref, h2p_ref, xs_ref):
    c = pl.program_id(0)
    s = pl.program_id(1)

    @pl.when(s == 0)
    def _():
        xs_ref[...] = jnp.zeros(xs_ref.shape, U32)

    base = c * ROUTE_CHUNK + s * TD

    for k in range(TD):
        row = h2p_ref[pl.ds(_row_addr(k), ROW_TILES, stride=SUBLANES), :]
        xs_ref[pl.ds(a0_ref[base + k], ROW_TILES, stride=SUBLANES), :] = row
        xs_ref[pl.ds(a1_ref[base + k], ROW_TILES, stride=SUBLANES), :] = row


def _dispatch_call(a0, a1, h2p_2d):
    steps = ROUTE_CHUNK // TD
    return pl.pallas_call(
        _dispatch_kernel,
        grid_spec=pltpu.PrefetchScalarGridSpec(
            num_scalar_prefetch=2,
            grid=(N_CHUNKS, steps),
            in_specs=[pl.BlockSpec((TD * ROW_TILES, LANES), lambda c, s, a0, a1: (c * steps + s, 0))],
            out_specs=pl.BlockSpec((CHUNK_ROWS * ROW_TILES, LANES), lambda c, s, a0, a1: (c, 0)),
        ),
        out_shape=jax.ShapeDtypeStruct((TOTAL_ROWS * ROW_TILES, LANES), U32),
        compiler_params=pltpu.CompilerParams(
            dimension_semantics=("arbitrary", "arbitrary"), vmem_limit_bytes=VMEM_LIMIT),
        name="dispatch",
    )(a0, a1, h2p_2d)


def _expert_kernel(tb_ref, gst_ref, xs_hbm, wg_ref, wu_ref, wd_ref, ys_hbm,
                   wg_s, wu_s, wd_s, xbuf, ybuf, sem_in, sem_out):
    e = pl.program_id(0)
    g_first = gst_ref[e]
    g_end = gst_ref[e + 1]
    g_all = gst_ref[N_EXPERTS]

    def in_copy(q, t):
        rows = pl.ds(tb_ref[GROUP * q + t] * ROW_GROUPS, ROW_GROUPS)
        return pltpu.make_async_copy(xs_hbm.at[rows], xbuf.at[q % 2, t], sem_in.at[q % 2, t])

    def out_copy(q, t):
        rows = pl.ds(tb_ref[GROUP * q + t] * ROW_GROUPS, ROW_GROUPS)
        return pltpu.make_async_copy(ybuf.at[q % 2, t], ys_hbm.at[rows], sem_out.at[q % 2, t])

    def for_group(q, action, make):
        for t in range(GROUP):
            @pl.when(tb_ref[GROUP * q + t] >= 0)
            def _():
                if action == "start":
                    make(q, t).start(priority=1)
                else:
                    make(q, t).wait()

    @pl.when(e == 0)
    def _():
        xbuf[...] = jnp.zeros(xbuf.shape, U32)

        @pl.when(g_all > 0)
        def _():
            for_group(0, "start", in_copy)

    wg_s[...] = wg_ref[0, 0].astype(BF16)
    wu_s[...] = wu_ref[0, 0].astype(BF16)
    wd_s[...] = wd_ref[0, 0].astype(BF16)

    def compute(slot, tiles):
        los, his = [], []
        for j in range(ROW_TILES):
            words = jnp.concatenate(
                [xbuf[slot, t, :, j, :, :].reshape(ROW_TILE, LANES) for t in tiles], axis=0)
            lo, hi = _unpack_halves(words)
            los.append(lo)
            his.append(hi)
        xb = jnp.concatenate(los + his, axis=-1).astype(BF16)
        g = jnp.dot(xb, wg_s[...], preferred_element_type=F32)
        u = jnp.dot(xb, wu_s[...], preferred_element_type=F32)
        act = (g * (1.0 / (1.0 + jnp.exp(-g))) * u).astype(BF16)
        y = jnp.dot(act, wd_s[...], preferred_element_type=F32)
        packed = _pack_halves(y.astype(BF16).astype(F32))
        for n, t in enumerate(tiles):
            for j in range(ROW_TILES):
                ybuf[slot, t, :, j, :, :] = packed[n * ROW_TILE:(n + 1) * ROW_TILE,
                                                   j * LANES:(j + 1) * LANES].reshape(
                    ROW_GROUPS, SUBLANES, LANES)

    def body(q, carry):
        slot = q % 2
        for_group(q, "wait", in_copy)

        @pl.when(q + 1 < g_all)
        def _():
            for_group(q + 1, "start", in_copy)

        @pl.when(q >= 2)
        def _():
            for_group(q - 2, "wait", out_copy)

        compute(slot, (0, 1, 2, 3))
        for_group(q, "start", out_copy)
        return carry

    lax.fori_loop(g_first, g_end, body, 0)

    @pl.when(e == N_EXPERTS - 1)
    def _():
        @pl.when(g_all >= 2)
        def _():
            for_group(g_all - 2, "wait", out_copy)

        @pl.when(g_all >= 1)
        def _():
            for_group(g_all - 1, "wait", out_copy)


def _expert_call(layer, group_tiles, group_start, xs4, wg, wu, wd):
    group_buf = (2, GROUP, ROW_GROUPS, ROW_TILES, SUBLANES, LANES)
    return pl.pallas_call(
        _expert_kernel,
        grid_spec=pltpu.PrefetchScalarGridSpec(
            num_scalar_prefetch=2,
            grid=(N_EXPERTS,),
            in_specs=[
                pl.BlockSpec(memory_space=pl.ANY),
                pl.BlockSpec((1, 1, D_MODEL, D_EXPERT), lambda e, tb, gs: (layer, e, 0, 0)),
                pl.BlockSpec((1, 1, D_MODEL, D_EXPERT), lambda e, tb, gs: (layer, e, 0, 0)),
                pl.BlockSpec((1, 1, D_EXPERT, D_MODEL), lambda e, tb, gs: (layer, e, 0, 0)),
            ],
            out_specs=pl.BlockSpec(memory_space=pl.ANY),
            scratch_shapes=[
                pltpu.VMEM((D_MODEL, D_EXPERT), BF16),
                pltpu.VMEM((D_MODEL, D_EXPERT), BF16),
                pltpu.VMEM((D_EXPERT, D_MODEL), BF16),
                pltpu.VMEM(group_buf, U32),
                pltpu.VMEM(group_buf, U32),
                pltpu.SemaphoreType.DMA((2, GROUP)),
                pltpu.SemaphoreType.DMA((2, GROUP)),
            ],
        ),
        out_shape=jax.ShapeDtypeStruct((TOTAL_ROWS // SUBLANES, ROW_TILES, SUBLANES, LANES), U32),
        input_output_aliases={2: 0},
        compiler_params=pltpu.CompilerParams(
            dimension_semantics=("arbitrary",), vmem_limit_bytes=VMEM_LIMIT),
        name="experts",
    )(group_tiles, group_start, xs4, wg, wu, wd)


def _combine_kernel(a0_ref, a1_ref, ys_hbm, x1_ref, w_ref, ada_ref, fg_ref, out_ref,
                    ys_buf, r0_scr, r1_scr, sem, *, final):
    c = pl.program_id(0)
    s = pl.program_id(1)
    base = c * ROUTE_CHUNK + s * TC
    region = CHUNK_ROWS * ROW_TILES
    piece = region // COMBINE_STEPS

    def piece_copy(chunk, p):
        src = ys_hbm.at[pl.ds(chunk * region + p * piece, piece)]
        dst = ys_buf.at[pl.ds((chunk % 2) * region + p * piece, piece)]
        return pltpu.make_async_copy(src, dst, sem.at[chunk % 2, p])

    @pl.when((c == 0) & (s == 0))
    def _():
        for p in range(COMBINE_STEPS):
            piece_copy(0, p).start()
        for p in range(COMBINE_STEPS):
            piece_copy(0, p).wait()

    @pl.when(c + 1 < N_CHUNKS)
    def _():
        for p in range(COMBINE_STEPS):
            @pl.when(s == p)
            def _():
                piece_copy(c + 1, p).start()

    off = (c % 2) * region

    for k in range(TC):
        dst = pl.ds(_row_addr(k), ROW_TILES, stride=SUBLANES)
        r0_scr[dst, :] = ys_buf[pl.ds(off + a0_ref[base + k], ROW_TILES, stride=SUBLANES), :]
        r1_scr[dst, :] = ys_buf[pl.ds(off + a1_ref[base + k], ROW_TILES, stride=SUBLANES), :]

    w = w_ref[...]
    w0 = w[:, 0:1]
    w1 = w[:, 1:2]
    g2 = ada_ref[0][5:6]
    n_grp = TC // SUBLANES
    los, his = [], []
    for j in range(ROW_TILES):
        parts0 = [r0_scr[pl.ds((tb * ROW_TILES + j) * SUBLANES, SUBLANES), :] for tb in range(n_grp)]
        parts1 = [r1_scr[pl.ds((tb * ROW_TILES + j) * SUBLANES, SUBLANES), :] for tb in range(n_grp)]
        lo0, hi0 = _unpack_halves(jnp.concatenate(parts0, axis=0))
        lo1, hi1 = _unpack_halves(jnp.concatenate(parts1, axis=0))
        los.append(w0 * lo0 + w1 * lo1)
        his.append(w0 * hi0 + w1 * hi1)
    x2 = x1_ref[...] + g2 * jnp.concatenate(los + his, axis=-1)
    if final:
        r = lax.rsqrt(jnp.mean(x2 * x2, axis=-1, keepdims=True) + EPS)
        x2 = x2 * r * fg_ref[...]
    out_ref[...] = x2

    @pl.when((s == COMBINE_STEPS - 1) & (c + 1 < N_CHUNKS))
    def _():
        for p in range(COMBINE_STEPS):
            piece_copy(c + 1, p).wait()


def _combine_call(a0, a1, ys_2d, x1, wcol, ada_l, final_g, final):
    tok = lambda c, s, a0, a1: (c * COMBINE_STEPS + s, 0)
    return pl.pallas_call(
        functools.partial(_combine_kernel, final=final),
        grid_spec=pltpu.PrefetchScalarGridSpec(
            num_scalar_prefetch=2,
            grid=(N_CHUNKS, COMBINE_STEPS),
            in_specs=[
                pl.BlockSpec(memory_space=pl.ANY),
                pl.BlockSpec((TC, D_MODEL), tok),
                pl.BlockSpec((TC, 2), tok),
                pl.BlockSpec((1, N_ADA, D_MODEL),
                             lambda c, s, a0, a1: ((c * ROUTE_CHUNK + s * TC) // SEQ, 0, 0)),
                pl.BlockSpec((1, D_MODEL), lambda c, s, a0, a1: (0, 0)),
            ],
            out_specs=pl.BlockSpec((TC, D_MODEL), tok),
            scratch_shapes=[pltpu.VMEM((2 * CHUNK_ROWS * ROW_TILES, LANES), U32),
                            pltpu.VMEM((TC * ROW_TILES, LANES), U32),
                            pltpu.VMEM((TC * ROW_TILES, LANES), U32),
                            pltpu.SemaphoreType.DMA((2, COMBINE_STEPS))],
        ),
        out_shape=jax.ShapeDtypeStruct((N_TOK, D_MODEL), F32),
        compiler_params=pltpu.CompilerParams(
            dimension_semantics=("arbitrary", "arbitrary"), vmem_limit_bytes=VMEM_LIMIT),
        name="combine_final" if final else "combine",
    )(a0, a1, ys_2d, x1, wcol, ada_l, final_g)


def _routing_tables(rt, cnt):
    e0 = rt[0].astype(I32)
    e1 = rt[1].astype(I32)
    rank0 = rt[4].astype(I32)
    rank1 = rt[5].astype(I32)
    counts = cnt[TILES_PER_CHUNK - 1::TILES_PER_CHUNK, :, 0].astype(I32)
    n_tiles = (counts + ROW_TILE - 1) // ROW_TILE
    tile_off = jnp.cumsum(n_tiles, axis=1) - n_tiles
    row_off = tile_off * ROW_TILE
    experts = jnp.arange(N_EXPERTS, dtype=I32)[None, :, None]
    off_tok = row_off[:, :, None]
    pick = lambda e: jnp.sum(
        jnp.where(e.reshape(N_CHUNKS, 1, ROUTE_CHUNK) == experts, off_tok, 0), axis=1).reshape(N_TOK)
    slot0 = pick(e0) + rank0
    slot1 = pick(e1) + rank1
    a0 = _row_addr(slot0)
    a1 = _row_addr(slot1)

    nt_flat = n_tiles.T.reshape(-1)
    ends = jnp.cumsum(nt_flat)
    starts = ends - nt_flat
    total = ends[-1]
    step = jnp.minimum(jnp.arange(MAX_ROW_TILES, dtype=I32), total - 1)
    seg = jnp.sum((ends[None, :] <= step[:, None]).astype(I32), axis=1)
    in_seg = seg[:, None] == jnp.arange(N_EXPERTS * N_CHUNKS, dtype=I32)[None, :]
    k = step - jnp.sum(jnp.where(in_seg, starts[None, :], 0), axis=1)
    c = seg % N_CHUNKS
    tile_off_flat = tile_off.T.reshape(-1)
    tile_blk = c * (CHUNK_ROWS // ROW_TILE) + jnp.sum(jnp.where(in_seg, tile_off_flat[None, :], 0), axis=1) + k
    e_count = jnp.sum(n_tiles, axis=0)
    e_start = jnp.cumsum(e_count) - e_count
    g_count = (e_count + GROUP - 1) // GROUP
    g_end = jnp.cumsum(g_count)
    g_start = g_end - g_count
    slot_id = jnp.arange(MAX_GROUPS * GROUP, dtype=I32)
    q = slot_id // GROUP
    owner = jnp.sum((g_end[None, :] <= q[:, None]).astype(I32), axis=1)
    is_owner = owner[:, None] == jnp.arange(N_EXPERTS, dtype=I32)[None, :]
    sel = lambda v: jnp.sum(jnp.where(is_owner, v[None, :], 0), axis=1)
    k = (q - sel(g_start)) * GROUP + slot_id % GROUP
    valid = (owner < N_EXPERTS) & (k < sel(e_count))
    flat = sel(e_start) + k
    pick_tile = flat[:, None] == jnp.arange(MAX_ROW_TILES, dtype=I32)[None, :]
    group_tiles = jnp.where(valid, jnp.sum(jnp.where(pick_tile, tile_blk[None, :], 0), axis=1), -1)
    group_start = jnp.concatenate([g_start, g_end[-1:]])
    return a0, a1, group_tiles.astype(I32), group_start.astype(I32)


def _layer_consts(l, norm1_g, norm2_g, w_in, conv_w, conv_b, sp_norm_g, sp_w, sp_b, mix_out_g,
                  w_out, shared):
    pos = jnp.arange(SP_BLOCK)
    mask = (pos[None, :] // CHUNK) <= (pos[:, None] // CHUNK)
    w = jnp.where(mask[None], sp_w[l], 0.0)
    w_pairs = jnp.concatenate([w[0::2], w[1::2]], axis=-1).astype(BF16)
    sp_bias = jnp.repeat(sp_b[l].T, HEAD_DIM, axis=1)
    return (
        norm1_g[l].reshape(1, D_MODEL), norm2_g[l].reshape(1, D_MODEL),
        w_in[l].astype(BF16),
        conv_w[l], conv_b[l].reshape(1, D_CONV), sp_norm_g[l].reshape(1, D_SPATIAL),
        w_pairs, sp_bias,
        mix_out_g[l].reshape(1, D_MODEL),
        w_out[l].astype(BF16),
    ) + shared


def kernel(x, c, w_ada, b_ada, norm1_g, norm2_g, w_in, conv_w, conv_b, sp_norm_g, sp_w, sp_b, mix_out_g, w_out, router_w, router_b, exp_w_gate, exp_w_up, exp_w_down, final_g):
    ada = _ada_table(c, w_ada, b_ada).reshape(DEPTH, BATCH, N_ADA, D_MODEL)

    rw_hi = router_w.astype(BF16)
    rw_lo = (router_w - rw_hi.astype(F32)).astype(BF16)
    rw_cat = jnp.pad(jnp.concatenate([rw_hi, rw_lo], axis=1), ((0, 0), (0, LANES - 2 * N_EXPERTS)))
    head_of = jnp.arange(D_MODEL) // HEAD_DIM
    hred = (head_of[:, None] == jnp.arange(LANES)[None, :]).astype(F32) / HEAD_DIM
    hexp_half = (jnp.arange(LANES)[:, None] == head_of[None, :]).astype(F32)
    hexp = jnp.concatenate([hexp_half, hexp_half], axis=0)
    tri = (jnp.arange(SB)[:, None] < jnp.arange(SB)[None, :])
    shared = (rw_cat, router_b.reshape(N_EXPERTS, 1), hred.astype(BF16), hexp.astype(BF16),
              tri.astype(BF16))
    fg = final_g.reshape(1, D_MODEL)

    xc = x.reshape(N_TOK, D_MODEL)
    for l in range(DEPTH):
        consts = _layer_consts(l, norm1_g, norm2_g, w_in, conv_w, conv_b, sp_norm_g, sp_w, sp_b,
                               mix_out_g, w_out, shared)
        x1, h2p, rt, cnt = _mix_call(xc, ada[l], consts)
        a0, a1, group_tiles, group_start = _routing_tables(rt, cnt)
        xs = _dispatch_call(a0, a1, h2p.reshape(N_TOK * ROW_TILES, LANES))
        ys = _expert_call(l, group_tiles, group_start,
                          xs.reshape(TOTAL_ROWS // SUBLANES, ROW_TILES, SUBLANES, LANES),
                          exp_w_gate, exp_w_up, exp_w_down)
        wcol = rt[2:4].T
        xc = _combine_call(a0, a1, ys.reshape(TOTAL_ROWS * ROW_TILES, LANES), x1, wcol, ada[l], fg,
                           final=(l == DEPTH - 1))
    return xc.reshape(BATCH, SEQ, D_MODEL)
```

```python
import functools

import jax
import jax.numpy as jnp
from jax import lax
from jax.experimental import pallas as pl
from jax.experimental.pallas import tpu as pltpu

F32 = jnp.float32
BF16 = jnp.bfloat16
U32 = jnp.uint32
I32 = jnp.int32

D_MODEL = 1024
BATCH = 8
SEQ = 2048
DEPTH = 4
N_TOK = BATCH * SEQ
CHUNK = 64
HEAD_DIM = 64
D_CONV = 512
D_SPATIAL = 512
N_SP_HEADS = 8
N_MIX_HEADS = 16
SP_BLOCK = 128
N_EXPERTS = 16
N_GROUPS = 4
PER_GROUP = 4
D_EXPERT = 512
N_ADA = 6
EPS = 1e-6

LANES = 128
SUBLANES = 8
HALF = D_MODEL // 2
ROW_TILES = HALF // LANES

TM = 1024
SB = 256
TILES_PER_SEQ = SEQ // TM
ROUTE_CHUNK = 4096
N_CHUNKS = N_TOK // ROUTE_CHUNK
TILES_PER_CHUNK = ROUTE_CHUNK // TM
ROW_TILE = 64
ROW_GROUPS = ROW_TILE // SUBLANES
GROUP = 8
GROUP_WIDTHS = (8, 6, 4, 2)
CHUNK_ROWS = 2 * ROUTE_CHUNK + N_EXPERTS * ROW_TILE
TOTAL_ROWS = N_CHUNKS * CHUNK_ROWS
MAX_ROW_TILES = TOTAL_ROWS // ROW_TILE
MAX_GROUPS = MAX_ROW_TILES // GROUP + N_EXPERTS
TD = 1024
TC = 512
COMBINE_STEPS = ROUTE_CHUNK // TC
VMEM_LIMIT = 58 * 1024 * 1024


def _gelu_tanh(x):
    c = 0.7978845608028654
    return 0.5 * x * (1.0 + jnp.tanh(c * (x + 0.044715 * (x * x * x))))


def _split_bf16(x):
    hi = x.astype(BF16)
    lo = (x - hi.astype(F32)).astype(BF16)
    return hi, lo


def _pack_halves(x):
    return pltpu.pack_elementwise([x[:, :HALF], x[:, HALF:]], packed_dtype=BF16)


def _unpack_halves(words):
    lo = pltpu.unpack_elementwise(words, index=0, packed_dtype=BF16, unpacked_dtype=F32)
    hi = pltpu.unpack_elementwise(words, index=1, packed_dtype=BF16, unpacked_dtype=F32)
    return lo, hi


def _ada_kernel(c_ref, w_ref, b_ref, o_ref):
    c = c_ref[...]
    ca = c * (1.0 / (1.0 + jnp.exp(-c)))
    c_hi, c_lo = _split_bf16(ca)
    w_hi, w_lo = _split_bf16(w_ref[0])
    acc = jnp.dot(c_hi, w_hi, preferred_element_type=F32)
    acc += jnp.dot(c_lo, w_hi, preferred_element_type=F32)
    acc += jnp.dot(c_hi, w_lo, preferred_element_type=F32)
    o_ref[0] = acc + b_ref[0]


def _ada_table(c, w_ada, b_ada):
    n_blk = N_ADA
    return pl.pallas_call(
        _ada_kernel,
        grid=(DEPTH, n_blk),
        in_specs=[
            pl.BlockSpec((BATCH, D_MODEL), lambda l, n: (0, 0)),
            pl.BlockSpec((1, D_MODEL, D_MODEL), lambda l, n: (l, 0, n)),
            pl.BlockSpec((1, 1, D_MODEL), lambda l, n: (l, 0, n)),
        ],
        out_specs=pl.BlockSpec((1, BATCH, D_MODEL), lambda l, n: (l, 0, n)),
        out_shape=jax.ShapeDtypeStruct((DEPTH, BATCH, N_ADA * D_MODEL), F32),
        name="ada_table",
    )(c, w_ada, b_ada.reshape(DEPTH, 1, N_ADA * D_MODEL))


def _mix_kernel(x_ref, ada_ref, n1g_ref, n2g_ref, win_ref, cw_ref, cb_ref, spg_ref, spw_ref,
                spbias_ref, mog_ref, wout_ref, rw_ref, rb_ref, hred_ref, hexp_ref,
                tri_ref, x1_ref, h2p_ref, rt_ref, cnt_ref, z_scr, cnt_scr):
    i = pl.program_id(0)

    @pl.when(i % TILES_PER_SEQ == 0)
    def _():
        z_scr[pl.ds(0, SUBLANES), :] = jnp.zeros((SUBLANES, D_CONV), F32)

    @pl.when(i % TILES_PER_CHUNK == 0)
    def _():
        cnt_scr[...] = jnp.zeros((N_EXPERTS, LANES), F32)

    chains = [
        _mix_rows(sb * SB, x_ref, ada_ref, n1g_ref, n2g_ref, win_ref, cw_ref, cb_ref, spg_ref,
                  spw_ref, spbias_ref, mog_ref, wout_ref, rw_ref, rb_ref, hred_ref,
                  hexp_ref, tri_ref, x1_ref, h2p_ref, rt_ref, z_scr, cnt_scr)
        for sb in range(TM // SB)]
    for c in MIX_STAGE_ORDER:
        next(chains[c])
    z_scr[pl.ds(0, SUBLANES), :] = z_scr[pl.ds(TM, SUBLANES), :]
    cnt_ref[0] = cnt_scr[...]


MIX_STAGES = 8
MIX_SKEW = 2
MIX_STAGE_ORDER = tuple(
    c for t in range(MIX_STAGES + MIX_SKEW * (TM // SB - 1)) for c in range(TM // SB)
    if 0 <= t - MIX_SKEW * c < MIX_STAGES)


def _mix_rows(r0, x_ref, ada_ref, n1g_ref, n2g_ref, win_ref, cw_ref, cb_ref, spg_ref, spw_ref,
              spbias_ref, mog_ref, wout_ref, rw_ref, rb_ref, hred_ref, hexp_ref,
              tri_ref, x1_ref, h2p_ref, rt_ref, z_scr, cnt_scr):
    x = x_ref[pl.ds(r0, SB), :]
    ada = ada_ref[0]
    sh1, sc1, g1 = ada[0:1], ada[1:2], ada[2:3]
    sh2, sc2 = ada[3:4], ada[4:5]

    rinv = lax.rsqrt(jnp.mean(x * x, axis=-1, keepdims=True) + EPS)
    h = (x * rinv) * (n1g_ref[...] * (1.0 + sc1)) + sh1
    hb = h.astype(BF16)
    yield

    def proj(k):
        return jnp.dot(hb, win_ref[:, k * 512:(k + 1) * 512], preferred_element_type=F32)

    z = proj(1) * proj(2)
    z_scr[pl.ds(SUBLANES + r0, SB), :] = z
    z1 = z_scr[pl.ds(SUBLANES + r0 - 1, SB), :]
    z2 = z_scr[pl.ds(SUBLANES + r0 - 2, SB), :]
    yield
    p_b, p_u, p_v = proj(0), proj(3), proj(4)
    yield
    cw = cw_ref[...]
    acc = cb_ref[...] + cw[0:1] * z2
    acc = acc + cw[1:2] * z1
    acc = acc + cw[2:3] * z
    y_conv = p_b * acc

    ug = _gelu_tanh(p_u)
    vg = _gelu_tanh(p_v)
    vr = lax.rsqrt(jnp.mean(vg * vg, axis=-1, keepdims=True) + EPS)
    vn = (vg * vr * spg_ref[...]).astype(BF16)
    lane = lax.broadcasted_iota(I32, (1, LANES), 1)
    first_head = lane < HEAD_DIM
    zero = jnp.zeros((), BF16)
    s_rows = []
    for blk in range(SB // SP_BLOCK):
        cols = []
        for p in range(N_SP_HEADS // 2):
            vv = vn[blk * SP_BLOCK:(blk + 1) * SP_BLOCK, p * LANES:(p + 1) * LANES]
            rhs = jnp.concatenate([jnp.where(first_head, vv, zero), jnp.where(first_head, zero, vv)], axis=0)
            cols.append(jnp.dot(spw_ref[p], rhs, preferred_element_type=F32))
        s_rows.append(jnp.concatenate(cols, axis=-1) + spbias_ref[...])
    y_sp = ug * jnp.concatenate(s_rows, axis=0)
    yield

    y = jnp.concatenate([y_conv, y_sp], axis=-1)
    ms = jnp.dot((y * y).astype(BF16), hred_ref[...], preferred_element_type=F32)
    r_hi, r_lo = _split_bf16(lax.rsqrt(ms + EPS))
    scale = jnp.dot(jnp.concatenate([r_hi, r_lo], axis=-1), hexp_ref[...], preferred_element_type=F32)
    yn = (y * scale * mog_ref[...]).astype(BF16)
    yield
    x1 = x + g1 * jnp.dot(yn, wout_ref[...], preferred_element_type=F32)
    x1_ref[pl.ds(r0, SB), :] = x1
    yield

    r2 = lax.rsqrt(jnp.mean(x1 * x1, axis=-1, keepdims=True) + EPS)
    h2 = (x1 * r2) * (n2g_ref[...] * (1.0 + sc2)) + sh2
    h2_hi = h2.astype(BF16)
    packed = _pack_halves(h2_hi.astype(F32))
    for j in range(ROW_TILES):
        h2p_ref[pl.ds(r0 // SUBLANES, SB // SUBLANES), j, :, :] = (
            packed[:, j * LANES:(j + 1) * LANES].reshape(SB // SUBLANES, SUBLANES, LANES))
    yield

    logits_t = jnp.dot(h2_hi, rw_ref[...], preferred_element_type=F32).T
    lg = logits_t[0:N_EXPERTS, :] + logits_t[N_EXPERTS:2 * N_EXPERTS, :]
    scores = 1.0 / (1.0 + jnp.exp(-lg))
    sel = scores + rb_ref[...]

    gs = []
    for g in range(N_GROUPS):
        a, b, c, d = (sel[g * PER_GROUP + k:g * PER_GROUP + k + 1, :] for k in range(PER_GROUP))
        hi1, lo1 = jnp.maximum(a, b), jnp.minimum(a, b)
        hi2, lo2 = jnp.maximum(c, d), jnp.minimum(c, d)
        gs.append(jnp.maximum(hi1, hi2) + jnp.maximum(jnp.minimum(hi1, hi2), jnp.maximum(lo1, lo2)))
    best = gs[0]
    top_group = jnp.zeros((1, SB), I32)
    for g in range(1, N_GROUPS):
        better = gs[g] > best
        best = jnp.where(better, gs[g], best)
        top_group = jnp.where(better, g, top_group)

    eid_i = lax.broadcasted_iota(I32, (N_EXPERTS, SB), 0)
    eid = eid_i.astype(F32)
    neg_inf = jnp.float32(-jnp.inf)
    masked = jnp.where((eid_i // PER_GROUP) == top_group, sel, neg_inf)
    m1 = jnp.max(masked, axis=0, keepdims=True)
    i1 = jnp.min(jnp.where(masked == m1, eid, float(N_EXPERTS)), axis=0, keepdims=True)
    masked2 = jnp.where(eid == i1, neg_inf, masked)
    m2 = jnp.max(masked2, axis=0, keepdims=True)
    i2 = jnp.min(jnp.where(masked2 == m2, eid, float(N_EXPERTS)), axis=0, keepdims=True)
    pick1 = eid == i1
    pick2 = eid == i2
    w1 = jnp.sum(jnp.where(pick1, scores, 0.0), axis=0, keepdims=True)
    w2 = jnp.sum(jnp.where(pick2, scores, 0.0), axis=0, keepdims=True)
    wsum = w1 + w2
    c1 = w1 / wsum
    c2 = w2 / wsum

    assigned = jnp.where(pick1 | pick2, 1.0, 0.0)
    before = jnp.dot(assigned.astype(BF16), tri_ref[...], preferred_element_type=F32)
    carry = cnt_scr[...]
    rank = before + carry[:, 0:1]
    rank1 = jnp.sum(jnp.where(pick1, rank, 0.0), axis=0, keepdims=True)
    rank2 = jnp.sum(jnp.where(pick2, rank, 0.0), axis=0, keepdims=True)
    cnt_scr[...] = carry + jnp.sum(assigned, axis=1, keepdims=True)
    unused = jnp.zeros((1, SB), F32)
    for r, row in enumerate((i1, i2, c1, c2, rank1, rank2, unused, unused)):
        rt_ref[r:r + 1, pl.ds(r0, SB)] = row
    yield


def _mix_call(x2d, ada_l, consts):
    n_tiles = N_TOK // TM
    full = lambda shape: pl.BlockSpec(shape, lambda i: (0,) * len(shape))
    in_specs = [
        pl.BlockSpec((TM, D_MODEL), lambda i: (i, 0)),
        pl.BlockSpec((1, N_ADA, D_MODEL), lambda i: (i // TILES_PER_SEQ, 0, 0)),
        full((1, D_MODEL)), full((1, D_MODEL)),
        full((D_MODEL, 5 * 512)),
        full((3, D_CONV)), full((1, D_CONV)), full((1, D_SPATIAL)),
        full((N_SP_HEADS // 2, SP_BLOCK, 2 * SP_BLOCK)),
        full((SP_BLOCK, D_SPATIAL)),
        full((1, D_MODEL)),
        full((D_MODEL, D_MODEL)),
        full((D_MODEL, LANES)), full((N_EXPERTS, 1)),
        full((D_MODEL, LANES)), full((2 * LANES, D_MODEL)),
        full((SB, SB)),
    ]
    out_specs = [
        pl.BlockSpec((TM, D_MODEL), lambda i: (i, 0)),
        pl.BlockSpec((TM // SUBLANES, ROW_TILES, SUBLANES, LANES), lambda i: (i, 0, 0, 0)),
        pl.BlockSpec((SUBLANES, TM), lambda i: (0, i)),
        pl.BlockSpec((1, N_EXPERTS, LANES), lambda i: (i, 0, 0)),
    ]
    out_shape = [
        jax.ShapeDtypeStruct((N_TOK, D_MODEL), F32),
        jax.ShapeDtypeStruct((N_TOK // SUBLANES, ROW_TILES, SUBLANES, LANES), U32),
        jax.ShapeDtypeStruct((SUBLANES, N_TOK), F32),
        jax.ShapeDtypeStruct((n_tiles, N_EXPERTS, LANES), F32),
    ]
    return pl.pallas_call(
        _mix_kernel,
        grid=(n_tiles,),
        in_specs=in_specs,
        out_specs=out_specs,
        out_shape=out_shape,
        scratch_shapes=[pltpu.VMEM((TM + SUBLANES, D_CONV), F32), pltpu.VMEM((N_EXPERTS, LANES), F32)],
        compiler_params=pltpu.CompilerParams(
            dimension_semantics=("arbitrary",), vmem_limit_bytes=VMEM_LIMIT),
        name="mix",
    )(x2d, ada_l, *consts)


def _row_addr(t):
    return (t // SUBLANES) * (ROW_TILES * SUBLANES) + t % SUBLANES


def _dispatch_kernel(a0_ref, a1_ref, h2p_ref, xs_ref):
    c = pl.program_id(0)
    s = pl.program_id(1)

    @pl.when(s == 0)
    def _():
        xs_ref[...] = jnp.zeros(xs_ref.shape, U32)

    base = c * ROUTE_CHUNK + s * TD

    for k in range(TD):
        row = h2p_ref[pl.ds(_row_addr(k), ROW_TILES, stride=SUBLANES), :]
        xs_ref[pl.ds(a0_ref[base + k], ROW_TILES, stride=SUBLANES), :] = row
        xs_ref[pl.ds(a1_ref[base + k], ROW_TILES, stride=SUBLANES), :] = row


def _dispatch_call(a0, a1, h2p_2d):
    steps = ROUTE_CHUNK // TD
    return pl.pallas_call(
        _dispatch_kernel,
        grid_spec=pltpu.PrefetchScalarGridSpec(
            num_scalar_prefetch=2,
            grid=(N_CHUNKS, steps),
            in_specs=[pl.BlockSpec((TD * ROW_TILES, LANES), lambda c, s, a0, a1: (c * steps + s, 0))],
            out_specs=pl.BlockSpec((CHUNK_ROWS * ROW_TILES, LANES), lambda c, s, a0, a1: (c, 0)),
        ),
        out_shape=jax.ShapeDtypeStruct((TOTAL_ROWS * ROW_TILES, LANES), U32),
        compiler_params=pltpu.CompilerParams(
            dimension_semantics=("arbitrary", "arbitrary"), vmem_limit_bytes=VMEM_LIMIT),
        name="dispatch",
    )(a0, a1, h2p_2d)


def _expert_kernel(tb_ref, gst_ref, xs_hbm, wg_ref, wu_ref, wd_ref, ys_hbm,
                   wg_s, wu_s, wd_s, xbuf, ybuf, sem_in, sem_out):
    e = pl.program_id(0)
    g_first = gst_ref[e]
    g_end = gst_ref[e + 1]
    g_all = gst_ref[N_EXPERTS]

    def in_copy(q, t):
        rows = pl.ds(tb_ref[GROUP * q + t] * ROW_GROUPS, ROW_GROUPS)
        return pltpu.make_async_copy(xs_hbm.at[rows], xbuf.at[q % 2, t], sem_in.at[q % 2, t])

    def out_copy(q, t):
        rows = pl.ds(tb_ref[GROUP * q + t] * ROW_GROUPS, ROW_GROUPS)
        return pltpu.make_async_copy(ybuf.at[q % 2, t], ys_hbm.at[rows], sem_out.at[q % 2, t])

    def for_group(q, action, make):
        for t in range(GROUP):
            @pl.when(tb_ref[GROUP * q + t] >= 0)
            def _():
                if action == "start":
                    make(q, t).start(priority=1)
                else:
                    make(q, t).wait()

    @pl.when(e == 0)
    def _():
        xbuf[...] = jnp.zeros(xbuf.shape, U32)

        @pl.when(g_all > 0)
        def _():
            for_group(0, "start", in_copy)

    wg_s[...] = wg_ref[0, 0].astype(BF16)
    wu_s[...] = wu_ref[0, 0].astype(BF16)
    wd_s[...] = wd_ref[0, 0].astype(BF16)

    def compute(slot, tiles):
        los, his = [], []
        for j in range(ROW_TILES):
            words = jnp.concatenate(
                [xbuf[slot, t, :, j, :, :].reshape(ROW_TILE, LANES) for t in tiles], axis=0)
            lo, hi = _unpack_halves(words)
            los.append(lo)
            his.append(hi)
        xb = jnp.concatenate(los + his, axis=-1).astype(BF16)
        g = jnp.dot(xb, wg_s[...], preferred_element_type=F32)
        u = jnp.dot(xb, wu_s[...], preferred_element_type=F32)
        act = (g * (1.0 / (1.0 + jnp.exp(-g))) * u).astype(BF16)
        y = jnp.dot(act, wd_s[...], preferred_element_type=F32)
        packed = _pack_halves(y.astype(BF16).astype(F32))
        for n, t in enumerate(tiles):
            for j in range(ROW_TILES):
                ybuf[slot, t, :, j, :, :] = packed[n * ROW_TILE:(n + 1) * ROW_TILE,
                                                   j * LANES:(j + 1) * LANES].reshape(
                    ROW_GROUPS, SUBLANES, LANES)

    def body(q, carry):
        slot = q % 2
        for_group(q, "wait", in_copy)

        @pl.when(q + 1 < g_all)
        def _():
            for_group(q + 1, "start", in_copy)

        @pl.when(q >= 2)
        def _():
            for_group(q - 2, "wait", out_copy)

        n_valid = sum((tb_ref[GROUP * q + t] >= 0).astype(I32) for t in range(GROUP))
        for width, narrower in zip(GROUP_WIDTHS, GROUP_WIDTHS[1:] + (0,)):
            @pl.when((n_valid > narrower) & (n_valid <= width))
            def _():
                compute(slot, tuple(range(width)))
        for_group(q, "start", out_copy)
        return carry

    lax.fori_loop(g_first, g_end, body, 0)

    @pl.when(e == N_EXPERTS - 1)
    def _():
        @pl.when(g_all >= 2)
        def _():
            for_group(g_all - 2, "wait", out_copy)

        @pl.when(g_all >= 1)
        def _():
            for_group(g_all - 1, "wait", out_copy)


def _expert_call(layer, group_tiles, group_start, xs4, wg, wu, wd):
    group_buf = (2, GROUP, ROW_GROUPS, ROW_TILES, SUBLANES, LANES)
    return pl.pallas_call(
        _expert_kernel,
        grid_spec=pltpu.PrefetchScalarGridSpec(
            num_scalar_prefetch=2,
            grid=(N_EXPERTS,),
            in_specs=[
                pl.BlockSpec(memory_space=pl.ANY),
                pl.BlockSpec((1, 1, D_MODEL, D_EXPERT), lambda e, tb, gs: (layer, e, 0, 0)),
                pl.BlockSpec((1, 1, D_MODEL, D_EXPERT), lambda e, tb, gs: (layer, e, 0, 0)),
                pl.BlockSpec((1, 1, D_EXPERT, D_MODEL), lambda e, tb, gs: (layer, e, 0, 0)),
            ],
            out_specs=pl.BlockSpec(memory_space=pl.ANY),
            scratch_shapes=[
                pltpu.VMEM((D_MODEL, D_EXPERT), BF16),
                pltpu.VMEM((D_MODEL, D_EXPERT), BF16),
                pltpu.VMEM((D_EXPERT, D_MODEL), BF16),
                pltpu.VMEM(group_buf, U32),
                pltpu.VMEM(group_buf, U32),
                pltpu.SemaphoreType.DMA((2, GROUP)),
                pltpu.SemaphoreType.DMA((2, GROUP)),
            ],
        ),
        out_shape=jax.ShapeDtypeStruct((TOTAL_ROWS // SUBLANES, ROW_TILES, SUBLANES, LANES), U32),
        input_output_aliases={2: 0},
        compiler_params=pltpu.CompilerParams(
            dimension_semantics=("arbitrary",), vmem_limit_bytes=VMEM_LIMIT),
        name="experts",
    )(group_tiles, group_start, xs4, wg, wu, wd)


def _combine_kernel(a0_ref, a1_ref, ys_hbm, x1_ref, w_ref, ada_ref, fg_ref, out_ref,
                    ys_buf, r0_scr, r1_scr, sem, *, final):
    c = pl.program_id(0)
    s = pl.program_id(1)
    base = c * ROUTE_CHUNK + s * TC
    region = CHUNK_ROWS * ROW_TILES
    piece = region // COMBINE_STEPS

    def piece_copy(chunk, p):
        src = ys_hbm.at[pl.ds(chunk * region + p * piece, piece)]
        dst = ys_buf.at[pl.ds((chunk % 2) * region + p * piece, piece)]
        return pltpu.make_async_copy(src, dst, sem.at[chunk % 2, p])

    @pl.when((c == 0) & (s == 0))
    def _():
        for p in range(COMBINE_STEPS):
            piece_copy(0, p).start()
        for p in range(COMBINE_STEPS):
            piece_copy(0, p).wait()

    @pl.when(c + 1 < N_CHUNKS)
    def _():
        for p in range(COMBINE_STEPS):
            @pl.when(s == p)
            def _():
                piece_copy(c + 1, p).start()

    off = (c % 2) * region

    for k in range(TC):
        dst = pl.ds(_row_addr(k), ROW_TILES, stride=SUBLANES)
        r0_scr[dst, :] = ys_buf[pl.ds(off + a0_ref[base + k], ROW_TILES, stride=SUBLANES), :]
        r1_scr[dst, :] = ys_buf[pl.ds(off + a1_ref[base + k], ROW_TILES, stride=SUBLANES), :]

    w = w_ref[...]
    w0 = w[:, 0:1]
    w1 = w[:, 1:2]
    g2 = ada_ref[0][5:6]
    n_grp = TC // SUBLANES
    los, his = [], []
    for j in range(ROW_TILES):
        parts0 = [r0_scr[pl.ds((tb * ROW_TILES + j) * SUBLANES, SUBLANES), :] for tb in range(n_grp)]
        parts1 = [r1_scr[pl.ds((tb * ROW_TILES + j) * SUBLANES, SUBLANES), :] for tb in range(n_grp)]
        lo0, hi0 = _unpack_halves(jnp.concatenate(parts0, axis=0))
        lo1, hi1 = _unpack_halves(jnp.concatenate(parts1, axis=0))
        los.append(w0 * lo0 + w1 * lo1)
        his.append(w0 * hi0 + w1 * hi1)
    x2 = x1_ref[...] + g2 * jnp.concatenate(los + his, axis=-1)
    if final:
        r = lax.rsqrt(jnp.mean(x2 * x2, axis=-1, keepdims=True) + EPS)
        x2 = x2 * r * fg_ref[...]
    out_ref[...] = x2

    @pl.when((s == COMBINE_STEPS - 1) & (c + 1 < N_CHUNKS))
    def _():
        for p in range(COMBINE_STEPS):
            piece_copy(c + 1, p).wait()


def _combine_call(a0, a1, ys_2d, x1, wcol, ada_l, final_g, final):
    tok = lambda c, s, a0, a1: (c * COMBINE_STEPS + s, 0)
    return pl.pallas_call(
        functools.partial(_combine_kernel, final=final),
        grid_spec=pltpu.PrefetchScalarGridSpec(
            num_scalar_prefetch=2,
            grid=(N_CHUNKS, COMBINE_STEPS),
            in_specs=[
                pl.BlockSpec(memory_space=pl.ANY),
                pl.BlockSpec((TC, D_MODEL), tok),
                pl.BlockSpec((TC, 2), tok),
                pl.BlockSpec((1, N_ADA, D_MODEL),
                             lambda c, s, a0, a1: ((c * ROUTE_CHUNK + s * TC) // SEQ, 0, 0)),
                pl.BlockSpec((1, D_MODEL), lambda c, s, a0, a1: (0, 0)),
            ],
            out_specs=pl.BlockSpec((TC, D_MODEL), tok),
            scratch_shapes=[pltpu.VMEM((2 * CHUNK_ROWS * ROW_TILES, LANES), U32),
                            pltpu.VMEM((TC * ROW_TILES, LANES), U32),
                            pltpu.VMEM((TC * ROW_TILES, LANES), U32),
                            pltpu.SemaphoreType.DMA((2, COMBINE_STEPS))],
        ),
        out_shape=jax.ShapeDtypeStruct((N_TOK, D_MODEL), F32),
        compiler_params=pltpu.CompilerParams(
            dimension_semantics=("arbitrary", "arbitrary"), vmem_limit_bytes=VMEM_LIMIT),
        name="combine_final" if final else "combine",
    )(a0, a1, ys_2d, x1, wcol, ada_l, final_g)


def _routing_tables(rt, cnt):
    e0 = rt[0].astype(I32)
    e1 = rt[1].astype(I32)
    rank0 = rt[4].astype(I32)
    rank1 = rt[5].astype(I32)
    counts = cnt[TILES_PER_CHUNK - 1::TILES_PER_CHUNK, :, 0].astype(I32)
    n_tiles = (counts + ROW_TILE - 1) // ROW_TILE
    tile_off = jnp.cumsum(n_tiles, axis=1) - n_tiles
    row_off = tile_off * ROW_TILE
    experts = jnp.arange(N_EXPERTS, dtype=I32)[None, :, None]
    off_tok = row_off[:, :, None]
    pick = lambda e: jnp.sum(
        jnp.where(e.reshape(N_CHUNKS, 1, ROUTE_CHUNK) == experts, off_tok, 0), axis=1).reshape(N_TOK)
    slot0 = pick(e0) + rank0
    slot1 = pick(e1) + rank1
    a0 = _row_addr(slot0)
    a1 = _row_addr(slot1)

    nt_flat = n_tiles.T.reshape(-1)
    ends = jnp.cumsum(nt_flat)
    starts = ends - nt_flat
    total = ends[-1]
    step = jnp.minimum(jnp.arange(MAX_ROW_TILES, dtype=I32), total - 1)
    seg = jnp.sum((ends[None, :] <= step[:, None]).astype(I32), axis=1)
    in_seg = seg[:, None] == jnp.arange(N_EXPERTS * N_CHUNKS, dtype=I32)[None, :]
    k = step - jnp.sum(jnp.where(in_seg, starts[None, :], 0), axis=1)
    c = seg % N_CHUNKS
    tile_off_flat = tile_off.T.reshape(-1)
    tile_blk = c * (CHUNK_ROWS // ROW_TILE) + jnp.sum(jnp.where(in_seg, tile_off_flat[None, :], 0), axis=1) + k
    e_count = jnp.sum(n_tiles, axis=0)
    e_start = jnp.cumsum(e_count) - e_count
    g_count = (e_count + GROUP - 1) // GROUP
    g_end = jnp.cumsum(g_count)
    g_start = g_end - g_count
    slot_id = jnp.arange(MAX_GROUPS * GROUP, dtype=I32)
    q = slot_id // GROUP
    owner = jnp.sum((g_end[None, :] <= q[:, None]).astype(I32), axis=1)
    is_owner = owner[:, None] == jnp.arange(N_EXPERTS, dtype=I32)[None, :]
    sel = lambda v: jnp.sum(jnp.where(is_owner, v[None, :], 0), axis=1)
    k = (q - sel(g_start)) * GROUP + slot_id % GROUP
    valid = (owner < N_EXPERTS) & (k < sel(e_count))
    flat = sel(e_start) + k
    pick_tile = flat[:, None] == jnp.arange(MAX_ROW_TILES, dtype=I32)[None, :]
    group_tiles = jnp.where(valid, jnp.sum(jnp.where(pick_tile, tile_blk[None, :], 0), axis=1), -1)
    group_start = jnp.concatenate([g_start, g_end[-1:]])
    return a0, a1, group_tiles.astype(I32), group_start.astype(I32)


def _layer_consts(l, norm1_g, norm2_g, w_in, conv_w, conv_b, sp_norm_g, sp_w, sp_b, mix_out_g,
                  w_out, shared):
    pos = jnp.arange(SP_BLOCK)
    mask = (pos[None, :] // CHUNK) <= (pos[:, None] // CHUNK)
    w = jnp.where(mask[None], sp_w[l], 0.0)
    w_pairs = jnp.concatenate([w[0::2], w[1::2]], axis=-1).astype(BF16)
    sp_bias = jnp.repeat(sp_b[l].T, HEAD_DIM, axis=1)
    return (
        norm1_g[l].reshape(1, D_MODEL), norm2_g[l].reshape(1, D_MODEL),
        w_in[l].astype(BF16),
        conv_w[l], conv_b[l].reshape(1, D_CONV), sp_norm_g[l].reshape(1, D_SPATIAL),
        w_pairs, sp_bias,
        mix_out_g[l].reshape(1, D_MODEL),
        w_out[l].astype(BF16),
    ) + shared


def kernel(x, c, w_ada, b_ada, norm1_g, norm2_g, w_in, conv_w, conv_b, sp_norm_g, sp_w, sp_b, mix_out_g, w_out, router_w, router_b, exp_w_gate, exp_w_up, exp_w_down, final_g):
    ada = _ada_table(c, w_ada, b_ada).reshape(DEPTH, BATCH, N_ADA, D_MODEL)

    rw_hi = router_w.astype(BF16)
    rw_lo = (router_w - rw_hi.astype(F32)).astype(BF16)
    rw_cat = jnp.pad(jnp.concatenate([rw_hi, rw_lo], axis=1), ((0, 0), (0, LANES - 2 * N_EXPERTS)))
    head_of = jnp.arange(D_MODEL) // HEAD_DIM
    hred = (head_of[:, None] == jnp.arange(LANES)[None, :]).astype(F32) / HEAD_DIM
    hexp_half = (jnp.arange(LANES)[:, None] == head_of[None, :]).astype(F32)
    hexp = jnp.concatenate([hexp_half, hexp_half], axis=0)
    tri = (jnp.arange(SB)[:, None] < jnp.arange(SB)[None, :])
    shared = (rw_cat, router_b.reshape(N_EXPERTS, 1), hred.astype(BF16), hexp.astype(BF16),
              tri.astype(BF16))
    fg = final_g.reshape(1, D_MODEL)

    xc = x.reshape(N_TOK, D_MODEL)
    for l in range(DEPTH):
        consts = _layer_consts(l, norm1_g, norm2_g, w_in, conv_w, conv_b, sp_norm_g, sp_w, sp_b,
                               mix_out_g, w_out, shared)
        x1, h2p, rt, cnt = _mix_call(xc, ada[l], consts)
        a0, a1, group_tiles, group_start = _routing_tables(rt, cnt)
        xs = _dispatch_call(a0, a1, h2p.reshape(N_TOK * ROW_TILES, LANES))
        ys = _expert_call(l, group_tiles, group_start,
                          xs.reshape(TOTAL_ROWS // SUBLANES, ROW_TILES, SUBLANES, LANES),
                          exp_w_gate, exp_w_up, exp_w_down)
        wcol = rt[2:4].T
        xc = _combine_call(a0, a1, ys.reshape(TOTAL_ROWS * ROW_TILES, LANES), x1, wcol, ada[l], fg,
                           final=(l == DEPTH - 1))
    return xc.reshape(BATCH, SEQ, D_MODEL)
```

```python
import functools

import jax
import jax.numpy as jnp
from jax import lax
from jax.experimental import pallas as pl
from jax.experimental.pallas import tpu as pltpu

F32 = jnp.float32
BF16 = jnp.bfloat16
U32 = jnp.uint32
I32 = jnp.int32

D_MODEL = 1024
BATCH = 8
SEQ = 2048
DEPTH = 4
N_TOK = BATCH * SEQ
CHUNK = 64
HEAD_DIM = 64
D_CONV = 512
D_SPATIAL = 512
N_SP_HEADS = 8
N_MIX_HEADS = 16
SP_BLOCK = 128
N_EXPERTS = 16
N_GROUPS = 4
PER_GROUP = 4
D_EXPERT = 512
N_ADA = 6
EPS = 1e-6

LANES = 128
SUBLANES = 8
HALF = D_MODEL // 2
ROW_TILES = HALF // LANES

TM = 1024
SB = 256
TILES_PER_SEQ = SEQ // TM
ROUTE_CHUNK = 4096
N_CHUNKS = N_TOK // ROUTE_CHUNK
TILES_PER_CHUNK = ROUTE_CHUNK // TM
ROW_TILE = 64
ROW_GROUPS = ROW_TILE // SUBLANES
GROUP = 8
GROUP_WIDTHS = (8, 6, 4, 2)
CHUNK_ROWS = 2 * ROUTE_CHUNK + N_EXPERTS * ROW_TILE
TOTAL_ROWS = N_CHUNKS * CHUNK_ROWS
MAX_ROW_TILES = TOTAL_ROWS // ROW_TILE
MAX_GROUPS = MAX_ROW_TILES // GROUP + N_EXPERTS
TD = 2048
TC = 512
COMBINE_STEPS = ROUTE_CHUNK // TC
VMEM_LIMIT = 58 * 1024 * 1024


def _gelu_tanh(x):
    c = 0.7978845608028654
    return 0.5 * x * (1.0 + jnp.tanh(c * (x + 0.044715 * (x * x * x))))


def _split_bf16(x):
    hi = x.astype(BF16)
    lo = (x - hi.astype(F32)).astype(BF16)
    return hi, lo


def _pack_halves(x):
    return pltpu.pack_elementwise([x[:, :HALF], x[:, HALF:]], packed_dtype=BF16)


def _unpack_halves(words):
    lo = pltpu.unpack_elementwise(words, index=0, packed_dtype=BF16, unpacked_dtype=F32)
    hi = pltpu.unpack_elementwise(words, index=1, packed_dtype=BF16, unpacked_dtype=F32)
    return lo, hi


def _ada_kernel(c_ref, w_ref, b_ref, o_ref):
    c = c_ref[...]
    ca = c * (1.0 / (1.0 + jnp.exp(-c)))
    c_hi, c_lo = _split_bf16(ca)
    w_hi, w_lo = _split_bf16(w_ref[0])
    acc = jnp.dot(c_hi, w_hi, preferred_element_type=F32)
    acc += jnp.dot(c_lo, w_hi, preferred_element_type=F32)
    acc += jnp.dot(c_hi, w_lo, preferred_element_type=F32)
    o_ref[0] = acc + b_ref[0]


def _ada_table(c, w_ada, b_ada):
    n_blk = N_ADA
    return pl.pallas_call(
        _ada_kernel,
        grid=(DEPTH, n_blk),
        in_specs=[
            pl.BlockSpec((BATCH, D_MODEL), lambda l, n: (0, 0)),
            pl.BlockSpec((1, D_MODEL, D_MODEL), lambda l, n: (l, 0, n)),
            pl.BlockSpec((1, 1, D_MODEL), lambda l, n: (l, 0, n)),
        ],
        out_specs=pl.BlockSpec((1, BATCH, D_MODEL), lambda l, n: (l, 0, n)),
        out_shape=jax.ShapeDtypeStruct((DEPTH, BATCH, N_ADA * D_MODEL), F32),
        name="ada_table",
    )(c, w_ada, b_ada.reshape(DEPTH, 1, N_ADA * D_MODEL))


def _mix_kernel(x_ref, ada_ref, n1g_ref, n2g_ref, win_ref, cw_ref, cb_ref, spg_ref, spw_ref,
                spbias_ref, wout_ref, rw_ref, rb_ref, hred_ref, hexp_ref,
                tri_ref, x1_ref, h2p_ref, rt_ref, cnt_ref, z_scr, cnt_scr):
    i = pl.program_id(0)

    @pl.when(i % TILES_PER_SEQ == 0)
    def _():
        z_scr[pl.ds(0, SUBLANES), :] = jnp.zeros((SUBLANES, D_CONV), F32)

    @pl.when(i % TILES_PER_CHUNK == 0)
    def _():
        cnt_scr[...] = jnp.zeros((N_EXPERTS, LANES), F32)

    chains = [
        _mix_rows(sb * SB, x_ref, ada_ref, n1g_ref, n2g_ref, win_ref, cw_ref, cb_ref, spg_ref,
                  spw_ref, spbias_ref, wout_ref, rw_ref, rb_ref, hred_ref,
                  hexp_ref, tri_ref, x1_ref, h2p_ref, rt_ref, z_scr, cnt_scr)
        for sb in range(TM // SB)]
    for c in MIX_STAGE_ORDER:
        next(chains[c])
    z_scr[pl.ds(0, SUBLANES), :] = z_scr[pl.ds(TM, SUBLANES), :]
    cnt_ref[0] = cnt_scr[...]


MIX_STAGES = 8
MIX_SKEW = 1
MIX_STAGE_ORDER = tuple(
    c for t in range(MIX_STAGES + MIX_SKEW * (TM // SB - 1)) for c in range(TM // SB)
    if 0 <= t - MIX_SKEW * c < MIX_STAGES)


def _mix_rows(r0, x_ref, ada_ref, n1g_ref, n2g_ref, win_ref, cw_ref, cb_ref, spg_ref, spw_ref,
              spbias_ref, wout_ref, rw_ref, rb_ref, hred_ref, hexp_ref,
              tri_ref, x1_ref, h2p_ref, rt_ref, z_scr, cnt_scr):
    x = x_ref[pl.ds(r0, SB), :]
    ada = ada_ref[0]
    sh1, sc1, g1 = ada[0:1], ada[1:2], ada[2:3]
    sh2, sc2 = ada[3:4], ada[4:5]

    rinv = lax.rsqrt(jnp.mean(x * x, axis=-1, keepdims=True) + EPS)
    h = (x * rinv) * (n1g_ref[...] * (1.0 + sc1)) + sh1
    hb = h.astype(BF16)
    yield

    def proj(k):
        return jnp.dot(hb, win_ref[:, k * 512:(k + 1) * 512], preferred_element_type=F32)

    z = proj(1) * proj(2)
    z_scr[pl.ds(SUBLANES + r0, SB), :] = z
    z1 = z_scr[pl.ds(SUBLANES + r0 - 1, SB), :]
    z2 = z_scr[pl.ds(SUBLANES + r0 - 2, SB), :]
    yield
    p_b, p_u, p_v = proj(0), proj(3), proj(4)
    yield
    cw = cw_ref[...]
    acc = cb_ref[...] + cw[0:1] * z2
    acc = acc + cw[1:2] * z1
    acc = acc + cw[2:3] * z
    y_conv = p_b * acc

    ug = _gelu_tanh(p_u)
    vg = _gelu_tanh(p_v)
    vr = lax.rsqrt(jnp.mean(vg * vg, axis=-1, keepdims=True) + EPS)
    vn = (vg * vr * spg_ref[...]).astype(BF16)
    lane = lax.broadcasted_iota(I32, (1, LANES), 1)
    first_head = lane < HEAD_DIM
    zero = jnp.zeros((), BF16)
    s_rows = []
    for blk in range(SB // SP_BLOCK):
        cols = []
        for p in range(N_SP_HEADS // 2):
            vv = vn[blk * SP_BLOCK:(blk + 1) * SP_BLOCK, p * LANES:(p + 1) * LANES]
            rhs = jnp.concatenate([jnp.where(first_head, vv, zero), jnp.where(first_head, zero, vv)], axis=0)
            cols.append(jnp.dot(spw_ref[p], rhs, preferred_element_type=F32))
        s_rows.append(jnp.concatenate(cols, axis=-1) + spbias_ref[...])
    y_sp = ug * jnp.concatenate(s_rows, axis=0)
    yield

    y = jnp.concatenate([y_conv, y_sp], axis=-1)
    ms = jnp.dot((y * y).astype(BF16), hred_ref[...], preferred_element_type=F32)
    r_hi, r_lo = _split_bf16(lax.rsqrt(ms + EPS))
    scale = jnp.dot(jnp.concatenate([r_hi, r_lo], axis=-1), hexp_ref[...], preferred_element_type=F32)
    yn = (y * scale).astype(BF16)
    yield
    x1 = x + g1 * jnp.dot(yn, wout_ref[...], preferred_element_type=F32)
    x1_ref[pl.ds(r0, SB), :] = x1
    yield

    r2 = lax.rsqrt(jnp.mean(x1 * x1, axis=-1, keepdims=True) + EPS)
    h2 = (x1 * r2) * (n2g_ref[...] * (1.0 + sc2)) + sh2
    h2_hi = h2.astype(BF16)
    packed = _pack_halves(h2_hi.astype(F32))
    for j in range(ROW_TILES):
        h2p_ref[pl.ds(r0 // SUBLANES, SB // SUBLANES), j, :, :] = (
            packed[:, j * LANES:(j + 1) * LANES].reshape(SB // SUBLANES, SUBLANES, LANES))
    yield

    logits_t = jnp.dot(h2_hi, rw_ref[...], preferred_element_type=F32).T
    lg = logits_t[0:N_EXPERTS, :] + logits_t[N_EXPERTS:2 * N_EXPERTS, :]
    scores = 1.0 / (1.0 + jnp.exp(-lg))
    sel = scores + rb_ref[...]

    gs = []
    for g in range(N_GROUPS):
        a, b, c, d = (sel[g * PER_GROUP + k:g * PER_GROUP + k + 1, :] for k in range(PER_GROUP))
        hi1, lo1 = jnp.maximum(a, b), jnp.minimum(a, b)
        hi2, lo2 = jnp.maximum(c, d), jnp.minimum(c, d)
        gs.append(jnp.maximum(hi1, hi2) + jnp.maximum(jnp.minimum(hi1, hi2), jnp.maximum(lo1, lo2)))
    best = gs[0]
    top_group = jnp.zeros((1, SB), I32)
    for g in range(1, N_GROUPS):
        better = gs[g] > best
        best = jnp.where(better, gs[g], best)
        top_group = jnp.where(better, g, top_group)

    eid_i = lax.broadcasted_iota(I32, (N_EXPERTS, SB), 0)
    eid = eid_i.astype(F32)
    neg_inf = jnp.float32(-jnp.inf)
    masked = jnp.where((eid_i // PER_GROUP) == top_group, sel, neg_inf)
    m1 = jnp.max(masked, axis=0, keepdims=True)
    i1 = jnp.min(jnp.where(masked == m1, eid, float(N_EXPERTS)), axis=0, keepdims=True)
    masked2 = jnp.where(eid == i1, neg_inf, masked)
    m2 = jnp.max(masked2, axis=0, keepdims=True)
    i2 = jnp.min(jnp.where(masked2 == m2, eid, float(N_EXPERTS)), axis=0, keepdims=True)
    pick1 = eid == i1
    pick2 = eid == i2
    w1 = jnp.sum(jnp.where(pick1, scores, 0.0), axis=0, keepdims=True)
    w2 = jnp.sum(jnp.where(pick2, scores, 0.0), axis=0, keepdims=True)
    wsum = w1 + w2
    c1 = w1 / wsum
    c2 = w2 / wsum

    assigned = jnp.where(pick1 | pick2, 1.0, 0.0)
    before = jnp.dot(assigned.astype(BF16), tri_ref[...], preferred_element_type=F32)
    carry = cnt_scr[...]
    rank = before + carry[:, 0:1]
    rank1 = jnp.sum(jnp.where(pick1, rank, 0.0), axis=0, keepdims=True)
    rank2 = jnp.sum(jnp.where(pick2, rank, 0.0), axis=0, keepdims=True)
    cnt_scr[...] = carry + jnp.sum(assigned, axis=1, keepdims=True)
    unused = jnp.zeros((1, SB), F32)
    for r, row in enumerate((i1, i2, c1, c2, rank1, rank2, unused, unused)):
        rt_ref[r:r + 1, pl.ds(r0, SB)] = row
    yield


def _mix_call(x2d, ada_l, consts):
    n_tiles = N_TOK // TM
    full = lambda shape: pl.BlockSpec(shape, lambda i: (0,) * len(shape))
    in_specs = [
        pl.BlockSpec((TM, D_MODEL), lambda i: (i, 0)),
        pl.BlockSpec((1, N_ADA, D_MODEL), lambda i: (i // TILES_PER_SEQ, 0, 0)),
        full((1, D_MODEL)), full((1, D_MODEL)),
        full((D_MODEL, 5 * 512)),
        full((3, D_CONV)), full((1, D_CONV)), full((1, D_SPATIAL)),
        full((N_SP_HEADS // 2, SP_BLOCK, 2 * SP_BLOCK)),
        full((SP_BLOCK, D_SPATIAL)),
        full((D_MODEL, D_MODEL)),
        full((D_MODEL, LANES)), full((N_EXPERTS, 1)),
        full((D_MODEL, LANES)), full((2 * LANES, D_MODEL)),
        full((SB, SB)),
    ]
    out_specs = [
        pl.BlockSpec((TM, D_MODEL), lambda i: (i, 0)),
        pl.BlockSpec((TM // SUBLANES, ROW_TILES, SUBLANES, LANES), lambda i: (i, 0, 0, 0)),
        pl.BlockSpec((SUBLANES, TM), lambda i: (0, i)),
        pl.BlockSpec((1, N_EXPERTS, LANES), lambda i: (i, 0, 0)),
    ]
    out_shape = [
        jax.ShapeDtypeStruct((N_TOK, D_MODEL), F32),
        jax.ShapeDtypeStruct((N_TOK // SUBLANES, ROW_TILES, SUBLANES, LANES), U32),
        jax.ShapeDtypeStruct((SUBLANES, N_TOK), F32),
        jax.ShapeDtypeStruct((n_tiles, N_EXPERTS, LANES), F32),
    ]
    return pl.pallas_call(
        _mix_kernel,
        grid=(n_tiles,),
        in_specs=in_specs,
        out_specs=out_specs,
        out_shape=out_shape,
        scratch_shapes=[pltpu.VMEM((TM + SUBLANES, D_CONV), F32), pltpu.VMEM((N_EXPERTS, LANES), F32)],
        compiler_params=pltpu.CompilerParams(
            dimension_semantics=("arbitrary",), vmem_limit_bytes=VMEM_LIMIT),
        name="mix",
    )(x2d, ada_l, *consts)


def _row_addr(t):
    return (t // SUBLANES) * (ROW_TILES * SUBLANES) + t % SUBLANES


def _dispatch_kernel(a0_ref, a1_ref, h2p_ref, xs_ref):
    c = pl.program_id(0)
    s = pl.program_id(1)

    @pl.when(s == 0)
    def _():
        xs_ref[...] = jnp.zeros(xs_ref.shape, U32)

    base = c * ROUTE_CHUNK + s * TD

    for k in range(TD):
        row = h2p_ref[pl.ds(_row_addr(k), ROW_TILES, stride=SUBLANES), :]
        xs_ref[pl.ds(a0_ref[base + k], ROW_TILES, stride=SUBLANES), :] = row
        xs_ref[pl.ds(a1_ref[base + k], ROW_TILES, stride=SUBLANES), :] = row


def _dispatch_call(a0, a1, h2p_2d):
    steps = ROUTE_CHUNK // TD
    return pl.pallas_call(
        _dispatch_kernel,
        grid_spec=pltpu.PrefetchScalarGridSpec(
            num_scalar_prefetch=2,
            grid=(N_CHUNKS, steps),
            in_specs=[pl.BlockSpec((TD * ROW_TILES, LANES), lambda c, s, a0, a1: (c * steps + s, 0))],
            out_specs=pl.BlockSpec((CHUNK_ROWS * ROW_TILES, LANES), lambda c, s, a0, a1: (c, 0)),
        ),
        out_shape=jax.ShapeDtypeStruct((TOTAL_ROWS * ROW_TILES, LANES), U32),
        compiler_params=pltpu.CompilerParams(
            dimension_semantics=("arbitrary", "arbitrary"), vmem_limit_bytes=VMEM_LIMIT),
        name="dispatch",
    )(a0, a1, h2p_2d)


def _expert_kernel(tb_ref, gst_ref, xs_hbm, wg_ref, wu_ref, wd_ref, ys_hbm,
                   wg_s, wu_s, wd_s, xbuf, ybuf, sem_in, sem_out):
    e = pl.program_id(0)
    g_first = gst_ref[e]
    g_end = gst_ref[e + 1]
    g_all = gst_ref[N_EXPERTS]

    def in_copy(q, t):
        rows = pl.ds(tb_ref[GROUP * q + t] * ROW_GROUPS, ROW_GROUPS)
        return pltpu.make_async_copy(xs_hbm.at[rows], xbuf.at[q % 2, t], sem_in.at[q % 2, t])

    def out_copy(q, t):
        rows = pl.ds(tb_ref[GROUP * q + t] * ROW_GROUPS, ROW_GROUPS)
        return pltpu.make_async_copy(ybuf.at[q % 2, t], ys_hbm.at[rows], sem_out.at[q % 2, t])

    def for_group(q, action, make):
        for t in range(GROUP):
            @pl.when(tb_ref[GROUP * q + t] >= 0)
            def _():
                if action == "start":
                    make(q, t).start(priority=1)
                else:
                    make(q, t).wait()

    @pl.when(e == 0)
    def _():
        xbuf[...] = jnp.zeros(xbuf.shape, U32)

        @pl.when(g_all > 0)
        def _():
            for_group(0, "start", in_copy)

    wg_s[...] = wg_ref[0, 0].astype(BF16)
    wu_s[...] = wu_ref[0, 0].astype(BF16)
    wd_s[...] = wd_ref[0, 0].astype(BF16)

    def compute(slot, tiles):
        los, his = [], []
        for j in range(ROW_TILES):
            words = jnp.concatenate(
                [xbuf[slot, t, :, j, :, :].reshape(ROW_TILE, LANES) for t in tiles], axis=0)
            lo, hi = _unpack_halves(words)
            los.append(lo)
            his.append(hi)
        xb = jnp.concatenate(los + his, axis=-1).astype(BF16)
        g = jnp.dot(xb, wg_s[...], preferred_element_type=F32)
        u = jnp.dot(xb, wu_s[...], preferred_element_type=F32)
        act = (g * (1.0 / (1.0 + jnp.exp(-g))) * u).astype(BF16)
        y = jnp.dot(act, wd_s[...], preferred_element_type=F32)
        packed = _pack_halves(y.astype(BF16).astype(F32))
        for n, t in enumerate(tiles):
            for j in range(ROW_TILES):
                ybuf[slot, t, :, j, :, :] = packed[n * ROW_TILE:(n + 1) * ROW_TILE,
                                                   j * LANES:(j + 1) * LANES].reshape(
                    ROW_GROUPS, SUBLANES, LANES)

    def body(q, carry):
        slot = q % 2
        for_group(q, "wait", in_copy)

        @pl.when(q + 1 < g_all)
        def _():
            for_group(q + 1, "start", in_copy)

        @pl.when(q >= 2)
        def _():
            for_group(q - 2, "wait", out_copy)

        n_valid = sum((tb_ref[GROUP * q + t] >= 0).astype(I32) for t in range(GROUP))
        for width, narrower in zip(GROUP_WIDTHS, GROUP_WIDTHS[1:] + (0,)):
            @pl.when((n_valid > narrower) & (n_valid <= width))
            def _():
                compute(slot, tuple(range(width)))
        for_group(q, "start", out_copy)
        return carry

    lax.fori_loop(g_first, g_end, body, 0)

    @pl.when(e == N_EXPERTS - 1)
    def _():
        @pl.when(g_all >= 2)
        def _():
            for_group(g_all - 2, "wait", out_copy)

        @pl.when(g_all >= 1)
        def _():
            for_group(g_all - 1, "wait", out_copy)


def _expert_call(layer, group_tiles, group_start, xs4, wg, wu, wd):
    group_buf = (2, GROUP, ROW_GROUPS, ROW_TILES, SUBLANES, LANES)
    return pl.pallas_call(
        _expert_kernel,
        grid_spec=pltpu.PrefetchScalarGridSpec(
            num_scalar_prefetch=2,
            grid=(N_EXPERTS,),
            in_specs=[
                pl.BlockSpec(memory_space=pl.ANY),
                pl.BlockSpec((1, 1, D_MODEL, D_EXPERT), lambda e, tb, gs: (layer, e, 0, 0)),
                pl.BlockSpec((1, 1, D_MODEL, D_EXPERT), lambda e, tb, gs: (layer, e, 0, 0)),
                pl.BlockSpec((1, 1, D_EXPERT, D_MODEL), lambda e, tb, gs: (layer, e, 0, 0)),
            ],
            out_specs=pl.BlockSpec(memory_space=pl.ANY),
            scratch_shapes=[
                pltpu.VMEM((D_MODEL, D_EXPERT), BF16),
                pltpu.VMEM((D_MODEL, D_EXPERT), BF16),
                pltpu.VMEM((D_EXPERT, D_MODEL), BF16),
                pltpu.VMEM(group_buf, U32),
                pltpu.VMEM(group_buf, U32),
                pltpu.SemaphoreType.DMA((2, GROUP)),
                pltpu.SemaphoreType.DMA((2, GROUP)),
            ],
        ),
        out_shape=jax.ShapeDtypeStruct((TOTAL_ROWS // SUBLANES, ROW_TILES, SUBLANES, LANES), U32),
        input_output_aliases={2: 0},
        compiler_params=pltpu.CompilerParams(
            dimension_semantics=("arbitrary",), vmem_limit_bytes=VMEM_LIMIT),
        name="experts",
    )(group_tiles, group_start, xs4, wg, wu, wd)


def _combine_kernel(a0_ref, a1_ref, ys_hbm, x1_ref, w_ref, ada_ref, fg_ref, out_ref,
                    ys_buf, r0_scr, r1_scr, sem, *, final):
    c = pl.program_id(0)
    s = pl.program_id(1)
    base = c * ROUTE_CHUNK + s * TC
    region = CHUNK_ROWS * ROW_TILES
    piece = region // COMBINE_STEPS

    def piece_copy(chunk, p):
        src = ys_hbm.at[pl.ds(chunk * region + p * piece, piece)]
        dst = ys_buf.at[pl.ds((chunk % 2) * region + p * piece, piece)]
        return pltpu.make_async_copy(src, dst, sem.at[chunk % 2, p])

    @pl.when((c == 0) & (s == 0))
    def _():
        for p in range(COMBINE_STEPS):
            piece_copy(0, p).start()
        for p in range(COMBINE_STEPS):
            piece_copy(0, p).wait()

    @pl.when(c + 1 < N_CHUNKS)
    def _():
        for p in range(COMBINE_STEPS):
            @pl.when(s == p)
            def _():
                piece_copy(c + 1, p).start()

    off = (c % 2) * region

    for k in range(TC):
        dst = pl.ds(_row_addr(k), ROW_TILES, stride=SUBLANES)
        r0_scr[dst, :] = ys_buf[pl.ds(off + a0_ref[base + k], ROW_TILES, stride=SUBLANES), :]
        r1_scr[dst, :] = ys_buf[pl.ds(off + a1_ref[base + k], ROW_TILES, stride=SUBLANES), :]

    w = w_ref[...]
    w0 = w[:, 0:1]
    w1 = w[:, 1:2]
    g2 = ada_ref[0][5:6]
    n_grp = TC // SUBLANES
    los, his = [], []
    for j in range(ROW_TILES):
        parts0 = [r0_scr[pl.ds((tb * ROW_TILES + j) * SUBLANES, SUBLANES), :] for tb in range(n_grp)]
        parts1 = [r1_scr[pl.ds((tb * ROW_TILES + j) * SUBLANES, SUBLANES), :] for tb in range(n_grp)]
        lo0, hi0 = _unpack_halves(jnp.concatenate(parts0, axis=0))
        lo1, hi1 = _unpack_halves(jnp.concatenate(parts1, axis=0))
        los.append(w0 * lo0 + w1 * lo1)
        his.append(w0 * hi0 + w1 * hi1)
    x2 = x1_ref[...] + g2 * jnp.concatenate(los + his, axis=-1)
    if final:
        r = lax.rsqrt(jnp.mean(x2 * x2, axis=-1, keepdims=True) + EPS)
        x2 = x2 * r * fg_ref[...]
    out_ref[...] = x2

    @pl.when((s == COMBINE_STEPS - 1) & (c + 1 < N_CHUNKS))
    def _():
        for p in range(COMBINE_STEPS):
            piece_copy(c + 1, p).wait()


def _combine_call(a0, a1, ys_2d, x1, wcol, ada_l, final_g, final):
    tok = lambda c, s, a0, a1: (c * COMBINE_STEPS + s, 0)
    return pl.pallas_call(
        functools.partial(_combine_kernel, final=final),
        grid_spec=pltpu.PrefetchScalarGridSpec(
            num_scalar_prefetch=2,
            grid=(N_CHUNKS, COMBINE_STEPS),
            in_specs=[
                pl.BlockSpec(memory_space=pl.ANY),
                pl.BlockSpec((TC, D_MODEL), tok),
                pl.BlockSpec((TC, 2), tok),
                pl.BlockSpec((1, N_ADA, D_MODEL),
                             lambda c, s, a0, a1: ((c * ROUTE_CHUNK + s * TC) // SEQ, 0, 0)),
                pl.BlockSpec((1, D_MODEL), lambda c, s, a0, a1: (0, 0)),
            ],
            out_specs=pl.BlockSpec((TC, D_MODEL), tok),
            scratch_shapes=[pltpu.VMEM((2 * CHUNK_ROWS * ROW_TILES, LANES), U32),
                            pltpu.VMEM((TC * ROW_TILES, LANES), U32),
                            pltpu.VMEM((TC * ROW_TILES, LANES), U32),
                            pltpu.SemaphoreType.DMA((2, COMBINE_STEPS))],
        ),
        out_shape=jax.ShapeDtypeStruct((N_TOK, D_MODEL), F32),
        compiler_params=pltpu.CompilerParams(
            dimension_semantics=("arbitrary", "arbitrary"), vmem_limit_bytes=VMEM_LIMIT),
        name="combine_final" if final else "combine",
    )(a0, a1, ys_2d, x1, wcol, ada_l, final_g)


def _routing_tables(rt, cnt):
    e0 = rt[0].astype(I32)
    e1 = rt[1].astype(I32)
    rank0 = rt[4].astype(I32)
    rank1 = rt[5].astype(I32)
    counts = cnt[TILES_PER_CHUNK - 1::TILES_PER_CHUNK, :, 0].astype(I32)
    n_tiles = (counts + ROW_TILE - 1) // ROW_TILE
    tile_off = jnp.cumsum(n_tiles, axis=1) - n_tiles
    row_off = tile_off * ROW_TILE
    experts = jnp.arange(N_EXPERTS, dtype=I32)[None, :, None]
    off_tok = row_off[:, :, None]
    pick = lambda e: jnp.sum(
        jnp.where(e.reshape(N_CHUNKS, 1, ROUTE_CHUNK) == experts, off_tok, 0), axis=1).reshape(N_TOK)
    slot0 = pick(e0) + rank0
    slot1 = pick(e1) + rank1
    a0 = _row_addr(slot0)
    a1 = _row_addr(slot1)

    nt_flat = n_tiles.T.reshape(-1)
    ends = jnp.cumsum(nt_flat)
    starts = ends - nt_flat
    total = ends[-1]
    step = jnp.minimum(jnp.arange(MAX_ROW_TILES, dtype=I32), total - 1)
    seg = jnp.sum((ends[None, :] <= step[:, None]).astype(I32), axis=1)
    in_seg = seg[:, None] == jnp.arange(N_EXPERTS * N_CHUNKS, dtype=I32)[None, :]
    k = step - jnp.sum(jnp.where(in_seg, starts[None, :], 0), axis=1)
    c = seg % N_CHUNKS
    tile_off_flat = tile_off.T.reshape(-1)
    tile_blk = c * (CHUNK_ROWS // ROW_TILE) + jnp.sum(jnp.where(in_seg, tile_off_flat[None, :], 0), axis=1) + k
    e_count = jnp.sum(n_tiles, axis=0)
    e_start = jnp.cumsum(e_count) - e_count
    g_count = (e_count + GROUP - 1) // GROUP
    g_end = jnp.cumsum(g_count)
    g_start = g_end - g_count
    slot_id = jnp.arange(MAX_GROUPS * GROUP, dtype=I32)
    q = slot_id // GROUP
    owner = jnp.sum((g_end[None, :] <= q[:, None]).astype(I32), axis=1)
    is_owner = owner[:, None] == jnp.arange(N_EXPERTS, dtype=I32)[None, :]
    sel = lambda v: jnp.sum(jnp.where(is_owner, v[None, :], 0), axis=1)
    k = (q - sel(g_start)) * GROUP + slot_id % GROUP
    valid = (owner < N_EXPERTS) & (k < sel(e_count))
    flat = sel(e_start) + k
    pick_tile = flat[:, None] == jnp.arange(MAX_ROW_TILES, dtype=I32)[None, :]
    group_tiles = jnp.where(valid, jnp.sum(jnp.where(pick_tile, tile_blk[None, :], 0), axis=1), -1)
    group_start = jnp.concatenate([g_start, g_end[-1:]])
    return a0, a1, group_tiles.astype(I32), group_start.astype(I32)


def _layer_consts(l, norm1_g, norm2_g, w_in, conv_w, conv_b, sp_norm_g, sp_w, sp_b, mix_out_g,
                  w_out, shared):
    pos = jnp.arange(SP_BLOCK)
    mask = (pos[None, :] // CHUNK) <= (pos[:, None] // CHUNK)
    w = jnp.where(mask[None], sp_w[l], 0.0)
    w_pairs = jnp.concatenate([w[0::2], w[1::2]], axis=-1).astype(BF16)
    sp_bias = jnp.repeat(sp_b[l].T, HEAD_DIM, axis=1)
    return (
        norm1_g[l].reshape(1, D_MODEL), norm2_g[l].reshape(1, D_MODEL),
        w_in[l].astype(BF16),
        conv_w[l], conv_b[l].reshape(1, D_CONV), sp_norm_g[l].reshape(1, D_SPATIAL),
        w_pairs, sp_bias,
        (mix_out_g[l][:, None] * w_out[l]).astype(BF16),
    ) + shared


def kernel(x, c, w_ada, b_ada, norm1_g, norm2_g, w_in, conv_w, conv_b, sp_norm_g, sp_w, sp_b, mix_out_g, w_out, router_w, router_b, exp_w_gate, exp_w_up, exp_w_down, final_g):
    ada = _ada_table(c, w_ada, b_ada).reshape(DEPTH, BATCH, N_ADA, D_MODEL)

    rw_hi = router_w.astype(BF16)
    rw_lo = (router_w - rw_hi.astype(F32)).astype(BF16)
    rw_cat = jnp.pad(jnp.concatenate([rw_hi, rw_lo], axis=1), ((0, 0), (0, LANES - 2 * N_EXPERTS)))
    head_of = jnp.arange(D_MODEL) // HEAD_DIM
    hred = (head_of[:, None] == jnp.arange(LANES)[None, :]).astype(F32) / HEAD_DIM
    hexp_half = (jnp.arange(LANES)[:, None] == head_of[None, :]).astype(F32)
    hexp = jnp.concatenate([hexp_half, hexp_half], axis=0)
    tri = (jnp.arange(SB)[:, None] < jnp.arange(SB)[None, :])
    shared = (rw_cat, router_b.reshape(N_EXPERTS, 1), hred.astype(BF16), hexp.astype(BF16),
              tri.astype(BF16))
    fg = final_g.reshape(1, D_MODEL)

    xc = x.reshape(N_TOK, D_MODEL)
    for l in range(DEPTH):
        consts = _layer_consts(l, norm1_g, norm2_g, w_in, conv_w, conv_b, sp_norm_g, sp_w, sp_b,
                               mix_out_g, w_out, shared)
        x1, h2p, rt, cnt = _mix_call(xc, ada[l], consts)
        a0, a1, group_tiles, group_start = _routing_tables(rt, cnt)
        xs = _dispatch_call(a0, a1, h2p.reshape(N_TOK * ROW_TILES, LANES))
        ys = _expert_call(l, group_tiles, group_start,
                          xs.reshape(TOTAL_ROWS // SUBLANES, ROW_TILES, SUBLANES, LANES),
                          exp_w_gate, exp_w_up, exp_w_down)
        wcol = rt[2:4].T
        xc = _combine_call(a0, a1, ys.reshape(TOTAL_ROWS * ROW_TILES, LANES), x1, wcol, ada[l], fg,
                           final=(l == DEPTH - 1))
    return xc.reshape(BATCH, SEQ, D_MODEL)
```

```python
import functools

import jax
import jax.numpy as jnp
from jax import lax
from jax.experimental import pallas as pl
from jax.experimental.pallas import tpu as pltpu

F32 = jnp.float32
BF16 = jnp.bfloat16
U32 = jnp.uint32
I32 = jnp.int32

D_MODEL = 1024
BATCH = 8
SEQ = 2048
DEPTH = 4
N_TOK = BATCH * SEQ
CHUNK = 64
HEAD_DIM = 64
D_CONV = 512
D_SPATIAL = 512
N_SP_HEADS = 8
N_MIX_HEADS = 16
SP_BLOCK = 128
N_EXPERTS = 16
N_GROUPS = 4
PER_GROUP = 4
D_EXPERT = 512
N_ADA = 6
EPS = 1e-6

LANES = 128
SUBLANES = 8
HALF = D_MODEL // 2
ROW_TILES = HALF // LANES

TM = 1024
SB = 512
TILES_PER_SEQ = SEQ // TM
ROUTE_CHUNK = 4096
N_CHUNKS = N_TOK // ROUTE_CHUNK
TILES_PER_CHUNK = ROUTE_CHUNK // TM
ROW_TILE = 64
ROW_GROUPS = ROW_TILE // SUBLANES
GROUP = 8
GROUP_WIDTHS = (8, 6, 4, 2)
CHUNK_ROWS = 2 * ROUTE_CHUNK + N_EXPERTS * ROW_TILE
TOTAL_ROWS = N_CHUNKS * CHUNK_ROWS
MAX_ROW_TILES = TOTAL_ROWS // ROW_TILE
MAX_GROUPS = MAX_ROW_TILES // GROUP + N_EXPERTS
TD = 2048
TC = 512
COMBINE_STEPS = ROUTE_CHUNK // TC
VMEM_LIMIT = 58 * 1024 * 1024


def _gelu_tanh(x):
    c = 0.7978845608028654
    return 0.5 * x * (1.0 + jnp.tanh(c * (x + 0.044715 * (x * x * x))))


def _split_bf16(x):
    hi = x.astype(BF16)
    lo = (x - hi.astype(F32)).astype(BF16)
    return hi, lo


def _pack_halves(x):
    return pltpu.pack_elementwise([x[:, :HALF], x[:, HALF:]], packed_dtype=BF16)


def _unpack_halves(words):
    lo = pltpu.unpack_elementwise(words, index=0, packed_dtype=BF16, unpacked_dtype=F32)
    hi = pltpu.unpack_elementwise(words, index=1, packed_dtype=BF16, unpacked_dtype=F32)
    return lo, hi


def _ada_kernel(c_ref, w_ref, b_ref, o_ref):
    c = c_ref[...]
    ca = c * (1.0 / (1.0 + jnp.exp(-c)))
    c_hi, c_lo = _split_bf16(ca)
    w_hi, w_lo = _split_bf16(w_ref[0])
    acc = jnp.dot(c_hi, w_hi, preferred_element_type=F32)
    acc += jnp.dot(c_lo, w_hi, preferred_element_type=F32)
    acc += jnp.dot(c_hi, w_lo, preferred_element_type=F32)
    o_ref[0] = acc + b_ref[0]


def _ada_table(c, w_ada, b_ada):
    width = 2 * D_MODEL
    return pl.pallas_call(
        _ada_kernel,
        grid=(DEPTH, N_ADA * D_MODEL // width),
        in_specs=[
            pl.BlockSpec((BATCH, D_MODEL), lambda l, n: (0, 0)),
            pl.BlockSpec((1, D_MODEL, width), lambda l, n: (l, 0, n)),
            pl.BlockSpec((1, 1, width), lambda l, n: (l, 0, n)),
        ],
        out_specs=pl.BlockSpec((1, BATCH, width), lambda l, n: (l, 0, n)),
        out_shape=jax.ShapeDtypeStruct((DEPTH, BATCH, N_ADA * D_MODEL), F32),
        name="ada_table",
    )(c, w_ada, b_ada.reshape(DEPTH, 1, N_ADA * D_MODEL))


def _mix_kernel(x_ref, ada_ref, n1g_ref, n2g_ref, win_ref, cw_ref, cb_ref, spg_ref, spw_ref,
                spbias_ref, wout_ref, rw_ref, rb_ref, hred_ref, hexp_ref,
                tri_ref, x1_ref, h2p_ref, rt_ref, cnt_ref, z_scr, cnt_scr):
    i = pl.program_id(0)

    @pl.when(i % TILES_PER_SEQ == 0)
    def _():
        z_scr[pl.ds(0, SUBLANES), :] = jnp.zeros((SUBLANES, D_CONV), F32)

    @pl.when(i % TILES_PER_CHUNK == 0)
    def _():
        cnt_scr[...] = jnp.zeros((N_EXPERTS, LANES), F32)

    chains = [
        _mix_rows(sb * SB, x_ref, ada_ref, n1g_ref, n2g_ref, win_ref, cw_ref, cb_ref, spg_ref,
                  spw_ref, spbias_ref, wout_ref, rw_ref, rb_ref, hred_ref,
                  hexp_ref, tri_ref, x1_ref, h2p_ref, rt_ref, z_scr, cnt_scr)
        for sb in range(TM // SB)]
    for c in MIX_STAGE_ORDER:
        next(chains[c])
    z_scr[pl.ds(0, SUBLANES), :] = z_scr[pl.ds(TM, SUBLANES), :]
    cnt_ref[0] = cnt_scr[...]


MIX_STAGES = 8
MIX_SKEW = 1
MIX_STAGE_ORDER = tuple(
    c for t in range(MIX_STAGES + MIX_SKEW * (TM // SB - 1)) for c in range(TM // SB)
    if 0 <= t - MIX_SKEW * c < MIX_STAGES)


def _mix_rows(r0, x_ref, ada_ref, n1g_ref, n2g_ref, win_ref, cw_ref, cb_ref, spg_ref, spw_ref,
              spbias_ref, wout_ref, rw_ref, rb_ref, hred_ref, hexp_ref,
              tri_ref, x1_ref, h2p_ref, rt_ref, z_scr, cnt_scr):
    x = x_ref[pl.ds(r0, SB), :]
    ada = ada_ref[0]
    sh1, sc1, g1 = ada[0:1], ada[1:2], ada[2:3]
    sh2, sc2 = ada[3:4], ada[4:5]

    rinv = lax.rsqrt(jnp.mean(x * x, axis=-1, keepdims=True) + EPS)
    h = (x * rinv) * (n1g_ref[...] * (1.0 + sc1)) + sh1
    hb = h.astype(BF16)
    yield

    def proj(k):
        return jnp.dot(hb, win_ref[:, k * 512:(k + 1) * 512], preferred_element_type=F32)

    z = proj(1) * proj(2)
    z_scr[pl.ds(SUBLANES + r0, SB), :] = z
    z1 = z_scr[pl.ds(SUBLANES + r0 - 1, SB), :]
    z2 = z_scr[pl.ds(SUBLANES + r0 - 2, SB), :]
    yield
    p_b, p_u, p_v = proj(0), proj(3), proj(4)
    yield
    cw = cw_ref[...]
    acc = cb_ref[...] + cw[0:1] * z2
    acc = acc + cw[1:2] * z1
    acc = acc + cw[2:3] * z
    y_conv = p_b * acc

    ug = _gelu_tanh(p_u)
    vg = _gelu_tanh(p_v)
    vr = lax.rsqrt(jnp.mean(vg * vg, axis=-1, keepdims=True) + EPS)
    vn = (vg * vr * spg_ref[...]).astype(BF16)
    lane = lax.broadcasted_iota(I32, (1, LANES), 1)
    first_head = lane < HEAD_DIM
    zero = jnp.zeros((), BF16)
    s_rows = []
    for blk in range(SB // SP_BLOCK):
        cols = []
        for p in range(N_SP_HEADS // 2):
            vv = vn[blk * SP_BLOCK:(blk + 1) * SP_BLOCK, p * LANES:(p + 1) * LANES]
            rhs = jnp.concatenate([jnp.where(first_head, vv, zero), jnp.where(first_head, zero, vv)], axis=0)
            cols.append(jnp.dot(spw_ref[p], rhs, preferred_element_type=F32))
        s_rows.append(jnp.concatenate(cols, axis=-1) + spbias_ref[...])
    y_sp = ug * jnp.concatenate(s_rows, axis=0)
    yield

    y = jnp.concatenate([y_conv, y_sp], axis=-1)
    ms = jnp.dot((y * y).astype(BF16), hred_ref[...], preferred_element_type=F32)
    r_hi, r_lo = _split_bf16(lax.rsqrt(ms + EPS))
    scale = jnp.dot(jnp.concatenate([r_hi, r_lo], axis=-1), hexp_ref[...], preferred_element_type=F32)
    yn = (y * scale).astype(BF16)
    yield
    x1 = x + g1 * jnp.dot(yn, wout_ref[...], preferred_element_type=F32)
    x1_ref[pl.ds(r0, SB), :] = x1
    yield

    r2 = lax.rsqrt(jnp.mean(x1 * x1, axis=-1, keepdims=True) + EPS)
    h2 = (x1 * r2) * (n2g_ref[...] * (1.0 + sc2)) + sh2
    h2_hi = h2.astype(BF16)
    packed = _pack_halves(h2_hi.astype(F32))
    for j in range(ROW_TILES):
        h2p_ref[pl.ds(r0 // SUBLANES, SB // SUBLANES), j, :, :] = (
            packed[:, j * LANES:(j + 1) * LANES].reshape(SB // SUBLANES, SUBLANES, LANES))
    yield

    logits_t = jnp.dot(h2_hi, rw_ref[...], preferred_element_type=F32).T
    lg = logits_t[0:N_EXPERTS, :] + logits_t[N_EXPERTS:2 * N_EXPERTS, :]
    scores = 1.0 / (1.0 + jnp.exp(-lg))
    sel = scores + rb_ref[...]

    gs = []
    for g in range(N_GROUPS):
        a, b, c, d = (sel[g * PER_GROUP + k:g * PER_GROUP + k + 1, :] for k in range(PER_GROUP))
        hi1, lo1 = jnp.maximum(a, b), jnp.minimum(a, b)
        hi2, lo2 = jnp.maximum(c, d), jnp.minimum(c, d)
        gs.append(jnp.maximum(hi1, hi2) + jnp.maximum(jnp.minimum(hi1, hi2), jnp.maximum(lo1, lo2)))
    best = gs[0]
    top_group = jnp.zeros((1, SB), I32)
    for g in range(1, N_GROUPS):
        better = gs[g] > best
        best = jnp.where(better, gs[g], best)
        top_group = jnp.where(better, g, top_group)

    eid_i = lax.broadcasted_iota(I32, (N_EXPERTS, SB), 0)
    eid = eid_i.astype(F32)
    neg_inf = jnp.float32(-jnp.inf)
    masked = jnp.where((eid_i // PER_GROUP) == top_group, sel, neg_inf)
    m1 = jnp.max(masked, axis=0, keepdims=True)
    i1 = jnp.min(jnp.where(masked == m1, eid, float(N_EXPERTS)), axis=0, keepdims=True)
    masked2 = jnp.where(eid == i1, neg_inf, masked)
    m2 = jnp.max(masked2, axis=0, keepdims=True)
    i2 = jnp.min(jnp.where(masked2 == m2, eid, float(N_EXPERTS)), axis=0, keepdims=True)
    pick1 = eid == i1
    pick2 = eid == i2
    w1 = jnp.sum(jnp.where(pick1, scores, 0.0), axis=0, keepdims=True)
    w2 = jnp.sum(jnp.where(pick2, scores, 0.0), axis=0, keepdims=True)
    wsum = w1 + w2
    c1 = w1 / wsum
    c2 = w2 / wsum

    assigned = jnp.where(pick1 | pick2, 1.0, 0.0)
    before = jnp.dot(assigned.astype(BF16), tri_ref[...], preferred_element_type=F32)
    carry = cnt_scr[...]
    rank = before + carry[:, 0:1]
    rank1 = jnp.sum(jnp.where(pick1, rank, 0.0), axis=0, keepdims=True)
    rank2 = jnp.sum(jnp.where(pick2, rank, 0.0), axis=0, keepdims=True)
    cnt_scr[...] = carry + jnp.sum(assigned, axis=1, keepdims=True)
    unused = jnp.zeros((1, SB), F32)
    for r, row in enumerate((i1, i2, c1, c2, rank1, rank2, unused, unused)):
        rt_ref[r:r + 1, pl.ds(r0, SB)] = row
    yield


def _mix_call(x2d, ada_l, consts):
    n_tiles = N_TOK // TM
    full = lambda shape: pl.BlockSpec(shape, lambda i: (0,) * len(shape))
    in_specs = [
        pl.BlockSpec((TM, D_MODEL), lambda i: (i, 0)),
        pl.BlockSpec((1, N_ADA, D_MODEL), lambda i: (i // TILES_PER_SEQ, 0, 0)),
        full((1, D_MODEL)), full((1, D_MODEL)),
        full((D_MODEL, 5 * 512)),
        full((3, D_CONV)), full((1, D_CONV)), full((1, D_SPATIAL)),
        full((N_SP_HEADS // 2, SP_BLOCK, 2 * SP_BLOCK)),
        full((SP_BLOCK, D_SPATIAL)),
        full((D_MODEL, D_MODEL)),
        full((D_MODEL, LANES)), full((N_EXPERTS, 1)),
        full((D_MODEL, LANES)), full((2 * LANES, D_MODEL)),
        full((SB, SB)),
    ]
    out_specs = [
        pl.BlockSpec((TM, D_MODEL), lambda i: (i, 0)),
        pl.BlockSpec((TM // SUBLANES, ROW_TILES, SUBLANES, LANES), lambda i: (i, 0, 0, 0)),
        pl.BlockSpec((SUBLANES, TM), lambda i: (0, i)),
        pl.BlockSpec((1, N_EXPERTS, LANES), lambda i: (i, 0, 0)),
    ]
    out_shape = [
        jax.ShapeDtypeStruct((N_TOK, D_MODEL), F32),
        jax.ShapeDtypeStruct((N_TOK // SUBLANES, ROW_TILES, SUBLANES, LANES), U32),
        jax.ShapeDtypeStruct((SUBLANES, N_TOK), F32),
        jax.ShapeDtypeStruct((n_tiles, N_EXPERTS, LANES), F32),
    ]
    return pl.pallas_call(
        _mix_kernel,
        grid=(n_tiles,),
        in_specs=in_specs,
        out_specs=out_specs,
        out_shape=out_shape,
        scratch_shapes=[pltpu.VMEM((TM + SUBLANES, D_CONV), F32), pltpu.VMEM((N_EXPERTS, LANES), F32)],
        compiler_params=pltpu.CompilerParams(
            dimension_semantics=("arbitrary",), vmem_limit_bytes=VMEM_LIMIT),
        name="mix",
    )(x2d, ada_l, *consts)


def _row_addr(t):
    return (t // SUBLANES) * (ROW_TILES * SUBLANES) + t % SUBLANES


def _dispatch_kernel(a0_ref, a1_ref, h2p_ref, xs_ref):
    c = pl.program_id(0)
    s = pl.program_id(1)

    @pl.when(s == 0)
    def _():
        xs_ref[...] = jnp.zeros(xs_ref.shape, U32)

    base = c * ROUTE_CHUNK + s * TD

    for k in range(TD):
        row = h2p_ref[pl.ds(_row_addr(k), ROW_TILES, stride=SUBLANES), :]
        xs_ref[pl.ds(a0_ref[base + k], ROW_TILES, stride=SUBLANES), :] = row
        xs_ref[pl.ds(a1_ref[base + k], ROW_TILES, stride=SUBLANES), :] = row


def _dispatch_call(a0, a1, h2p_2d):
    steps = ROUTE_CHUNK // TD
    return pl.pallas_call(
        _dispatch_kernel,
        grid_spec=pltpu.PrefetchScalarGridSpec(
            num_scalar_prefetch=2,
            grid=(N_CHUNKS, steps),
            in_specs=[pl.BlockSpec((TD * ROW_TILES, LANES), lambda c, s, a0, a1: (c * steps + s, 0))],
            out_specs=pl.BlockSpec((CHUNK_ROWS * ROW_TILES, LANES), lambda c, s, a0, a1: (c, 0)),
        ),
        out_shape=jax.ShapeDtypeStruct((TOTAL_ROWS * ROW_TILES, LANES), U32),
        compiler_params=pltpu.CompilerParams(
            dimension_semantics=("arbitrary", "arbitrary"), vmem_limit_bytes=VMEM_LIMIT),
        name="dispatch",
    )(a0, a1, h2p_2d)


def _expert_kernel(tb_ref, gst_ref, xs_hbm, wg_ref, wu_ref, wd_ref, ys_hbm,
                   wg_s, wu_s, wd_s, xbuf, ybuf, sem_in, sem_out):
    e = pl.program_id(0)
    g_first = gst_ref[e]
    g_end = gst_ref[e + 1]
    g_all = gst_ref[N_EXPERTS]

    def in_copy(q, t):
        rows = pl.ds(tb_ref[GROUP * q + t] * ROW_GROUPS, ROW_GROUPS)
        return pltpu.make_async_copy(xs_hbm.at[rows], xbuf.at[q % 2, t], sem_in.at[q % 2, t])

    def out_copy(q, t):
        rows = pl.ds(tb_ref[GROUP * q + t] * ROW_GROUPS, ROW_GROUPS)
        return pltpu.make_async_copy(ybuf.at[q % 2, t], ys_hbm.at[rows], sem_out.at[q % 2, t])

    def for_group(q, action, make):
        for t in range(GROUP):
            @pl.when(tb_ref[GROUP * q + t] >= 0)
            def _():
                if action == "start":
                    make(q, t).start(priority=1)
                else:
                    make(q, t).wait()

    @pl.when(e == 0)
    def _():
        xbuf[...] = jnp.zeros(xbuf.shape, U32)

        @pl.when(g_all > 0)
        def _():
            for_group(0, "start", in_copy)

    wg_s[...] = wg_ref[0, 0].astype(BF16)
    wu_s[...] = wu_ref[0, 0].astype(BF16)
    wd_s[...] = wd_ref[0, 0].astype(BF16)

    def compute(slot, tiles):
        los, his = [], []
        for j in range(ROW_TILES):
            words = jnp.concatenate(
                [xbuf[slot, t, :, j, :, :].reshape(ROW_TILE, LANES) for t in tiles], axis=0)
            lo, hi = _unpack_halves(words)
            los.append(lo)
            his.append(hi)
        xb = jnp.concatenate(los + his, axis=-1).astype(BF16)
        g = jnp.dot(xb, wg_s[...], preferred_element_type=F32)
        u = jnp.dot(xb, wu_s[...], preferred_element_type=F32)
        act = (g * (1.0 / (1.0 + jnp.exp(-g))) * u).astype(BF16)
        y = jnp.dot(act, wd_s[...], preferred_element_type=F32)
        packed = _pack_halves(y.astype(BF16).astype(F32))
        for n, t in enumerate(tiles):
            for j in range(ROW_TILES):
                ybuf[slot, t, :, j, :, :] = packed[n * ROW_TILE:(n + 1) * ROW_TILE,
                                                   j * LANES:(j + 1) * LANES].reshape(
                    ROW_GROUPS, SUBLANES, LANES)

    def body(q, carry):
        slot = q % 2
        for_group(q, "wait", in_copy)

        @pl.when(q + 1 < g_all)
        def _():
            for_group(q + 1, "start", in_copy)

        @pl.when(q >= 2)
        def _():
            for_group(q - 2, "wait", out_copy)

        n_valid = sum((tb_ref[GROUP * q + t] >= 0).astype(I32) for t in range(GROUP))
        for width, narrower in zip(GROUP_WIDTHS, GROUP_WIDTHS[1:] + (0,)):
            @pl.when((n_valid > narrower) & (n_valid <= width))
            def _():
                compute(slot, tuple(range(width)))
        for_group(q, "start", out_copy)
        return carry

    lax.fori_loop(g_first, g_end, body, 0)

    @pl.when(e == N_EXPERTS - 1)
    def _():
        @pl.when(g_all >= 2)
        def _():
            for_group(g_all - 2, "wait", out_copy)

        @pl.when(g_all >= 1)
        def _():
            for_group(g_all - 1, "wait", out_copy)


def _expert_call(layer, group_tiles, group_start, xs4, wg, wu, wd):
    group_buf = (2, GROUP, ROW_GROUPS, ROW_TILES, SUBLANES, LANES)
    return pl.pallas_call(
        _expert_kernel,
        grid_spec=pltpu.PrefetchScalarGridSpec(
            num_scalar_prefetch=2,
            grid=(N_EXPERTS,),
            in_specs=[
                pl.BlockSpec(memory_space=pl.ANY),
                pl.BlockSpec((1, 1, D_MODEL, D_EXPERT), lambda e, tb, gs: (layer, e, 0, 0)),
                pl.BlockSpec((1, 1, D_MODEL, D_EXPERT), lambda e, tb, gs: (layer, e, 0, 0)),
                pl.BlockSpec((1, 1, D_EXPERT, D_MODEL), lambda e, tb, gs: (layer, e, 0, 0)),
            ],
            out_specs=pl.BlockSpec(memory_space=pl.ANY),
            scratch_shapes=[
                pltpu.VMEM((D_MODEL, D_EXPERT), BF16),
                pltpu.VMEM((D_MODEL, D_EXPERT), BF16),
                pltpu.VMEM((D_EXPERT, D_MODEL), BF16),
                pltpu.VMEM(group_buf, U32),
                pltpu.VMEM(group_buf, U32),
                pltpu.SemaphoreType.DMA((2, GROUP)),
                pltpu.SemaphoreType.DMA((2, GROUP)),
            ],
        ),
        out_shape=jax.ShapeDtypeStruct((TOTAL_ROWS // SUBLANES, ROW_TILES, SUBLANES, LANES), U32),
        input_output_aliases={2: 0},
        compiler_params=pltpu.CompilerParams(
            dimension_semantics=("arbitrary",), vmem_limit_bytes=VMEM_LIMIT),
        name="experts",
    )(group_tiles, group_start, xs4, wg, wu, wd)


def _combine_kernel(a0_ref, a1_ref, ys_hbm, x1_ref, w_ref, ada_ref, fg_ref, out_ref,
                    ys_buf, r0_scr, r1_scr, sem, *, final):
    c = pl.program_id(0)
    s = pl.program_id(1)
    base = c * ROUTE_CHUNK + s * TC
    region = CHUNK_ROWS * ROW_TILES
    piece = region // COMBINE_STEPS

    def piece_copy(chunk, p):
        src = ys_hbm.at[pl.ds(chunk * region + p * piece, piece)]
        dst = ys_buf.at[pl.ds((chunk % 2) * region + p * piece, piece)]
        return pltpu.make_async_copy(src, dst, sem.at[chunk % 2, p])

    @pl.when((c == 0) & (s == 0))
    def _():
        for p in range(COMBINE_STEPS):
            piece_copy(0, p).start()
        for p in range(COMBINE_STEPS):
            piece_copy(0, p).wait()

    @pl.when(c + 1 < N_CHUNKS)
    def _():
        for p in range(COMBINE_STEPS):
            @pl.when(s == p)
            def _():
                piece_copy(c + 1, p).start()

    off = (c % 2) * region

    for k in range(TC):
        dst = pl.ds(_row_addr(k), ROW_TILES, stride=SUBLANES)
        r0_scr[dst, :] = ys_buf[pl.ds(off + a0_ref[base + k], ROW_TILES, stride=SUBLANES), :]
        r1_scr[dst, :] = ys_buf[pl.ds(off + a1_ref[base + k], ROW_TILES, stride=SUBLANES), :]

    w = w_ref[...]
    w0 = w[:, 0:1]
    w1 = w[:, 1:2]
    g2 = ada_ref[0][5:6]
    n_grp = TC // SUBLANES
    los, his = [], []
    for j in range(ROW_TILES):
        parts0 = [r0_scr[pl.ds((tb * ROW_TILES + j) * SUBLANES, SUBLANES), :] for tb in range(n_grp)]
        parts1 = [r1_scr[pl.ds((tb * ROW_TILES + j) * SUBLANES, SUBLANES), :] for tb in range(n_grp)]
        lo0, hi0 = _unpack_halves(jnp.concatenate(parts0, axis=0))
        lo1, hi1 = _unpack_halves(jnp.concatenate(parts1, axis=0))
        los.append(w0 * lo0 + w1 * lo1)
        his.append(w0 * hi0 + w1 * hi1)
    x2 = x1_ref[...] + g2 * jnp.concatenate(los + his, axis=-1)
    if final:
        r = lax.rsqrt(jnp.mean(x2 * x2, axis=-1, keepdims=True) + EPS)
        x2 = x2 * r * fg_ref[...]
    out_ref[...] = x2

    @pl.when((s == COMBINE_STEPS - 1) & (c + 1 < N_CHUNKS))
    def _():
        for p in range(COMBINE_STEPS):
            piece_copy(c + 1, p).wait()


def _combine_call(a0, a1, ys_2d, x1, wcol, ada_l, final_g, final):
    tok = lambda c, s, a0, a1: (c * COMBINE_STEPS + s, 0)
    return pl.pallas_call(
        functools.partial(_combine_kernel, final=final),
        grid_spec=pltpu.PrefetchScalarGridSpec(
            num_scalar_prefetch=2,
            grid=(N_CHUNKS, COMBINE_STEPS),
            in_specs=[
                pl.BlockSpec(memory_space=pl.ANY),
                pl.BlockSpec((TC, D_MODEL), tok),
                pl.BlockSpec((TC, 2), tok),
                pl.BlockSpec((1, N_ADA, D_MODEL),
                             lambda c, s, a0, a1: ((c * ROUTE_CHUNK + s * TC) // SEQ, 0, 0)),
                pl.BlockSpec((1, D_MODEL), lambda c, s, a0, a1: (0, 0)),
            ],
            out_specs=pl.BlockSpec((TC, D_MODEL), tok),
            scratch_shapes=[pltpu.VMEM((2 * CHUNK_ROWS * ROW_TILES, LANES), U32),
                            pltpu.VMEM((TC * ROW_TILES, LANES), U32),
                            pltpu.VMEM((TC * ROW_TILES, LANES), U32),
                            pltpu.SemaphoreType.DMA((2, COMBINE_STEPS))],
        ),
        out_shape=jax.ShapeDtypeStruct((N_TOK, D_MODEL), F32),
        compiler_params=pltpu.CompilerParams(
            dimension_semantics=("arbitrary", "arbitrary"), vmem_limit_bytes=VMEM_LIMIT),
        name="combine_final" if final else "combine",
    )(a0, a1, ys_2d, x1, wcol, ada_l, final_g)


def _routing_tables(rt, cnt):
    e0 = rt[0].astype(I32)
    e1 = rt[1].astype(I32)
    rank0 = rt[4].astype(I32)
    rank1 = rt[5].astype(I32)
    counts = cnt[TILES_PER_CHUNK - 1::TILES_PER_CHUNK, :, 0].astype(I32)
    n_tiles = (counts + ROW_TILE - 1) // ROW_TILE
    tile_off = jnp.cumsum(n_tiles, axis=1) - n_tiles
    row_off = tile_off * ROW_TILE
    experts = jnp.arange(N_EXPERTS, dtype=I32)[None, :, None]
    off_tok = row_off[:, :, None]
    pick = lambda e: jnp.sum(
        jnp.where(e.reshape(N_CHUNKS, 1, ROUTE_CHUNK) == experts, off_tok, 0), axis=1).reshape(N_TOK)
    slot0 = pick(e0) + rank0
    slot1 = pick(e1) + rank1
    a0 = _row_addr(slot0)
    a1 = _row_addr(slot1)

    nt_flat = n_tiles.T.reshape(-1)
    ends = jnp.cumsum(nt_flat)
    starts = ends - nt_flat
    total = ends[-1]
    step = jnp.minimum(jnp.arange(MAX_ROW_TILES, dtype=I32), total - 1)
    seg = jnp.sum((ends[None, :] <= step[:, None]).astype(I32), axis=1)
    in_seg = seg[:, None] == jnp.arange(N_EXPERTS * N_CHUNKS, dtype=I32)[None, :]
    k = step - jnp.sum(jnp.where(in_seg, starts[None, :], 0), axis=1)
    c = seg % N_CHUNKS
    tile_off_flat = tile_off.T.reshape(-1)
    tile_blk = c * (CHUNK_ROWS // ROW_TILE) + jnp.sum(jnp.where(in_seg, tile_off_flat[None, :], 0), axis=1) + k
    e_count = jnp.sum(n_tiles, axis=0)
    e_start = jnp.cumsum(e_count) - e_count
    g_count = (e_count + GROUP - 1) // GROUP
    g_end = jnp.cumsum(g_count)
    g_start = g_end - g_count
    slot_id = jnp.arange(MAX_GROUPS * GROUP, dtype=I32)
    q = slot_id // GROUP
    owner = jnp.sum((g_end[None, :] <= q[:, None]).astype(I32), axis=1)
    is_owner = owner[:, None] == jnp.arange(N_EXPERTS, dtype=I32)[None, :]
    sel = lambda v: jnp.sum(jnp.where(is_owner, v[None, :], 0), axis=1)
    k = (q - sel(g_start)) * GROUP + slot_id % GROUP
    valid = (owner < N_EXPERTS) & (k < sel(e_count))
    flat = sel(e_start) + k
    pick_tile = flat[:, None] == jnp.arange(MAX_ROW_TILES, dtype=I32)[None, :]
    group_tiles = jnp.where(valid, jnp.sum(jnp.where(pick_tile, tile_blk[None, :], 0), axis=1), -1)
    group_start = jnp.concatenate([g_start, g_end[-1:]])
    return a0, a1, group_tiles.astype(I32), group_start.astype(I32)


def _layer_consts(l, norm1_g, norm2_g, w_in, conv_w, conv_b, sp_norm_g, sp_w, sp_b, mix_out_g,
                  w_out, shared):
    pos = jnp.arange(SP_BLOCK)
    mask = (pos[None, :] // CHUNK) <= (pos[:, None] // CHUNK)
    w = jnp.where(mask[None], sp_w[l], 0.0)
    w_pairs = jnp.concatenate([w[0::2], w[1::2]], axis=-1).astype(BF16)
    sp_bias = jnp.repeat(sp_b[l].T, HEAD_DIM, axis=1)
    return (
        norm1_g[l].reshape(1, D_MODEL), norm2_g[l].reshape(1, D_MODEL),
        w_in[l].astype(BF16),
        conv_w[l], conv_b[l].reshape(1, D_CONV), sp_norm_g[l].reshape(1, D_SPATIAL),
        w_pairs, sp_bias,
        (mix_out_g[l][:, None] * w_out[l]).astype(BF16),
    ) + shared


def kernel(x, c, w_ada, b_ada, norm1_g, norm2_g, w_in, conv_w, conv_b, sp_norm_g, sp_w, sp_b, mix_out_g, w_out, router_w, router_b, exp_w_gate, exp_w_up, exp_w_down, final_g):
    ada = _ada_table(c, w_ada, b_ada).reshape(DEPTH, BATCH, N_ADA, D_MODEL)

    rw_hi = router_w.astype(BF16)
    rw_lo = (router_w - rw_hi.astype(F32)).astype(BF16)
    rw_cat = jnp.pad(jnp.concatenate([rw_hi, rw_lo], axis=1), ((0, 0), (0, LANES - 2 * N_EXPERTS)))
    head_of = jnp.arange(D_MODEL) // HEAD_DIM
    hred = (head_of[:, None] == jnp.arange(LANES)[None, :]).astype(F32) / HEAD_DIM
    hexp_half = (jnp.arange(LANES)[:, None] == head_of[None, :]).astype(F32)
    hexp = jnp.concatenate([hexp_half, hexp_half], axis=0)
    tri = (jnp.arange(SB)[:, None] < jnp.arange(SB)[None, :])
    shared = (rw_cat, router_b.reshape(N_EXPERTS, 1), hred.astype(BF16), hexp.astype(BF16),
              tri.astype(BF16))
    fg = final_g.reshape(1, D_MODEL)

    xc = x.reshape(N_TOK, D_MODEL)
    for l in range(DEPTH):
        consts = _layer_consts(l, norm1_g, norm2_g, w_in, conv_w, conv_b, sp_norm_g, sp_w, sp_b,
                               mix_out_g, w_out, shared)
        x1, h2p, rt, cnt = _mix_call(xc, ada[l], consts)
        a0, a1, group_tiles, group_start = _routing_tables(rt, cnt)
        xs = _dispatch_call(a0, a1, h2p.reshape(N_TOK * ROW_TILES, LANES))
        ys = _expert_call(l, group_tiles, group_start,
                          xs.reshape(TOTAL_ROWS // SUBLANES, ROW_TILES, SUBLANES, LANES),
                          exp_w_gate, exp_w_up, exp_w_down)
        wcol = rt[2:4].T
        xc = _combine_call(a0, a1, ys.reshape(TOTAL_ROWS * ROW_TILES, LANES), x1, wcol, ada[l], fg,
                           final=(l == DEPTH - 1))
    return xc.reshape(BATCH, SEQ, D_MODEL)
```

```python
import functools

import jax
import jax.numpy as jnp
from jax import lax
from jax.experimental import pallas as pl
from jax.experimental.pallas import tpu as pltpu

F32 = jnp.float32
BF16 = jnp.bfloat16
U32 = jnp.uint32
I32 = jnp.int32

D_MODEL = 1024
BATCH = 8
SEQ = 2048
DEPTH = 4
N_TOK = BATCH * SEQ
CHUNK = 64
HEAD_DIM = 64
D_CONV = 512
D_SPATIAL = 512
N_SP_HEADS = 8
N_MIX_HEADS = 16
SP_BLOCK = 128
N_EXPERTS = 16
N_GROUPS = 4
PER_GROUP = 4
D_EXPERT = 512
N_ADA = 6
EPS = 1e-6

LANES = 128
SUBLANES = 8
HALF = D_MODEL // 2
ROW_TILES = HALF // LANES

TM = 1024
SB = 512
TILES_PER_SEQ = SEQ // TM
ROUTE_CHUNK = 4096
N_CHUNKS = N_TOK // ROUTE_CHUNK
TILES_PER_CHUNK = ROUTE_CHUNK // TM
ROW_TILE = 64
ROW_GROUPS = ROW_TILE // SUBLANES
GROUP = 16
GROUP_WIDTHS = (16, 12, 8, 4, 2)
CHUNK_ROWS = 2 * ROUTE_CHUNK + N_EXPERTS * ROW_TILE
TOTAL_ROWS = N_CHUNKS * CHUNK_ROWS
MAX_ROW_TILES = TOTAL_ROWS // ROW_TILE
MAX_GROUPS = MAX_ROW_TILES // GROUP + N_EXPERTS
TD = 2048
TC = 512
COMBINE_STEPS = ROUTE_CHUNK // TC
VMEM_LIMIT = 58 * 1024 * 1024


def _gelu_tanh(x):
    c = 0.7978845608028654
    return 0.5 * x * (1.0 + jnp.tanh(c * (x + 0.044715 * (x * x * x))))


def _split_bf16(x):
    hi = x.astype(BF16)
    lo = (x - hi.astype(F32)).astype(BF16)
    return hi, lo


def _pack_halves(x):
    return pltpu.pack_elementwise([x[:, :HALF], x[:, HALF:]], packed_dtype=BF16)


def _unpack_halves(words):
    lo = pltpu.unpack_elementwise(words, index=0, packed_dtype=BF16, unpacked_dtype=F32)
    hi = pltpu.unpack_elementwise(words, index=1, packed_dtype=BF16, unpacked_dtype=F32)
    return lo, hi


def _ada_kernel(c_ref, w_ref, b_ref, o_ref):
    c = c_ref[...]
    ca = c * (1.0 / (1.0 + jnp.exp(-c)))
    c_hi, c_lo = _split_bf16(ca)
    w_hi, w_lo = _split_bf16(w_ref[0])
    acc = jnp.dot(c_hi, w_hi, preferred_element_type=F32)
    acc += jnp.dot(c_lo, w_hi, preferred_element_type=F32)
    acc += jnp.dot(c_hi, w_lo, preferred_element_type=F32)
    o_ref[0] = acc + b_ref[0]


def _ada_table(c, w_ada, b_ada):
    width = 2 * D_MODEL
    return pl.pallas_call(
        _ada_kernel,
        grid=(DEPTH, N_ADA * D_MODEL // width),
        in_specs=[
            pl.BlockSpec((BATCH, D_MODEL), lambda l, n: (0, 0)),
            pl.BlockSpec((1, D_MODEL, width), lambda l, n: (l, 0, n)),
            pl.BlockSpec((1, 1, width), lambda l, n: (l, 0, n)),
        ],
        out_specs=pl.BlockSpec((1, BATCH, width), lambda l, n: (l, 0, n)),
        out_shape=jax.ShapeDtypeStruct((DEPTH, BATCH, N_ADA * D_MODEL), F32),
        name="ada_table",
    )(c, w_ada, b_ada.reshape(DEPTH, 1, N_ADA * D_MODEL))


def _mix_kernel(x_ref, ada_ref, n1g_ref, n2g_ref, win_ref, cw_ref, cb_ref, spg_ref, spw_ref,
                spbias_ref, wout_ref, rw_ref, rb_ref, hred_ref, hexp_ref,
                tri_ref, x1_ref, h2p_ref, rt_ref, cnt_ref, z_scr, cnt_scr):
    i = pl.program_id(0)

    @pl.when(i % TILES_PER_SEQ == 0)
    def _():
        z_scr[pl.ds(0, SUBLANES), :] = jnp.zeros((SUBLANES, D_CONV), F32)

    @pl.when(i % TILES_PER_CHUNK == 0)
    def _():
        cnt_scr[...] = jnp.zeros((N_EXPERTS, LANES), F32)

    chains = [
        _mix_rows(sb * SB, x_ref, ada_ref, n1g_ref, n2g_ref, win_ref, cw_ref, cb_ref, spg_ref,
                  spw_ref, spbias_ref, wout_ref, rw_ref, rb_ref, hred_ref,
                  hexp_ref, tri_ref, x1_ref, h2p_ref, rt_ref, z_scr, cnt_scr)
        for sb in range(TM // SB)]
    for c in MIX_STAGE_ORDER:
        next(chains[c])
    z_scr[pl.ds(0, SUBLANES), :] = z_scr[pl.ds(TM, SUBLANES), :]
    cnt_ref[0] = cnt_scr[...]


MIX_STAGES = 8
MIX_SKEW = 1
MIX_STAGE_ORDER = tuple(
    c for t in range(MIX_STAGES + MIX_SKEW * (TM // SB - 1)) for c in range(TM // SB)
    if 0 <= t - MIX_SKEW * c < MIX_STAGES)


def _mix_rows(r0, x_ref, ada_ref, n1g_ref, n2g_ref, win_ref, cw_ref, cb_ref, spg_ref, spw_ref,
              spbias_ref, wout_ref, rw_ref, rb_ref, hred_ref, hexp_ref,
              tri_ref, x1_ref, h2p_ref, rt_ref, z_scr, cnt_scr):
    x = x_ref[pl.ds(r0, SB), :]
    ada = ada_ref[0]
    sh1, sc1, g1 = ada[0:1], ada[1:2], ada[2:3]
    sh2, sc2 = ada[3:4], ada[4:5]

    rinv = lax.rsqrt(jnp.mean(x * x, axis=-1, keepdims=True) + EPS)
    h = (x * rinv) * (n1g_ref[...] * (1.0 + sc1)) + sh1
    hb = h.astype(BF16)
    yield

    def proj(k):
        return jnp.dot(hb, win_ref[:, k * 512:(k + 1) * 512], preferred_element_type=F32)

    z = proj(1) * proj(2)
    z_scr[pl.ds(SUBLANES + r0, SB), :] = z
    z1 = z_scr[pl.ds(SUBLANES + r0 - 1, SB), :]
    z2 = z_scr[pl.ds(SUBLANES + r0 - 2, SB), :]
    yield
    p_b, p_u, p_v = proj(0), proj(3), proj(4)
    yield
    cw = cw_ref[...]
    acc = cb_ref[...] + cw[0:1] * z2
    acc = acc + cw[1:2] * z1
    acc = acc + cw[2:3] * z
    y_conv = p_b * acc

    ug = _gelu_tanh(p_u)
    vg = _gelu_tanh(p_v)
    vr = lax.rsqrt(jnp.mean(vg * vg, axis=-1, keepdims=True) + EPS)
    vn = (vg * vr * spg_ref[...]).astype(BF16)
    lane = lax.broadcasted_iota(I32, (1, LANES), 1)
    first_head = lane < HEAD_DIM
    zero = jnp.zeros((), BF16)
    s_rows = []
    for blk in range(SB // SP_BLOCK):
        cols = []
        for p in range(N_SP_HEADS // 2):
            vv = vn[blk * SP_BLOCK:(blk + 1) * SP_BLOCK, p * LANES:(p + 1) * LANES]
            rhs = jnp.concatenate([jnp.where(first_head, vv, zero), jnp.where(first_head, zero, vv)], axis=0)
            cols.append(jnp.dot(spw_ref[p], rhs, preferred_element_type=F32))
        s_rows.append(jnp.concatenate(cols, axis=-1) + spbias_ref[...])
    y_sp = ug * jnp.concatenate(s_rows, axis=0)
    yield

    y = jnp.concatenate([y_conv, y_sp], axis=-1)
    ms = jnp.dot((y * y).astype(BF16), hred_ref[...], preferred_element_type=F32)
    r_hi, r_lo = _split_bf16(lax.rsqrt(ms + EPS))
    scale = jnp.dot(jnp.concatenate([r_hi, r_lo], axis=-1), hexp_ref[...], preferred_element_type=F32)
    yn = (y * scale).astype(BF16)
    yield
    x1 = x + g1 * jnp.dot(yn, wout_ref[...], preferred_element_type=F32)
    x1_ref[pl.ds(r0, SB), :] = x1
    yield

    r2 = lax.rsqrt(jnp.mean(x1 * x1, axis=-1, keepdims=True) + EPS)
    h2 = (x1 * r2) * (n2g_ref[...] * (1.0 + sc2)) + sh2
    h2_hi = h2.astype(BF16)
    packed = _pack_halves(h2_hi.astype(F32))
    for j in range(ROW_TILES):
        h2p_ref[pl.ds(r0 // SUBLANES, SB // SUBLANES), j, :, :] = (
            packed[:, j * LANES:(j + 1) * LANES].reshape(SB // SUBLANES, SUBLANES, LANES))
    yield

    logits_t = jnp.dot(h2_hi, rw_ref[...], preferred_element_type=F32).T
    lg = logits_t[0:N_EXPERTS, :] + logits_t[N_EXPERTS:2 * N_EXPERTS, :]
    scores = 1.0 / (1.0 + jnp.exp(-lg))
    sel = scores + rb_ref[...]

    gs = []
    for g in range(N_GROUPS):
        a, b, c, d = (sel[g * PER_GROUP + k:g * PER_GROUP + k + 1, :] for k in range(PER_GROUP))
        hi1, lo1 = jnp.maximum(a, b), jnp.minimum(a, b)
        hi2, lo2 = jnp.maximum(c, d), jnp.minimum(c, d)
        gs.append(jnp.maximum(hi1, hi2) + jnp.maximum(jnp.minimum(hi1, hi2), jnp.maximum(lo1, lo2)))
    best = gs[0]
    top_group = jnp.zeros((1, SB), I32)
    for g in range(1, N_GROUPS):
        better = gs[g] > best
        best = jnp.where(better, gs[g], best)
        top_group = jnp.where(better, g, top_group)

    eid_i = lax.broadcasted_iota(I32, (N_EXPERTS, SB), 0)
    eid = eid_i.astype(F32)
    neg_inf = jnp.float32(-jnp.inf)
    masked = jnp.where((eid_i // PER_GROUP) == top_group, sel, neg_inf)
    m1 = jnp.max(masked, axis=0, keepdims=True)
    i1 = jnp.min(jnp.where(masked == m1, eid, float(N_EXPERTS)), axis=0, keepdims=True)
    masked2 = jnp.where(eid == i1, neg_inf, masked)
    m2 = jnp.max(masked2, axis=0, keepdims=True)
    i2 = jnp.min(jnp.where(masked2 == m2, eid, float(N_EXPERTS)), axis=0, keepdims=True)
    pick1 = eid == i1
    pick2 = eid == i2
    w1 = jnp.sum(jnp.where(pick1, scores, 0.0), axis=0, keepdims=True)
    w2 = jnp.sum(jnp.where(pick2, scores, 0.0), axis=0, keepdims=True)
    wsum = w1 + w2
    c1 = w1 / wsum
    c2 = w2 / wsum

    assigned = jnp.where(pick1 | pick2, 1.0, 0.0)
    before = jnp.dot(assigned.astype(BF16), tri_ref[...], preferred_element_type=F32)
    carry = cnt_scr[...]
    rank = before + carry[:, 0:1]
    rank1 = jnp.sum(jnp.where(pick1, rank, 0.0), axis=0, keepdims=True)
    rank2 = jnp.sum(jnp.where(pick2, rank, 0.0), axis=0, keepdims=True)
    cnt_scr[...] = carry + jnp.sum(assigned, axis=1, keepdims=True)
    unused = jnp.zeros((1, SB), F32)
    for r, row in enumerate((i1, i2, c1, c2, rank1, rank2, unused, unused)):
        rt_ref[r:r + 1, pl.ds(r0, SB)] = row
    yield


def _mix_call(x2d, ada_l, consts):
    n_tiles = N_TOK // TM
    full = lambda shape: pl.BlockSpec(shape, lambda i: (0,) * len(shape))
    in_specs = [
        pl.BlockSpec((TM, D_MODEL), lambda i: (i, 0)),
        pl.BlockSpec((1, N_ADA, D_MODEL), lambda i: (i // TILES_PER_SEQ, 0, 0)),
        full((1, D_MODEL)), full((1, D_MODEL)),
        full((D_MODEL, 5 * 512)),
        full((3, D_CONV)), full((1, D_CONV)), full((1, D_SPATIAL)),
        full((N_SP_HEADS // 2, SP_BLOCK, 2 * SP_BLOCK)),
        full((SP_BLOCK, D_SPATIAL)),
        full((D_MODEL, D_MODEL)),
        full((D_MODEL, LANES)), full((N_EXPERTS, 1)),
        full((D_MODEL, LANES)), full((2 * LANES, D_MODEL)),
        full((SB, SB)),
    ]
    out_specs = [
        pl.BlockSpec((TM, D_MODEL), lambda i: (i, 0)),
        pl.BlockSpec((TM // SUBLANES, ROW_TILES, SUBLANES, LANES), lambda i: (i, 0, 0, 0)),
        pl.BlockSpec((SUBLANES, TM), lambda i: (0, i)),
        pl.BlockSpec((1, N_EXPERTS, LANES), lambda i: (i, 0, 0)),
    ]
    out_shape = [
        jax.ShapeDtypeStruct((N_TOK, D_MODEL), F32),
        jax.ShapeDtypeStruct((N_TOK // SUBLANES, ROW_TILES, SUBLANES, LANES), U32),
        jax.ShapeDtypeStruct((SUBLANES, N_TOK), F32),
        jax.ShapeDtypeStruct((n_tiles, N_EXPERTS, LANES), F32),
    ]
    return pl.pallas_call(
        _mix_kernel,
        grid=(n_tiles,),
        in_specs=in_specs,
        out_specs=out_specs,
        out_shape=out_shape,
        scratch_shapes=[pltpu.VMEM((TM + SUBLANES, D_CONV), F32), pltpu.VMEM((N_EXPERTS, LANES), F32)],
        compiler_params=pltpu.CompilerParams(
            dimension_semantics=("arbitrary",), vmem_limit_bytes=VMEM_LIMIT),
        name="mix",
    )(x2d, ada_l, *consts)


def _row_addr(t):
    return (t // SUBLANES) * (ROW_TILES * SUBLANES) + t % SUBLANES


def _dispatch_kernel(a0_ref, a1_ref, h2p_ref, xs_ref):
    c = pl.program_id(0)
    s = pl.program_id(1)

    @pl.when(s == 0)
    def _():
        xs_ref[...] = jnp.zeros(xs_ref.shape, U32)

    base = c * ROUTE_CHUNK + s * TD

    for k in range(TD):
        row = h2p_ref[pl.ds(_row_addr(k), ROW_TILES, stride=SUBLANES), :]
        xs_ref[pl.ds(a0_ref[base + k], ROW_TILES, stride=SUBLANES), :] = row
        xs_ref[pl.ds(a1_ref[base + k], ROW_TILES, stride=SUBLANES), :] = row


def _dispatch_call(a0, a1, h2p_2d):
    steps = ROUTE_CHUNK // TD
    return pl.pallas_call(
        _dispatch_kernel,
        grid_spec=pltpu.PrefetchScalarGridSpec(
            num_scalar_prefetch=2,
            grid=(N_CHUNKS, steps),
            in_specs=[pl.BlockSpec((TD * ROW_TILES, LANES), lambda c, s, a0, a1: (c * steps + s, 0))],
            out_specs=pl.BlockSpec((CHUNK_ROWS * ROW_TILES, LANES), lambda c, s, a0, a1: (c, 0)),
        ),
        out_shape=jax.ShapeDtypeStruct((TOTAL_ROWS * ROW_TILES, LANES), U32),
        compiler_params=pltpu.CompilerParams(
            dimension_semantics=("arbitrary", "arbitrary"), vmem_limit_bytes=VMEM_LIMIT),
        name="dispatch",
    )(a0, a1, h2p_2d)


def _expert_kernel(tb_ref, gst_ref, xs_hbm, wg_ref, wu_ref, wd_ref, ys_hbm,
                   wg_s, wu_s, wd_s, xbuf, ybuf, sem_in, sem_out):
    e = pl.program_id(0)
    g_first = gst_ref[e]
    g_end = gst_ref[e + 1]
    g_all = gst_ref[N_EXPERTS]

    def in_copy(q, t):
        rows = pl.ds(tb_ref[GROUP * q + t] * ROW_GROUPS, ROW_GROUPS)
        return pltpu.make_async_copy(xs_hbm.at[rows], xbuf.at[q % 2, t], sem_in.at[q % 2, t])

    def out_copy(q, t):
        rows = pl.ds(tb_ref[GROUP * q + t] * ROW_GROUPS, ROW_GROUPS)
        return pltpu.make_async_copy(ybuf.at[q % 2, t], ys_hbm.at[rows], sem_out.at[q % 2, t])

    def for_group(q, action, make):
        for t in range(GROUP):
            @pl.when(tb_ref[GROUP * q + t] >= 0)
            def _():
                if action == "start":
                    make(q, t).start(priority=1)
                else:
                    make(q, t).wait()

    @pl.when(e == 0)
    def _():
        xbuf[...] = jnp.zeros(xbuf.shape, U32)

        @pl.when(g_all > 0)
        def _():
            for_group(0, "start", in_copy)

    wg_s[...] = wg_ref[0, 0].astype(BF16)
    wu_s[...] = wu_ref[0, 0].astype(BF16)
    wd_s[...] = wd_ref[0, 0].astype(BF16)

    def compute(slot, tiles):
        los, his = [], []
        for j in range(ROW_TILES):
            words = jnp.concatenate(
                [xbuf[slot, t, :, j, :, :].reshape(ROW_TILE, LANES) for t in tiles], axis=0)
            lo, hi = _unpack_halves(words)
            los.append(lo)
            his.append(hi)
        xb = jnp.concatenate(los + his, axis=-1).astype(BF16)
        g = jnp.dot(xb, wg_s[...], preferred_element_type=F32)
        u = jnp.dot(xb, wu_s[...], preferred_element_type=F32)
        act = (g * (1.0 / (1.0 + jnp.exp(-g))) * u).astype(BF16)
        y = jnp.dot(act, wd_s[...], preferred_element_type=F32)
        packed = _pack_halves(y.astype(BF16).astype(F32))
        for n, t in enumerate(tiles):
            for j in range(ROW_TILES):
                ybuf[slot, t, :, j, :, :] = packed[n * ROW_TILE:(n + 1) * ROW_TILE,
                                                   j * LANES:(j + 1) * LANES].reshape(
                    ROW_GROUPS, SUBLANES, LANES)

    def body(q, carry):
        slot = q % 2
        for_group(q, "wait", in_copy)

        @pl.when(q + 1 < g_all)
        def _():
            for_group(q + 1, "start", in_copy)

        @pl.when(q >= 2)
        def _():
            for_group(q - 2, "wait", out_copy)

        n_valid = sum((tb_ref[GROUP * q + t] >= 0).astype(I32) for t in range(GROUP))
        for width, narrower in zip(GROUP_WIDTHS, GROUP_WIDTHS[1:] + (0,)):
            @pl.when((n_valid > narrower) & (n_valid <= width))
            def _():
                compute(slot, tuple(range(width)))
        for_group(q, "start", out_copy)
        return carry

    lax.fori_loop(g_first, g_end, body, 0)

    @pl.when(e == N_EXPERTS - 1)
    def _():
        @pl.when(g_all >= 2)
        def _():
            for_group(g_all - 2, "wait", out_copy)

        @pl.when(g_all >= 1)
        def _():
            for_group(g_all - 1, "wait", out_copy)


def _expert_call(layer, group_tiles, group_start, xs4, wg, wu, wd):
    group_buf = (2, GROUP, ROW_GROUPS, ROW_TILES, SUBLANES, LANES)
    return pl.pallas_call(
        _expert_kernel,
        grid_spec=pltpu.PrefetchScalarGridSpec(
            num_scalar_prefetch=2,
            grid=(N_EXPERTS,),
            in_specs=[
                pl.BlockSpec(memory_space=pl.ANY),
                pl.BlockSpec((1, 1, D_MODEL, D_EXPERT), lambda e, tb, gs: (layer, e, 0, 0)),
                pl.BlockSpec((1, 1, D_MODEL, D_EXPERT), lambda e, tb, gs: (layer, e, 0, 0)),
                pl.BlockSpec((1, 1, D_EXPERT, D_MODEL), lambda e, tb, gs: (layer, e, 0, 0)),
            ],
            out_specs=pl.BlockSpec(memory_space=pl.ANY),
            scratch_shapes=[
                pltpu.VMEM((D_MODEL, D_EXPERT), BF16),
                pltpu.VMEM((D_MODEL, D_EXPERT), BF16),
                pltpu.VMEM((D_EXPERT, D_MODEL), BF16),
                pltpu.VMEM(group_buf, U32),
                pltpu.VMEM(group_buf, U32),
                pltpu.SemaphoreType.DMA((2, GROUP)),
                pltpu.SemaphoreType.DMA((2, GROUP)),
            ],
        ),
        out_shape=jax.ShapeDtypeStruct((TOTAL_ROWS // SUBLANES, ROW_TILES, SUBLANES, LANES), U32),
        input_output_aliases={2: 0},
        compiler_params=pltpu.CompilerParams(
            dimension_semantics=("arbitrary",), vmem_limit_bytes=VMEM_LIMIT),
        name="experts",
    )(group_tiles, group_start, xs4, wg, wu, wd)


def _combine_kernel(a0_ref, a1_ref, ys_hbm, x1_ref, w_ref, ada_ref, fg_ref, out_ref,
                    ys_buf, r0_scr, r1_scr, sem, *, final):
    c = pl.program_id(0)
    s = pl.program_id(1)
    base = c * ROUTE_CHUNK + s * TC
    region = CHUNK_ROWS * ROW_TILES
    piece = region // COMBINE_STEPS

    def piece_copy(chunk, p):
        src = ys_hbm.at[pl.ds(chunk * region + p * piece, piece)]
        dst = ys_buf.at[pl.ds((chunk % 2) * region + p * piece, piece)]
        return pltpu.make_async_copy(src, dst, sem.at[chunk % 2, p])

    @pl.when((c == 0) & (s == 0))
    def _():
        for p in range(COMBINE_STEPS):
            piece_copy(0, p).start()
        for p in range(COMBINE_STEPS):
            piece_copy(0, p).wait()

    @pl.when(c + 1 < N_CHUNKS)
    def _():
        for p in range(COMBINE_STEPS):
            @pl.when(s == p)
            def _():
                piece_copy(c + 1, p).start()

    off = (c % 2) * region

    for k in range(TC):
        dst = pl.ds(_row_addr(k), ROW_TILES, stride=SUBLANES)
        r0_scr[dst, :] = ys_buf[pl.ds(off + a0_ref[base + k], ROW_TILES, stride=SUBLANES), :]
        r1_scr[dst, :] = ys_buf[pl.ds(off + a1_ref[base + k], ROW_TILES, stride=SUBLANES), :]

    w = w_ref[...]
    w0 = w[:, 0:1]
    w1 = w[:, 1:2]
    g2 = ada_ref[0][5:6]
    n_grp = TC // SUBLANES
    los, his = [], []
    for j in range(ROW_TILES):
        parts0 = [r0_scr[pl.ds((tb * ROW_TILES + j) * SUBLANES, SUBLANES), :] for tb in range(n_grp)]
        parts1 = [r1_scr[pl.ds((tb * ROW_TILES + j) * SUBLANES, SUBLANES), :] for tb in range(n_grp)]
        lo0, hi0 = _unpack_halves(jnp.concatenate(parts0, axis=0))
        lo1, hi1 = _unpack_halves(jnp.concatenate(parts1, axis=0))
        los.append(w0 * lo0 + w1 * lo1)
        his.append(w0 * hi0 + w1 * hi1)
    x2 = x1_ref[...] + g2 * jnp.concatenate(los + his, axis=-1)
    if final:
        r = lax.rsqrt(jnp.mean(x2 * x2, axis=-1, keepdims=True) + EPS)
        x2 = x2 * r * fg_ref[...]
    out_ref[...] = x2

    @pl.when((s == COMBINE_STEPS - 1) & (c + 1 < N_CHUNKS))
    def _():
        for p in range(COMBINE_STEPS):
            piece_copy(c + 1, p).wait()


def _combine_call(a0, a1, ys_2d, x1, wcol, ada_l, final_g, final):
    tok = lambda c, s, a0, a1: (c * COMBINE_STEPS + s, 0)
    return pl.pallas_call(
        functools.partial(_combine_kernel, final=final),
        grid_spec=pltpu.PrefetchScalarGridSpec(
            num_scalar_prefetch=2,
            grid=(N_CHUNKS, COMBINE_STEPS),
            in_specs=[
                pl.BlockSpec(memory_space=pl.ANY),
                pl.BlockSpec((TC, D_MODEL), tok),
                pl.BlockSpec((TC, 2), tok),
                pl.BlockSpec((1, N_ADA, D_MODEL),
                             lambda c, s, a0, a1: ((c * ROUTE_CHUNK + s * TC) // SEQ, 0, 0)),
                pl.BlockSpec((1, D_MODEL), lambda c, s, a0, a1: (0, 0)),
            ],
            out_specs=pl.BlockSpec((TC, D_MODEL), tok),
            scratch_shapes=[pltpu.VMEM((2 * CHUNK_ROWS * ROW_TILES, LANES), U32),
                            pltpu.VMEM((TC * ROW_TILES, LANES), U32),
                            pltpu.VMEM((TC * ROW_TILES, LANES), U32),
                            pltpu.SemaphoreType.DMA((2, COMBINE_STEPS))],
        ),
        out_shape=jax.ShapeDtypeStruct((N_TOK, D_MODEL), F32),
        compiler_params=pltpu.CompilerParams(
            dimension_semantics=("arbitrary", "arbitrary"), vmem_limit_bytes=VMEM_LIMIT),
        name="combine_final" if final else "combine",
    )(a0, a1, ys_2d, x1, wcol, ada_l, final_g)


def _routing_tables(rt, cnt):
    e0 = rt[0].astype(I32)
    e1 = rt[1].astype(I32)
    rank0 = rt[4].astype(I32)
    rank1 = rt[5].astype(I32)
    counts = cnt[TILES_PER_CHUNK - 1::TILES_PER_CHUNK, :, 0].astype(I32)
    n_tiles = (counts + ROW_TILE - 1) // ROW_TILE
    tile_off = jnp.cumsum(n_tiles, axis=1) - n_tiles
    row_off = tile_off * ROW_TILE
    experts = jnp.arange(N_EXPERTS, dtype=I32)[None, :, None]
    off_tok = row_off[:, :, None]
    pick = lambda e: jnp.sum(
        jnp.where(e.reshape(N_CHUNKS, 1, ROUTE_CHUNK) == experts, off_tok, 0), axis=1).reshape(N_TOK)
    slot0 = pick(e0) + rank0
    slot1 = pick(e1) + rank1
    a0 = _row_addr(slot0)
    a1 = _row_addr(slot1)

    nt_flat = n_tiles.T.reshape(-1)
    ends = jnp.cumsum(nt_flat)
    starts = ends - nt_flat
    total = ends[-1]
    step = jnp.minimum(jnp.arange(MAX_ROW_TILES, dtype=I32), total - 1)
    seg = jnp.sum((ends[None, :] <= step[:, None]).astype(I32), axis=1)
    in_seg = seg[:, None] == jnp.arange(N_EXPERTS * N_CHUNKS, dtype=I32)[None, :]
    k = step - jnp.sum(jnp.where(in_seg, starts[None, :], 0), axis=1)
    c = seg % N_CHUNKS
    tile_off_flat = tile_off.T.reshape(-1)
    tile_blk = c * (CHUNK_ROWS // ROW_TILE) + jnp.sum(jnp.where(in_seg, tile_off_flat[None, :], 0), axis=1) + k
    e_count = jnp.sum(n_tiles, axis=0)
    e_start = jnp.cumsum(e_count) - e_count
    g_count = (e_count + GROUP - 1) // GROUP
    g_end = jnp.cumsum(g_count)
    g_start = g_end - g_count
    slot_id = jnp.arange(MAX_GROUPS * GROUP, dtype=I32)
    q = slot_id // GROUP
    owner = jnp.sum((g_end[None, :] <= q[:, None]).astype(I32), axis=1)
    is_owner = owner[:, None] == jnp.arange(N_EXPERTS, dtype=I32)[None, :]
    sel = lambda v: jnp.sum(jnp.where(is_owner, v[None, :], 0), axis=1)
    k = (q - sel(g_start)) * GROUP + slot_id % GROUP
    valid = (owner < N_EXPERTS) & (k < sel(e_count))
    flat = sel(e_start) + k
    pick_tile = flat[:, None] == jnp.arange(MAX_ROW_TILES, dtype=I32)[None, :]
    group_tiles = jnp.where(valid, jnp.sum(jnp.where(pick_tile, tile_blk[None, :], 0), axis=1), -1)
    group_start = jnp.concatenate([g_start, g_end[-1:]])
    return a0, a1, group_tiles.astype(I32), group_start.astype(I32)


def _layer_consts(l, norm1_g, norm2_g, w_in, conv_w, conv_b, sp_norm_g, sp_w, sp_b, mix_out_g,
                  w_out, shared):
    pos = jnp.arange(SP_BLOCK)
    mask = (pos[None, :] // CHUNK) <= (pos[:, None] // CHUNK)
    w = jnp.where(mask[None], sp_w[l], 0.0)
    w_pairs = jnp.concatenate([w[0::2], w[1::2]], axis=-1).astype(BF16)
    sp_bias = jnp.repeat(sp_b[l].T, HEAD_DIM, axis=1)
    return (
        norm1_g[l].reshape(1, D_MODEL), norm2_g[l].reshape(1, D_MODEL),
        w_in[l].astype(BF16),
        conv_w[l], conv_b[l].reshape(1, D_CONV), sp_norm_g[l].reshape(1, D_SPATIAL),
        w_pairs, sp_bias,
        (mix_out_g[l][:, None] * w_out[l]).astype(BF16),
    ) + shared


def kernel(x, c, w_ada, b_ada, norm1_g, norm2_g, w_in, conv_w, conv_b, sp_norm_g, sp_w, sp_b, mix_out_g, w_out, router_w, router_b, exp_w_gate, exp_w_up, exp_w_down, final_g):
    ada = _ada_table(c, w_ada, b_ada).reshape(DEPTH, BATCH, N_ADA, D_MODEL)

    rw_hi = router_w.astype(BF16)
    rw_lo = (router_w - rw_hi.astype(F32)).astype(BF16)
    rw_cat = jnp.pad(jnp.concatenate([rw_hi, rw_lo], axis=1), ((0, 0), (0, LANES - 2 * N_EXPERTS)))
    head_of = jnp.arange(D_MODEL) // HEAD_DIM
    hred = (head_of[:, None] == jnp.arange(LANES)[None, :]).astype(F32) / HEAD_DIM
    hexp_half = (jnp.arange(LANES)[:, None] == head_of[None, :]).astype(F32)
    hexp = jnp.concatenate([hexp_half, hexp_half], axis=0)
    tri = (jnp.arange(SB)[:, None] < jnp.arange(SB)[None, :])
    shared = (rw_cat, router_b.reshape(N_EXPERTS, 1), hred.astype(BF16), hexp.astype(BF16),
              tri.astype(BF16))
    fg = final_g.reshape(1, D_MODEL)

    xc = x.reshape(N_TOK, D_MODEL)
    for l in range(DEPTH):
        consts = _layer_consts(l, norm1_g, norm2_g, w_in, conv_w, conv_b, sp_norm_g, sp_w, sp_b,
                               mix_out_g, w_out, shared)
        x1, h2p, rt, cnt = _mix_call(xc, ada[l], consts)
        a0, a1, group_tiles, group_start = _routing_tables(rt, cnt)
        xs = _dispatch_call(a0, a1, h2p.reshape(N_TOK * ROW_TILES, LANES))
        ys = _expert_call(l, group_tiles, group_start,
                          xs.reshape(TOTAL_ROWS // SUBLANES, ROW_TILES, SUBLANES, LANES),
                          exp_w_gate, exp_w_up, exp_w_down)
        wcol = rt[2:4].T
        xc = _combine_call(a0, a1, ys.reshape(TOTAL_ROWS * ROW_TILES, LANES), x1, wcol, ada[l], fg,
                           final=(l == DEPTH - 1))
    return xc.reshape(BATCH, SEQ, D_MODEL)
```

```python
import functools

import jax
import jax.numpy as jnp
from jax import lax
from jax.experimental import pallas as pl
from jax.experimental.pallas import tpu as pltpu

F32 = jnp.float32
BF16 = jnp.bfloat16
U32 = jnp.uint32
I32 = jnp.int32

D_MODEL = 1024
BATCH = 8
SEQ = 2048
DEPTH = 4
N_TOK = BATCH * SEQ
CHUNK = 64
HEAD_DIM = 64
D_CONV = 512
D_SPATIAL = 512
N_SP_HEADS = 8
N_MIX_HEADS = 16
SP_BLOCK = 128
N_EXPERTS = 16
N_GROUPS = 4
PER_GROUP = 4
D_EXPERT = 512
N_ADA = 6
EPS = 1e-6

LANES = 128
SUBLANES = 8
HALF = D_MODEL // 2
ROW_TILES = HALF // LANES

TM = 1024
SB = 512
TILES_PER_SEQ = SEQ // TM
ROUTE_CHUNK = 4096
N_CHUNKS = N_TOK // ROUTE_CHUNK
TILES_PER_CHUNK = ROUTE_CHUNK // TM
ROW_TILE = 64
ROW_GROUPS = ROW_TILE // SUBLANES
GROUP = 16
GROUP_WIDTHS = (16, 12, 8, 4, 2)
CHUNK_ROWS = 2 * ROUTE_CHUNK + N_EXPERTS * ROW_TILE
TOTAL_ROWS = N_CHUNKS * CHUNK_ROWS
MAX_ROW_TILES = TOTAL_ROWS // ROW_TILE
MAX_GROUPS = MAX_ROW_TILES // GROUP + N_EXPERTS
TD = 2048
TC = 512
COMBINE_STEPS = ROUTE_CHUNK // TC
VMEM_LIMIT = 58 * 1024 * 1024


def _gelu_tanh(x):
    c = 0.7978845608028654
    return 0.5 * x * (1.0 + jnp.tanh(c * (x + 0.044715 * (x * x * x))))


def _split_bf16(x):
    hi = x.astype(BF16)
    lo = (x - hi.astype(F32)).astype(BF16)
    return hi, lo


def _pack_halves(x):
    return pltpu.pack_elementwise([x[:, :HALF], x[:, HALF:]], packed_dtype=BF16)


def _unpack_halves(words):
    lo = pltpu.unpack_elementwise(words, index=0, packed_dtype=BF16, unpacked_dtype=F32)
    hi = pltpu.unpack_elementwise(words, index=1, packed_dtype=BF16, unpacked_dtype=F32)
    return lo, hi


def _ada_kernel(c_ref, w_ref, b_ref, o_ref):
    c = c_ref[...]
    ca = c * (1.0 / (1.0 + jnp.exp(-c)))
    c_hi, c_lo = _split_bf16(ca)
    w_hi, w_lo = _split_bf16(w_ref[0])
    acc = jnp.dot(c_hi, w_hi, preferred_element_type=F32)
    acc += jnp.dot(c_lo, w_hi, preferred_element_type=F32)
    acc += jnp.dot(c_hi, w_lo, preferred_element_type=F32)
    o_ref[0] = acc + b_ref[0]


def _ada_table(c, w_ada, b_ada):
    width = 2 * D_MODEL
    return pl.pallas_call(
        _ada_kernel,
        grid=(DEPTH, N_ADA * D_MODEL // width),
        in_specs=[
            pl.BlockSpec((BATCH, D_MODEL), lambda l, n: (0, 0)),
            pl.BlockSpec((1, D_MODEL, width), lambda l, n: (l, 0, n)),
            pl.BlockSpec((1, 1, width), lambda l, n: (l, 0, n)),
        ],
        out_specs=pl.BlockSpec((1, BATCH, width), lambda l, n: (l, 0, n)),
        out_shape=jax.ShapeDtypeStruct((DEPTH, BATCH, N_ADA * D_MODEL), F32),
        name="ada_table",
    )(c, w_ada, b_ada.reshape(DEPTH, 1, N_ADA * D_MODEL))


def _mix_kernel(x_ref, ada_ref, n1g_ref, n2g_ref, win_ref, cw_ref, cb_ref, spg_ref, spw_ref,
                spbias_ref, wout_ref, rw_ref, rb_ref, hred_ref, hexp_ref,
                tri_ref, x1_ref, h2p_ref, rt_ref, cnt_ref, z_scr, cnt_scr):
    i = pl.program_id(0)

    @pl.when(i % TILES_PER_SEQ == 0)
    def _():
        z_scr[pl.ds(0, SUBLANES), :] = jnp.zeros((SUBLANES, D_CONV), F32)

    @pl.when(i % TILES_PER_CHUNK == 0)
    def _():
        cnt_scr[...] = jnp.zeros((N_EXPERTS, LANES), F32)

    chains = [
        _mix_rows(sb * SB, x_ref, ada_ref, n1g_ref, n2g_ref, win_ref, cw_ref, cb_ref, spg_ref,
                  spw_ref, spbias_ref, wout_ref, rw_ref, rb_ref, hred_ref,
                  hexp_ref, tri_ref, x1_ref, h2p_ref, rt_ref, z_scr, cnt_scr)
        for sb in range(TM // SB)]
    for c in MIX_STAGE_ORDER:
        next(chains[c])
    z_scr[pl.ds(0, SUBLANES), :] = z_scr[pl.ds(TM, SUBLANES), :]
    cnt_ref[0] = cnt_scr[...]


MIX_STAGES = 8
MIX_SKEW = 1
MIX_STAGE_ORDER = tuple(
    c for t in range(MIX_STAGES + MIX_SKEW * (TM // SB - 1)) for c in range(TM // SB)
    if 0 <= t - MIX_SKEW * c < MIX_STAGES)


def _mix_rows(r0, x_ref, ada_ref, n1g_ref, n2g_ref, win_ref, cw_ref, cb_ref, spg_ref, spw_ref,
              spbias_ref, wout_ref, rw_ref, rb_ref, hred_ref, hexp_ref,
              tri_ref, x1_ref, h2p_ref, rt_ref, z_scr, cnt_scr):
    x = x_ref[pl.ds(r0, SB), :]
    ada = ada_ref[0]
    sh1, sc1, g1 = ada[0:1], ada[1:2], ada[2:3]
    sh2, sc2 = ada[3:4], ada[4:5]

    rinv = lax.rsqrt(jnp.mean(x * x, axis=-1, keepdims=True) + EPS)
    h = (x * rinv) * (n1g_ref[...] * (1.0 + sc1)) + sh1
    hb = h.astype(BF16)
    yield

    def proj(k):
        return jnp.dot(hb, win_ref[:, k * 512:(k + 1) * 512], preferred_element_type=F32)

    z = proj(1) * proj(2)
    z_scr[pl.ds(SUBLANES + r0, SB), :] = z
    z1 = z_scr[pl.ds(SUBLANES + r0 - 1, SB), :]
    z2 = z_scr[pl.ds(SUBLANES + r0 - 2, SB), :]
    yield
    p_b, p_u, p_v = proj(0), proj(3), proj(4)
    yield
    cw = cw_ref[...]
    acc = cb_ref[...] + cw[0:1] * z2
    acc = acc + cw[1:2] * z1
    acc = acc + cw[2:3] * z
    y_conv = p_b * acc

    ug = _gelu_tanh(p_u)
    vg = _gelu_tanh(p_v)
    vr = lax.rsqrt(jnp.mean(vg * vg, axis=-1, keepdims=True) + EPS)
    vn = (vg * vr * spg_ref[...]).astype(BF16)
    lane = lax.broadcasted_iota(I32, (1, LANES), 1)
    first_head = lane < HEAD_DIM
    zero = jnp.zeros((), BF16)
    s_rows = []
    for blk in range(SB // SP_BLOCK):
        cols = []
        for p in range(N_SP_HEADS // 2):
            vv = vn[blk * SP_BLOCK:(blk + 1) * SP_BLOCK, p * LANES:(p + 1) * LANES]
            rhs = jnp.concatenate([jnp.where(first_head, vv, zero), jnp.where(first_head, zero, vv)], axis=0)
            cols.append(jnp.dot(spw_ref[p], rhs, preferred_element_type=F32))
        s_rows.append(jnp.concatenate(cols, axis=-1) + spbias_ref[...])
    y_sp = ug * jnp.concatenate(s_rows, axis=0)
    yield

    y = jnp.concatenate([y_conv, y_sp], axis=-1)
    ms = jnp.dot((y * y).astype(BF16), hred_ref[...], preferred_element_type=F32)
    r_hi, r_lo = _split_bf16(lax.rsqrt(ms + EPS))
    scale = jnp.dot(jnp.concatenate([r_hi, r_lo], axis=-1), hexp_ref[...], preferred_element_type=F32)
    yn = (y * scale).astype(BF16)
    yield
    x1 = x + g1 * jnp.dot(yn, wout_ref[...], preferred_element_type=F32)
    x1_ref[pl.ds(r0, SB), :] = x1
    yield

    r2 = lax.rsqrt(jnp.mean(x1 * x1, axis=-1, keepdims=True) + EPS)
    h2 = (x1 * r2) * (n2g_ref[...] * (1.0 + sc2)) + sh2
    h2_hi = h2.astype(BF16)
    packed = _pack_halves(h2_hi.astype(F32))
    for j in range(ROW_TILES):
        h2p_ref[pl.ds(r0 // SUBLANES, SB // SUBLANES), j, :, :] = (
            packed[:, j * LANES:(j + 1) * LANES].reshape(SB // SUBLANES, SUBLANES, LANES))
    yield

    logits_t = jnp.dot(h2_hi, rw_ref[...], preferred_element_type=F32).T
    lg = logits_t[0:N_EXPERTS, :] + logits_t[N_EXPERTS:2 * N_EXPERTS, :]
    scores = 1.0 / (1.0 + jnp.exp(-lg))
    sel = scores + rb_ref[...]

    gs = []
    for g in range(N_GROUPS):
        a, b, c, d = (sel[g * PER_GROUP + k:g * PER_GROUP + k + 1, :] for k in range(PER_GROUP))
        hi1, lo1 = jnp.maximum(a, b), jnp.minimum(a, b)
        hi2, lo2 = jnp.maximum(c, d), jnp.minimum(c, d)
        gs.append(jnp.maximum(hi1, hi2) + jnp.maximum(jnp.minimum(hi1, hi2), jnp.maximum(lo1, lo2)))
    best = gs[0]
    top_group = jnp.zeros((1, SB), I32)
    for g in range(1, N_GROUPS):
        better = gs[g] > best
        best = jnp.where(better, gs[g], best)
        top_group = jnp.where(better, g, top_group)

    eid_i = lax.broadcasted_iota(I32, (N_EXPERTS, SB), 0)
    eid = eid_i.astype(F32)
    neg_inf = jnp.float32(-jnp.inf)
    masked = jnp.where((eid_i // PER_GROUP) == top_group, sel, neg_inf)
    m1 = jnp.max(masked, axis=0, keepdims=True)
    i1 = jnp.min(jnp.where(masked == m1, eid, float(N_EXPERTS)), axis=0, keepdims=True)
    masked2 = jnp.where(eid == i1, neg_inf, masked)
    m2 = jnp.max(masked2, axis=0, keepdims=True)
    i2 = jnp.min(jnp.where(masked2 == m2, eid, float(N_EXPERTS)), axis=0, keepdims=True)
    pick1 = eid == i1
    pick2 = eid == i2
    w1 = jnp.sum(jnp.where(pick1, scores, 0.0), axis=0, keepdims=True)
    w2 = jnp.sum(jnp.where(pick2, scores, 0.0), axis=0, keepdims=True)
    wsum = w1 + w2
    c1 = w1 / wsum
    c2 = w2 / wsum

    assigned = jnp.where(pick1 | pick2, 1.0, 0.0)
    before = jnp.dot(assigned.astype(BF16), tri_ref[...], preferred_element_type=F32)
    carry = cnt_scr[...]
    rank = before + carry[:, 0:1]
    rank1 = jnp.sum(jnp.where(pick1, rank, 0.0), axis=0, keepdims=True)
    rank2 = jnp.sum(jnp.where(pick2, rank, 0.0), axis=0, keepdims=True)
    cnt_scr[...] = carry + jnp.sum(assigned, axis=1, keepdims=True)
    unused = jnp.zeros((1, SB), F32)
    for r, row in enumerate((i1, i2, c1, c2, rank1, rank2, unused, unused)):
        rt_ref[r:r + 1, pl.ds(r0, SB)] = row
    yield


def _mix_call(x2d, ada_l, consts):
    n_tiles = N_TOK // TM
    full = lambda shape: pl.BlockSpec(shape, lambda i: (0,) * len(shape))
    in_specs = [
        pl.BlockSpec((TM, D_MODEL), lambda i: (i, 0)),
        pl.BlockSpec((1, N_ADA, D_MODEL), lambda i: (i // TILES_PER_SEQ, 0, 0)),
        full((1, D_MODEL)), full((1, D_MODEL)),
        full((D_MODEL, 5 * 512)),
        full((3, D_CONV)), full((1, D_CONV)), full((1, D_SPATIAL)),
        full((N_SP_HEADS // 2, SP_BLOCK, 2 * SP_BLOCK)),
        full((SP_BLOCK, D_SPATIAL)),
        full((D_MODEL, D_MODEL)),
        full((D_MODEL, LANES)), full((N_EXPERTS, 1)),
        full((D_MODEL, LANES)), full((2 * LANES, D_MODEL)),
        full((SB, SB)),
    ]
    out_specs = [
        pl.BlockSpec((TM, D_MODEL), lambda i: (i, 0)),
        pl.BlockSpec((TM // SUBLANES, ROW_TILES, SUBLANES, LANES), lambda i: (i, 0, 0, 0)),
        pl.BlockSpec((SUBLANES, TM), lambda i: (0, i)),
        pl.BlockSpec((1, N_EXPERTS, LANES), lambda i: (i, 0, 0)),
    ]
    out_shape = [
        jax.ShapeDtypeStruct((N_TOK, D_MODEL), F32),
        jax.ShapeDtypeStruct((N_TOK // SUBLANES, ROW_TILES, SUBLANES, LANES), U32),
        jax.ShapeDtypeStruct((SUBLANES, N_TOK), F32),
        jax.ShapeDtypeStruct((n_tiles, N_EXPERTS, LANES), F32),
    ]
    return pl.pallas_call(
        _mix_kernel,
        grid=(n_tiles,),
        in_specs=in_specs,
        out_specs=out_specs,
        out_shape=out_shape,
        scratch_shapes=[pltpu.VMEM((TM + SUBLANES, D_CONV), F32), pltpu.VMEM((N_EXPERTS, LANES), F32)],
        compiler_params=pltpu.CompilerParams(
            dimension_semantics=("arbitrary",), vmem_limit_bytes=VMEM_LIMIT),
        name="mix",
    )(x2d, ada_l, *consts)


def _row_addr(t):
    return (t // SUBLANES) * (ROW_TILES * SUBLANES) + t % SUBLANES


def _dispatch_kernel(a0_ref, a1_ref, h2p_ref, xs_ref):
    c = pl.program_id(0)
    s = pl.program_id(1)

    @pl.when(s == 0)
    def _():
        xs_ref[...] = jnp.zeros(xs_ref.shape, U32)

    base = c * ROUTE_CHUNK + s * TD

    for k in range(TD):
        row = h2p_ref[pl.ds(_row_addr(k), ROW_TILES, stride=SUBLANES), :]
        xs_ref[pl.ds(a0_ref[base + k], ROW_TILES, stride=SUBLANES), :] = row
        xs_ref[pl.ds(a1_ref[base + k], ROW_TILES, stride=SUBLANES), :] = row


def _dispatch_call(a0, a1, h2p_2d):
    steps = ROUTE_CHUNK // TD
    return pl.pallas_call(
        _dispatch_kernel,
        grid_spec=pltpu.PrefetchScalarGridSpec(
            num_scalar_prefetch=2,
            grid=(N_CHUNKS, steps),
            in_specs=[pl.BlockSpec((TD * ROW_TILES, LANES), lambda c, s, a0, a1: (c * steps + s, 0))],
            out_specs=pl.BlockSpec((CHUNK_ROWS * ROW_TILES, LANES), lambda c, s, a0, a1: (c, 0)),
        ),
        out_shape=jax.ShapeDtypeStruct((TOTAL_ROWS * ROW_TILES, LANES), U32),
        compiler_params=pltpu.CompilerParams(
            dimension_semantics=("arbitrary", "arbitrary"), vmem_limit_bytes=VMEM_LIMIT),
        name="dispatch",
    )(a0, a1, h2p_2d)


def _expert_kernel(tb_ref, gst_ref, xs_hbm, wg_ref, wu_ref, wd_ref, ys_hbm,
                   wg_s, wu_s, wd_s, xbuf, ybuf, sem_in, sem_out):
    e = pl.program_id(0)
    g_first = gst_ref[e]
    g_end = gst_ref[e + 1]
    g_all = gst_ref[N_EXPERTS]

    def in_copy(q, t):
        rows = pl.ds(tb_ref[GROUP * q + t] * ROW_GROUPS, ROW_GROUPS)
        return pltpu.make_async_copy(xs_hbm.at[rows], xbuf.at[q % 2, t], sem_in.at[q % 2, t])

    def out_copy(q, t):
        rows = pl.ds(tb_ref[GROUP * q + t] * ROW_GROUPS, ROW_GROUPS)
        return pltpu.make_async_copy(ybuf.at[q % 2, t], ys_hbm.at[rows], sem_out.at[q % 2, t])

    def for_group(q, action, make):
        for t in range(GROUP):
            @pl.when(tb_ref[GROUP * q + t] >= 0)
            def _():
                if action == "start":
                    make(q, t).start(priority=1)
                else:
                    make(q, t).wait()

    @pl.when(e == 0)
    def _():
        xbuf[...] = jnp.zeros(xbuf.shape, U32)

        @pl.when(g_all > 0)
        def _():
            for_group(0, "start", in_copy)

    wg_s[...] = wg_ref[0, 0].astype(BF16)
    wu_s[...] = wu_ref[0, 0].astype(BF16)
    wd_s[...] = wd_ref[0, 0].astype(BF16)

    def compute(slot, tiles):
        los, his = [], []
        for j in range(ROW_TILES):
            words = jnp.concatenate(
                [xbuf[slot, t, :, j, :, :].reshape(ROW_TILE, LANES) for t in tiles], axis=0)
            lo, hi = _unpack_halves(words)
            los.append(lo)
            his.append(hi)
        xb = jnp.concatenate(los + his, axis=-1).astype(BF16)
        y = None
        for h0 in range(0, D_EXPERT, D_EXPERT // 2):
            cols = slice(h0, h0 + D_EXPERT // 2)
            g = jnp.dot(xb, wg_s[:, cols], preferred_element_type=F32)
            u = jnp.dot(xb, wu_s[:, cols], preferred_element_type=F32)
            act = (g * (1.0 / (1.0 + jnp.exp(-g))) * u).astype(BF16)
            part = jnp.dot(act, wd_s[cols, :], preferred_element_type=F32)
            y = part if y is None else y + part
        packed = _pack_halves(y.astype(BF16).astype(F32))
        for n, t in enumerate(tiles):
            for j in range(ROW_TILES):
                ybuf[slot, t, :, j, :, :] = packed[n * ROW_TILE:(n + 1) * ROW_TILE,
                                                   j * LANES:(j + 1) * LANES].reshape(
                    ROW_GROUPS, SUBLANES, LANES)

    def body(q, carry):
        slot = q % 2
        for_group(q, "wait", in_copy)

        @pl.when(q + 1 < g_all)
        def _():
            for_group(q + 1, "start", in_copy)

        @pl.when(q >= 2)
        def _():
            for_group(q - 2, "wait", out_copy)

        n_valid = sum((tb_ref[GROUP * q + t] >= 0).astype(I32) for t in range(GROUP))
        for width, narrower in zip(GROUP_WIDTHS, GROUP_WIDTHS[1:] + (0,)):
            @pl.when((n_valid > narrower) & (n_valid <= width))
            def _():
                compute(slot, tuple(range(width)))
        for_group(q, "start", out_copy)
        return carry

    lax.fori_loop(g_first, g_end, body, 0)

    @pl.when(e == N_EXPERTS - 1)
    def _():
        @pl.when(g_all >= 2)
        def _():
            for_group(g_all - 2, "wait", out_copy)

        @pl.when(g_all >= 1)
        def _():
            for_group(g_all - 1, "wait", out_copy)


def _expert_call(layer, group_tiles, group_start, xs4, wg, wu, wd):
    group_buf = (2, GROUP, ROW_GROUPS, ROW_TILES, SUBLANES, LANES)
    return pl.pallas_call(
        _expert_kernel,
        grid_spec=pltpu.PrefetchScalarGridSpec(
            num_scalar_prefetch=2,
            grid=(N_EXPERTS,),
            in_specs=[
                pl.BlockSpec(memory_space=pl.ANY),
                pl.BlockSpec((1, 1, D_MODEL, D_EXPERT), lambda e, tb, gs: (layer, e, 0, 0)),
                pl.BlockSpec((1, 1, D_MODEL, D_EXPERT), lambda e, tb, gs: (layer, e, 0, 0)),
                pl.BlockSpec((1, 1, D_EXPERT, D_MODEL), lambda e, tb, gs: (layer, e, 0, 0)),
            ],
            out_specs=pl.BlockSpec(memory_space=pl.ANY),
            scratch_shapes=[
                pltpu.VMEM((D_MODEL, D_EXPERT), BF16),
                pltpu.VMEM((D_MODEL, D_EXPERT), BF16),
                pltpu.VMEM((D_EXPERT, D_MODEL), BF16),
                pltpu.VMEM(group_buf, U32),
                pltpu.VMEM(group_buf, U32),
                pltpu.SemaphoreType.DMA((2, GROUP)),
                pltpu.SemaphoreType.DMA((2, GROUP)),
            ],
        ),
        out_shape=jax.ShapeDtypeStruct((TOTAL_ROWS // SUBLANES, ROW_TILES, SUBLANES, LANES), U32),
        input_output_aliases={2: 0},
        compiler_params=pltpu.CompilerParams(
            dimension_semantics=("arbitrary",), vmem_limit_bytes=VMEM_LIMIT),
        name="experts",
    )(group_tiles, group_start, xs4, wg, wu, wd)


def _combine_kernel(a0_ref, a1_ref, ys_hbm, x1_ref, w_ref, ada_ref, fg_ref, out_ref,
                    ys_buf, r0_scr, r1_scr, sem, *, final):
    c = pl.program_id(0)
    s = pl.program_id(1)
    base = c * ROUTE_CHUNK + s * TC
    region = CHUNK_ROWS * ROW_TILES
    piece = region // COMBINE_STEPS

    def piece_copy(chunk, p):
        src = ys_hbm.at[pl.ds(chunk * region + p * piece, piece)]
        dst = ys_buf.at[pl.ds((chunk % 2) * region + p * piece, piece)]
        return pltpu.make_async_copy(src, dst, sem.at[chunk % 2, p])

    @pl.when((c == 0) & (s == 0))
    def _():
        for p in range(COMBINE_STEPS):
            piece_copy(0, p).start()
        for p in range(COMBINE_STEPS):
            piece_copy(0, p).wait()

    @pl.when(c + 1 < N_CHUNKS)
    def _():
        for p in range(COMBINE_STEPS):
            @pl.when(s == p)
            def _():
                piece_copy(c + 1, p).start()

    off = (c % 2) * region

    for k in range(TC):
        dst = pl.ds(_row_addr(k), ROW_TILES, stride=SUBLANES)
        r0_scr[dst, :] = ys_buf[pl.ds(off + a0_ref[base + k], ROW_TILES, stride=SUBLANES), :]
        r1_scr[dst, :] = ys_buf[pl.ds(off + a1_ref[base + k], ROW_TILES, stride=SUBLANES), :]

    w = w_ref[...]
    w0 = w[:, 0:1]
    w1 = w[:, 1:2]
    g2 = ada_ref[0][5:6]
    n_grp = TC // SUBLANES
    los, his = [], []
    for j in range(ROW_TILES):
        parts0 = [r0_scr[pl.ds((tb * ROW_TILES + j) * SUBLANES, SUBLANES), :] for tb in range(n_grp)]
        parts1 = [r1_scr[pl.ds((tb * ROW_TILES + j) * SUBLANES, SUBLANES), :] for tb in range(n_grp)]
        lo0, hi0 = _unpack_halves(jnp.concatenate(parts0, axis=0))
        lo1, hi1 = _unpack_halves(jnp.concatenate(parts1, axis=0))
        los.append(w0 * lo0 + w1 * lo1)
        his.append(w0 * hi0 + w1 * hi1)
    x2 = x1_ref[...] + g2 * jnp.concatenate(los + his, axis=-1)
    if final:
        r = lax.rsqrt(jnp.mean(x2 * x2, axis=-1, keepdims=True) + EPS)
        x2 = x2 * r * fg_ref[...]
    out_ref[...] = x2

    @pl.when((s == COMBINE_STEPS - 1) & (c + 1 < N_CHUNKS))
    def _():
        for p in range(COMBINE_STEPS):
            piece_copy(c + 1, p).wait()


def _combine_call(a0, a1, ys_2d, x1, wcol, ada_l, final_g, final):
    tok = lambda c, s, a0, a1: (c * COMBINE_STEPS + s, 0)
    return pl.pallas_call(
        functools.partial(_combine_kernel, final=final),
        grid_spec=pltpu.PrefetchScalarGridSpec(
            num_scalar_prefetch=2,
            grid=(N_CHUNKS, COMBINE_STEPS),
            in_specs=[
                pl.BlockSpec(memory_space=pl.ANY),
                pl.BlockSpec((TC, D_MODEL), tok),
                pl.BlockSpec((TC, 2), tok),
                pl.BlockSpec((1, N_ADA, D_MODEL),
                             lambda c, s, a0, a1: ((c * ROUTE_CHUNK + s * TC) // SEQ, 0, 0)),
                pl.BlockSpec((1, D_MODEL), lambda c, s, a0, a1: (0, 0)),
            ],
            out_specs=pl.BlockSpec((TC, D_MODEL), tok),
            scratch_shapes=[pltpu.VMEM((2 * CHUNK_ROWS * ROW_TILES, LANES), U32),
                            pltpu.VMEM((TC * ROW_TILES, LANES), U32),
                            pltpu.VMEM((TC * ROW_TILES, LANES), U32),
                            pltpu.SemaphoreType.DMA((2, COMBINE_STEPS))],
        ),
        out_shape=jax.ShapeDtypeStruct((N_TOK, D_MODEL), F32),
        compiler_params=pltpu.CompilerParams(
            dimension_semantics=("arbitrary", "arbitrary"), vmem_limit_bytes=VMEM_LIMIT),
        name="combine_final" if final else "combine",
    )(a0, a1, ys_2d, x1, wcol, ada_l, final_g)


def _routing_tables(rt, cnt):
    e0 = rt[0].astype(I32)
    e1 = rt[1].astype(I32)
    rank0 = rt[4].astype(I32)
    rank1 = rt[5].astype(I32)
    counts = cnt[TILES_PER_CHUNK - 1::TILES_PER_CHUNK, :, 0].astype(I32)
    n_tiles = (counts + ROW_TILE - 1) // ROW_TILE
    tile_off = jnp.cumsum(n_tiles, axis=1) - n_tiles
    row_off = tile_off * ROW_TILE
    experts = jnp.arange(N_EXPERTS, dtype=I32)[None, :, None]
    off_tok = row_off[:, :, None]
    pick = lambda e: jnp.sum(
        jnp.where(e.reshape(N_CHUNKS, 1, ROUTE_CHUNK) == experts, off_tok, 0), axis=1).reshape(N_TOK)
    slot0 = pick(e0) + rank0
    slot1 = pick(e1) + rank1
    a0 = _row_addr(slot0)
    a1 = _row_addr(slot1)

    nt_flat = n_tiles.T.reshape(-1)
    ends = jnp.cumsum(nt_flat)
    starts = ends - nt_flat
    total = ends[-1]
    step = jnp.minimum(jnp.arange(MAX_ROW_TILES, dtype=I32), total - 1)
    seg = jnp.sum((ends[None, :] <= step[:, None]).astype(I32), axis=1)
    in_seg = seg[:, None] == jnp.arange(N_EXPERTS * N_CHUNKS, dtype=I32)[None, :]
    k = step - jnp.sum(jnp.where(in_seg, starts[None, :], 0), axis=1)
    c = seg % N_CHUNKS
    tile_off_flat = tile_off.T.reshape(-1)
    tile_blk = c * (CHUNK_ROWS // ROW_TILE) + jnp.sum(jnp.where(in_seg, tile_off_flat[None, :], 0), axis=1) + k
    e_count = jnp.sum(n_tiles, axis=0)
    e_start = jnp.cumsum(e_count) - e_count
    g_count = (e_count + GROUP - 1) // GROUP
    g_end = jnp.cumsum(g_count)
    g_start = g_end - g_count
    slot_id = jnp.arange(MAX_GROUPS * GROUP, dtype=I32)
    q = slot_id // GROUP
    owner = jnp.sum((g_end[None, :] <= q[:, None]).astype(I32), axis=1)
    is_owner = owner[:, None] == jnp.arange(N_EXPERTS, dtype=I32)[None, :]
    sel = lambda v: jnp.sum(jnp.where(is_owner, v[None, :], 0), axis=1)
    k = (q - sel(g_start)) * GROUP + slot_id % GROUP
    valid = (owner < N_EXPERTS) & (k < sel(e_count))
    flat = sel(e_start) + k
    pick_tile = flat[:, None] == jnp.arange(MAX_ROW_TILES, dtype=I32)[None, :]
    group_tiles = jnp.where(valid, jnp.sum(jnp.where(pick_tile, tile_blk[None, :], 0), axis=1), -1)
    group_start = jnp.concatenate([g_start, g_end[-1:]])
    return a0, a1, group_tiles.astype(I32), group_start.astype(I32)


def _layer_consts(l, norm1_g, norm2_g, w_in, conv_w, conv_b, sp_norm_g, sp_w, sp_b, mix_out_g,
                  w_out, shared):
    pos = jnp.arange(SP_BLOCK)
    mask = (pos[None, :] // CHUNK) <= (pos[:, None] // CHUNK)
    w = jnp.where(mask[None], sp_w[l], 0.0)
    w_pairs = jnp.concatenate([w[0::2], w[1::2]], axis=-1).astype(BF16)
    sp_bias = jnp.repeat(sp_b[l].T, HEAD_DIM, axis=1)
    return (
        norm1_g[l].reshape(1, D_MODEL), norm2_g[l].reshape(1, D_MODEL),
        w_in[l].astype(BF16),
        conv_w[l], conv_b[l].reshape(1, D_CONV), sp_norm_g[l].reshape(1, D_SPATIAL),
        w_pairs, sp_bias,
        (mix_out_g[l][:, None] * w_out[l]).astype(BF16),
    ) + shared


def kernel(x, c, w_ada, b_ada, norm1_g, norm2_g, w_in, conv_w, conv_b, sp_norm_g, sp_w, sp_b, mix_out_g, w_out, router_w, router_b, exp_w_gate, exp_w_up, exp_w_down, final_g):
    ada = _ada_table(c, w_ada, b_ada).reshape(DEPTH, BATCH, N_ADA, D_MODEL)

    rw_hi = router_w.astype(BF16)
    rw_lo = (router_w - rw_hi.astype(F32)).astype(BF16)
    rw_cat = jnp.pad(jnp.concatenate([rw_hi, rw_lo], axis=1), ((0, 0), (0, LANES - 2 * N_EXPERTS)))
    head_of = jnp.arange(D_MODEL) // HEAD_DIM
    hred = (head_of[:, None] == jnp.arange(LANES)[None, :]).astype(F32) / HEAD_DIM
    hexp_half = (jnp.arange(LANES)[:, None] == head_of[None, :]).astype(F32)
    hexp = jnp.concatenate([hexp_half, hexp_half], axis=0)
    tri = (jnp.arange(SB)[:, None] < jnp.arange(SB)[None, :])
    shared = (rw_cat, router_b.reshape(N_EXPERTS, 1), hred.astype(BF16), hexp.astype(BF16),
              tri.astype(BF16))
    fg = final_g.reshape(1, D_MODEL)

    xc = x.reshape(N_TOK, D_MODEL)
    for l in range(DEPTH):
        consts = _layer_consts(l, norm1_g, norm2_g, w_in, conv_w, conv_b, sp_norm_g, sp_w, sp_b,
                               mix_out_g, w_out, shared)
        x1, h2p, rt, cnt = _mix_call(xc, ada[l], consts)
        a0, a1, group_tiles, group_start = _routing_tables(rt, cnt)
        xs = _dispatch_call(a0, a1, h2p.reshape(N_TOK * ROW_TILES, LANES))
        ys = _expert_call(l, group_tiles, group_start,
                          xs.reshape(TOTAL_ROWS // SUBLANES, ROW_TILES, SUBLANES, LANES),
                          exp_w_gate, exp_w_up, exp_w_down)
        wcol = rt[2:4].T
        xc = _combine_call(a0, a1, ys.reshape(TOTAL_ROWS * ROW_TILES, LANES), x1, wcol, ada[l], fg,
                           final=(l == DEPTH - 1))
    return xc.reshape(BATCH, SEQ, D_MODEL)
```
